```python
import jax, jax.numpy as jnp
from jax import lax
import numpy as np

D_MODEL = 1024
BATCH = 16
SEQ = 256
DEPTH = 4
DEC_BATCH = 2
DEC_SEQ = 1024
PAST_LEN = 512

GRID_W = 64
HEAD_DIM = 64
N_Q_HEADS = 8
N_KV_HEADS = 2
GROUP = N_Q_HEADS // N_KV_HEADS
ATT_W = N_Q_HEADS * HEAD_DIM
KV_W = N_KV_HEADS * HEAD_DIM
N_FREQ = HEAD_DIM // 4
ROPE_THETA = 10000.0
CONV_W = 512
POOL_W = 512
POOL_SIZES = (2, 4, 8, 16)
POOL_GROUPS = 4
POOL_GC = POOL_W // POOL_GROUPS
N_BRANCH = 4
BRANCH_W = 512
D_FF = -(-(8 * D_MODEL) // (3 * 256)) * 256
BLOCK = 128
WINDOW = 128
EPS = 1e-6
NEG = -1e30

kernel_name = "hybrid_diffusion_prefix_trunk_step"


def _rmsnorm(x, g):
    xf = x.astype(jnp.float32)
    y = xf * lax.rsqrt(jnp.mean(xf * xf, axis=-1, keepdims=True) + EPS)
    return (y * g.astype(jnp.float32)).astype(x.dtype)


def _rope_tables(T):
    rows = T // GRID_W
    row = jnp.repeat(jnp.arange(rows, dtype=jnp.float32), GRID_W)
    col = jnp.tile(jnp.arange(GRID_W, dtype=jnp.float32), rows)
    inv = 1.0 / (ROPE_THETA ** (jnp.arange(N_FREQ, dtype=jnp.float32) / N_FREQ))
    ang = jnp.stack([row[:, None] * inv, col[:, None] * inv], axis=1)
    return jnp.cos(ang)[:, None], jnp.sin(ang)[:, None]


def _rope(x, cos, sin):
    B, T, H, _ = x.shape
    xr = x.astype(jnp.float32).reshape(B, T, H, 2, 2, N_FREQ)
    x1, x2 = xr[..., 0, :], xr[..., 1, :]
    out = jnp.stack([x1 * cos - x2 * sin, x2 * cos + x1 * sin], axis=-2)
    return out.reshape(B, T, H, HEAD_DIM).astype(x.dtype)


def _attend(q, k, v, mask=None, sink=None):
    B, Tq = q.shape[:2]
    qg = q.reshape(B, Tq, N_KV_HEADS, GROUP, HEAD_DIM)
    s = jnp.einsum('bqkgd,bskd->bkgqs', qg, k, preferred_element_type=jnp.float32) * (HEAD_DIM ** -0.5)
    if mask is not None:
        s = jnp.where(mask, s, NEG)
    if sink is not None:
        sk = jnp.broadcast_to(sink.astype(jnp.float32).reshape(1, N_KV_HEADS, GROUP, 1, 1), s.shape[:-1] + (1,))
        p = jax.nn.softmax(jnp.concatenate([s, sk], axis=-1), axis=-1)[..., :-1]
    else:
        p = jax.nn.softmax(s, axis=-1)
    o = jnp.einsum('bkgqs,bskd->bqkgd', p.astype(v.dtype), v)
    return o.reshape(B, Tq, N_Q_HEADS * HEAD_DIM)


def _blocked(q, fn):
    B, T = q.shape[:2]
    nb = T // BLOCK
    qb = q.reshape((B, nb, BLOCK) + q.shape[2:]).swapaxes(0, 1)
    out = lax.map(lambda a: fn(a[0], a[1]), (jnp.arange(nb), qb))
    return out.swapaxes(0, 1).reshape(B, T, -1)


def _windowed(q, k, v, ck, cv, sink):
    B, T = q.shape[:2]
    S_c = ck.shape[1]
    pad = ((0, 0), (BLOCK, BLOCK), (0, 0), (0, 0))
    kp, vp = jnp.pad(k, pad), jnp.pad(v, pad)
    offs = jnp.arange(3 * BLOCK) - BLOCK
    qoff = jnp.arange(BLOCK)
    ctx_ok = jnp.ones((BLOCK, S_c), dtype=bool)

    def blk(i, qb):
        kb = lax.dynamic_slice_in_dim(kp, i * BLOCK, 3 * BLOCK, axis=1)
        vb = lax.dynamic_slice_in_dim(vp, i * BLOCK, 3 * BLOCK, axis=1)
        qpos = i * BLOCK + qoff
        kpos = i * BLOCK + offs
        band = (jnp.abs(qpos[:, None] - kpos[None, :]) <= WINDOW) & ((kpos >= 0) & (kpos < T))[None, :]
        mask = jnp.concatenate([ctx_ok, band], axis=1)
        return _attend(qb, jnp.concatenate([ck, kb], axis=1), jnp.concatenate([cv, vb], axis=1), mask, sink)

    return _blocked(q, blk)


def _short_conv(u, gb, gc, conv_w):
    z = gc * u
    T = z.shape[1]
    zp = jnp.pad(z, ((0, 0), (1, 1), (0, 0)))
    y = zp[:, :T] * conv_w[0] + zp[:, 1:T + 1] * conv_w[1] + zp[:, 2:] * conv_w[2]
    return gb * y


def _pool_mix(p, pool_w, pool_scale):
    B, T, _ = p.shape
    pf = p.astype(jnp.float32).reshape(B, T, POOL_GROUPS, POOL_GC)
    cs = jnp.concatenate([jnp.zeros((B, 1, POOL_GROUPS, POOL_GC), jnp.float32), jnp.cumsum(pf, axis=1)], axis=1)
    t = jnp.arange(T)
    means = []
    for g, w in enumerate(POOL_SIZES):
        lo = jnp.clip(t - w // 2, 0, T)
        hi = jnp.clip(t + w - w // 2, 0, T)
        means.append((cs[:, hi, g] - cs[:, lo, g]) / (hi - lo).astype(jnp.float32)[None, :, None])
    y = (jnp.stack(means, axis=2) - pf).astype(p.dtype)
    y = jnp.einsum('btgc,gce->btge', y, pool_w).reshape(B, T, POOL_W)
    return y * pool_scale


def _project(h, w_in, gq_a, gk_a, gq_c, gk_c):
    B, T, _ = h.shape
    sizes = [ATT_W, KV_W, KV_W, ATT_W, KV_W, KV_W, CONV_W, CONV_W, CONV_W, POOL_W, N_BRANCH * D_MODEL]
    idx = np.cumsum(sizes)[:-1].tolist()
    qa, ka, va, qc, kc, vc, u, gb, gc, pv, gl = jnp.split(h @ w_in, idx, axis=-1)
    heads = lambda t, n: t.reshape(B, T, n, HEAD_DIM)
    qa = _rmsnorm(heads(qa, N_Q_HEADS), gq_a)
    ka = _rmsnorm(heads(ka, N_KV_HEADS), gk_a)
    qc = _rmsnorm(heads(qc, N_Q_HEADS), gq_c)
    kc = _rmsnorm(heads(kc, N_KV_HEADS), gk_c)
    va = heads(va, N_KV_HEADS)
    vc = heads(vc, N_KV_HEADS)
    gates = jax.nn.sigmoid(gl.reshape(B, T, N_BRANCH, D_MODEL))
    return qa, ka, va, qc, kc, vc, u, gb, gc, pv, gates


def _layer(x, cond, prm, ctx):
    (w_mod, b_mod, g_mix, g_ffn, w_in, gq_a, gk_a, gq_c, gk_c, sink,
     conv_w, pool_w, pool_scale, w_br, w_o, w_gu, w_down) = prm
    T = x.shape[1]
    mod = jax.nn.silu(cond) @ w_mod + b_mod
    sh1, sc1, g1, sh2, sc2, g2 = [m[:, None, :] for m in jnp.split(mod, 6, axis=-1)]
    h = _rmsnorm(x, g_mix) * (1 + sc1) + sh1
    qa, ka, va, qc, kc, vc, u, gb, gc, pv, gates = _project(h, w_in, gq_a, gk_a, gq_c, gk_c)
    if ctx is None:
        new = (ka, va, kc, vc)
        att_a = _blocked(qa, lambda i, qb: _attend(qb, ka, va))
        att_c = _blocked(qc, lambda i, qb: _attend(qb, kc, vc, sink=sink))
    else:
        cka, cva, ckc, cvc = ctx
        cos, sin = _rope_tables(T)
        qa, ka, qc, kc = (_rope(t, cos, sin) for t in (qa, ka, qc, kc))
        k_all = jnp.concatenate([cka, ka], axis=1)
        v_all = jnp.concatenate([cva, va], axis=1)
        att_a = _blocked(qa, lambda i, qb: _attend(qb, k_all, v_all))
        att_c = _windowed(qc, kc, vc, ckc, cvc, sink)
        new = None
    conv = _short_conv(u, gb, gc, conv_w)
    pool = _pool_mix(pv, pool_w, pool_scale)
    br = jnp.stack([att_a, conv, att_c, pool], axis=2)
    proj = jnp.einsum('btkc,kcd->btkd', br, w_br)
    x = x + g1 * (jnp.sum(gates * proj, axis=2) @ w_o)
    h2 = _rmsnorm(x, g_ffn) * (1 + sc2) + sh2
    a, b = jnp.split(h2 @ w_gu, 2, axis=-1)
    x = x + g2 * ((jax.nn.silu(a) * b) @ w_down)
    return x, new


def setup_inputs(seed: int = 0) -> dict:
    key = jax.random.key(seed)
    ks = jax.random.split(key, 32)
    n = lambda k, s, sc: jax.random.normal(k, s, jnp.float32) * sc
    n_in = 2 * ATT_W + 4 * KV_W + 3 * CONV_W + POOL_W + N_BRANCH * D_MODEL
    cshape = (DEC_BATCH, DEPTH, PAST_LEN, N_KV_HEADS, HEAD_DIM)
    return {
        'x_prompt': n(ks[0], (BATCH, SEQ, D_MODEL), 1.0),
        'x_sample': n(ks[1], (DEC_BATCH, DEC_SEQ, D_MODEL), 1.0),
        'cache_k_attn': n(ks[2], cshape, 1.0),
        'cache_v_attn': n(ks[3], cshape, 1.0),
        'cache_k_win': n(ks[4], cshape, 1.0),
        'cache_v_win': n(ks[5], cshape, 1.0),
        'c': n(ks[6], (DEC_BATCH, D_MODEL), 1.0),
        'c_ctx': n(ks[7], (D_MODEL,), 1.0),
        'w_mod': n(ks[8], (DEPTH, D_MODEL, 6 * D_MODEL), 0.5 * D_MODEL ** -0.5),
        'b_mod': n(ks[9], (DEPTH, 6 * D_MODEL), 0.01),
        'g_mix': 1.0 + n(ks[10], (DEPTH, D_MODEL), 0.02),
        'g_ffn': 1.0 + n(ks[11], (DEPTH, D_MODEL), 0.02),
        'w_in': n(ks[12], (DEPTH, D_MODEL, n_in), D_MODEL ** -0.5),
        'gq_attn': 1.0 + n(ks[13], (DEPTH, HEAD_DIM), 0.02),
        'gk_attn': 1.0 + n(ks[14], (DEPTH, HEAD_DIM), 0.02),
        'gq_win': 1.0 + n(ks[15], (DEPTH, HEAD_DIM), 0.02),
        'gk_win': 1.0 + n(ks[16], (DEPTH, HEAD_DIM), 0.02),
        'sink_win': n(ks[17], (DEPTH, N_Q_HEADS), 0.5),
        'conv_w': n(ks[18], (DEPTH, 3, CONV_W), 3 ** -0.5),
        'pool_w': n(ks[19], (DEPTH, POOL_GROUPS, POOL_GC, POOL_GC), POOL_GC ** -0.5),
        'pool_scale': 1.0 + n(ks[20], (DEPTH, POOL_W), 0.1),
        'w_branch': n(ks[21], (DEPTH, N_BRANCH, BRANCH_W, D_MODEL), BRANCH_W ** -0.5),
        'w_out': n(ks[22], (DEPTH, D_MODEL, D_MODEL), D_MODEL ** -0.5),
        'w_gate_up': n(ks[23], (DEPTH, D_MODEL, 2 * D_FF), D_MODEL ** -0.5),
        'w_down': n(ks[24], (DEPTH, D_FF, D_MODEL), D_FF ** -0.5),
    }


def reference(x_prompt, x_sample, cache_k_attn, cache_v_attn, cache_k_win, cache_v_win, c, c_ctx,
              w_mod, b_mod, g_mix, g_ffn, w_in, gq_attn, gk_attn, gq_win, gk_win, sink_win,
              conv_w, pool_w, pool_scale, w_branch, w_out, w_gate_up, w_down):
    cond_ctx = jnp.broadcast_to(c_ctx, (x_prompt.shape[0], c_ctx.shape[0]))
    xp, xs = x_prompt, x_sample
    ka_l, va_l, kc_l, vc_l = [], [], [], []
    for l in range(DEPTH):
        prm = (w_mod[l], b_mod[l], g_mix[l], g_ffn[l], w_in[l], gq_attn[l], gk_attn[l],
               gq_win[l], gk_win[l], sink_win[l], conv_w[l], pool_w[l], pool_scale[l],
               w_branch[l], w_out[l], w_gate_up[l], w_down[l])
        xp, (ka, va, kc, vc) = _layer(xp, cond_ctx, prm, None)
        ka_l.append(ka)
        va_l.append(va)
        kc_l.append(kc)
        vc_l.append(vc)
        ctx = (cache_k_attn[:, l], cache_v_attn[:, l], cache_k_win[:, l], cache_v_win[:, l])
        xs, _ = _layer(xs, c, prm, ctx)
    k_attn = jnp.stack(ka_l, axis=1)
    v_attn = jnp.stack(va_l, axis=1)
    k_win = jnp.stack(kc_l, axis=1)
    v_win = jnp.stack(vc_l, axis=1)
    return (xp, xs, k_attn, v_attn, k_win, v_win)
```

```python
import functools

import numpy as np
import jax
import jax.numpy as jnp
from jax import lax
from jax.experimental import pallas as pl
from jax.experimental.pallas import tpu as pltpu

D_MODEL = 1024
BATCH = 16
SEQ = 256
DEPTH = 4
DEC_BATCH = 2
DEC_SEQ = 1024
PAST_LEN = 512
GRID_W = 64
HEAD_DIM = 64
N_Q_HEADS = 8
N_KV_HEADS = 2
ATT_W = N_Q_HEADS * HEAD_DIM
KV_W = N_KV_HEADS * HEAD_DIM
N_FREQ = HEAD_DIM // 4
ROPE_THETA = 10000.0
CONV_W = 512
POOL_W = 512
POOL_SIZES = (2, 4, 8, 16)
N_BRANCH = 4
BRANCH_W = 512
D_FF = 2816
WINDOW = 128
EPS = 1e-6
NEG = -1e30

P_ROWS = BATCH * SEQ
S_ROWS = DEC_BATCH * DEC_SEQ
ROWS = P_ROWS + S_ROWS
CHUNK = 1024
N_CHUNKS = ROWS // CHUNK
N_P_CHUNKS = P_ROWS // CHUNK
TN = 256
LANES = 128
VMEM_LIMIT = 56 * 1024 * 1024

COL_QA, COL_KVA, COL_QC, COL_KVC = 0, 2, 3, 5
COL_U, COL_GB, COL_GC, COL_PV, COL_GATE = 6, 8, 10, 12, 14

F32 = jnp.float32
BF16 = jnp.bfloat16


def _params(n_grid):
    return pltpu.CompilerParams(dimension_semantics=("arbitrary",) * n_grid,
                                vmem_limit_bytes=VMEM_LIMIT)


def _resident(shape, index_map):
    return pl.BlockSpec(shape, index_map, pipeline_mode=pl.Buffered(1))


def _dot(a, b):
    return jnp.dot(a, b, preferred_element_type=F32)


def _dot_nt(a, b):
    return lax.dot_general(a, b, (((1,), (1,)), ((), ())), preferred_element_type=F32)


def _sigmoid(x):
    return 1.0 / (1.0 + jnp.exp(-x))


def _rows(c):
    return pl.ds(pl.multiple_of(c * CHUNK, CHUNK), CHUNK)


def _for_each_chunk(body):
    def step(c, carry):
        body(c, False)
        return carry
    lax.fori_loop(0, N_P_CHUNKS, step, 0)
    for c in range(N_P_CHUNKS, N_CHUNKS):
        body(c, True)


def _head_segments(width):
    r = lax.broadcasted_iota(jnp.int32, (width, width), 0) // HEAD_DIM
    c = lax.broadcasted_iota(jnp.int32, (width, width), 1) // HEAD_DIM
    return jnp.where(r == c, 1.0, 0.0).astype(BF16)


def _head_mean_square(y, seg):
    sq = y * y
    hi = sq.astype(BF16)
    lo = (sq - hi.astype(F32)).astype(BF16)
    return (_dot(hi, seg) + _dot(lo, seg)) * (1.0 / HEAD_DIM)


def _swap16(x):
    lane = lax.broadcasted_iota(jnp.int32, x.shape, 1)
    fwd = pltpu.roll(x, LANES - 16, axis=1)
    bwd = pltpu.roll(x, 16, axis=1)
    return jnp.where((lane & 16) == 0, fwd, bwd)


def _rope128(x, cos, sin):
    return x * cos + _swap16(x) * sin


def _mod_kernel(cond_ref, w_ref, b_ref, o_ref):
    c = cond_ref[...]
    s = (c * _sigmoid(c)).astype(BF16)
    o_ref[...] = _dot(s, w_ref[...].astype(BF16)) + b_ref[...]


def _modulation(cond8, w_mod, b_mod):
    tn = 1536
    n = 6 * D_MODEL
    return pl.pallas_call(
        _mod_kernel,
        grid=(DEPTH, n // tn),
        in_specs=[pl.BlockSpec((8, D_MODEL), lambda l, j: (0, 0)),
                  pl.BlockSpec((None, D_MODEL, tn), lambda l, j: (l, 0, j)),
                  pl.BlockSpec((None, 1, tn), lambda l, j: (l, 0, j))],
        out_specs=pl.BlockSpec((None, 8, tn), lambda l, j: (l, 0, j)),
        out_shape=jax.ShapeDtypeStruct((DEPTH, 8, n), F32),
        compiler_params=_params(2),
        name="modulation",
    )(cond8, w_mod, b_mod.reshape(DEPTH, 1, n))


def _norm_kernel(x_ref, g_ref, sc_ref, sh_ref, o_ref):
    x = x_ref[...]
    y = x * lax.rsqrt(jnp.mean(x * x, axis=-1, keepdims=True) + EPS)
    o_ref[...] = ((y * g_ref[...]) * (1.0 + sc_ref[...]) + sh_ref[...]).astype(BF16)


def _norm_mod(x, g, modc, l, k_shift, k_scale, name):
    return pl.pallas_call(
        _norm_kernel,
        grid=(N_CHUNKS,),
        in_specs=[pl.BlockSpec((CHUNK, D_MODEL), lambda i: (i, 0)),
                  pl.BlockSpec((None, 1, D_MODEL), lambda i: (l, 0, 0)),
                  pl.BlockSpec((None, None, 1, D_MODEL), lambda i: (l, i, 0, k_scale)),
                  pl.BlockSpec((None, None, 1, D_MODEL), lambda i: (l, i, 0, k_shift))],
        out_specs=pl.BlockSpec((CHUNK, D_MODEL), lambda i: (i, 0)),
        out_shape=jax.ShapeDtypeStruct((ROWS, D_MODEL), BF16),
        compiler_params=_params(1),
        name=name,
    )(x, g.reshape(DEPTH, 1, D_MODEL), modc, modc)


def _proj_q_kernel(h_ref, w_ref, g_ref, cos_ref, sin_ref, o_ref):
    w = w_ref[...].astype(BF16)
    g = g_ref[...]
    seg = _head_segments(TN)

    def body(c, latent):
        rows = _rows(c)
        y = _dot(h_ref[rows, :], w)
        y = y * lax.rsqrt(_head_mean_square(y, seg) + EPS) * g
        for s in range(TN // LANES):
            part = y[:, s * LANES:(s + 1) * LANES]
            if latent:
                part = _rope128(part, cos_ref[...], sin_ref[...])
            o_ref[rows, s * LANES:(s + 1) * LANES] = (part * (HEAD_DIM ** -0.5)).astype(BF16)

    _for_each_chunk(body)


def _proj_q(h, w_in, gq2, cos, sin, l):
    return pl.pallas_call(
        _proj_q_kernel,
        grid=(4,),
        in_specs=[_resident((ROWS, D_MODEL), lambda j: (0, 0)),
                  pl.BlockSpec((None, D_MODEL, TN), lambda j: (l, 0, j + j // 2)),
                  pl.BlockSpec((None, None, 1, TN), lambda j: (l, j // 2, 0, 0)),
                  _resident((DEC_SEQ, LANES), lambda j: (0, 0)),
                  _resident((DEC_SEQ, LANES), lambda j: (0, 0))],
        out_specs=pl.BlockSpec((ROWS, TN), lambda j: (0, j)),
        out_shape=jax.ShapeDtypeStruct((ROWS, 2 * ATT_W), BF16),
        compiler_params=_params(1),
        name="proj_q",
    )(h, w_in, gq2, cos, sin)


def _proj_kv_kernel(h_ref, w_ref, g_ref, cos_ref, sin_ref, o_ref, new_ref):
    w = w_ref[...].astype(BF16)
    g = g_ref[...]
    seg = _head_segments(KV_W)

    def body(c, latent):
        rows = _rows(c)
        y = _dot(h_ref[rows, :], w)
        k = y[:, :KV_W]
        v = y[:, KV_W:]
        k = k * lax.rsqrt(_head_mean_square(k, seg) + EPS) * g
        if latent:
            k = _rope128(k, cos_ref[...], sin_ref[...])
        else:
            new_ref[rows, :KV_W] = k
            new_ref[rows, KV_W:] = v
        o_ref[rows, :KV_W] = k.astype(BF16)
        o_ref[rows, KV_W:] = v.astype(BF16)

    _for_each_chunk(body)


def _proj_kv(h, w_in, gk2, cos, sin, l):
    return pl.pallas_call(
        _proj_kv_kernel,
        grid=(2,),
        in_specs=[_resident((ROWS, D_MODEL), lambda j: (0, 0)),
                  pl.BlockSpec((None, D_MODEL, TN), lambda j: (l, 0, COL_KVA + 3 * j)),
                  pl.BlockSpec((None, None, 1, KV_W), lambda j: (l, j, 0, 0)),
                  _resident((DEC_SEQ, LANES), lambda j: (0, 0)),
                  _resident((DEC_SEQ, LANES), lambda j: (0, 0))],
        out_specs=[pl.BlockSpec((ROWS, TN), lambda j: (0, j)),
                   pl.BlockSpec((P_ROWS, TN), lambda j: (0, j))],
        out_shape=[jax.ShapeDtypeStruct((ROWS, 4 * KV_W), BF16),
                   jax.ShapeDtypeStruct((P_ROWS, 4 * KV_W), F32)],
        compiler_params=_params(1),
        name="proj_kv",
    )(h, w_in, gk2, cos, sin)


def _seq_pos(latent, width):
    row = lax.broadcasted_iota(jnp.int32, (CHUNK, width), 0)
    return row if latent else row & (SEQ - 1)


def _shift_rows(x, k, pos, seq_len):
    rolled = pltpu.roll(x, (-k) % CHUNK, axis=0)
    ok = (pos + k >= 0) & (pos + k < seq_len)
    return jnp.where(ok, rolled, 0.0)


def _proj_conv_kernel(h_ref, wu_ref, wb_ref, wc_ref, cw_ref, o_ref):
    wu = wu_ref[...].astype(BF16)
    wb = wb_ref[...].astype(BF16)
    wc = wc_ref[...].astype(BF16)
    cw = cw_ref[...]

    def body(c, latent):
        rows = _rows(c)
        h = h_ref[rows, :]
        seq_len = DEC_SEQ if latent else SEQ
        pos = _seq_pos(latent, TN)
        z = _dot(h, wc) * _dot(h, wu)
        y = (_shift_rows(z, -1, pos, seq_len) * cw[0:1, :] + z * cw[1:2, :]
             + _shift_rows(z, 1, pos, seq_len) * cw[2:3, :])
        o_ref[rows, :] = (_dot(h, wb) * y).astype(BF16)

    _for_each_chunk(body)


def _proj_conv(h, w_in, conv_w, l):
    return pl.pallas_call(
        _proj_conv_kernel,
        grid=(CONV_W // TN,),
        in_specs=[_resident((ROWS, D_MODEL), lambda j: (0, 0)),
                  pl.BlockSpec((None, D_MODEL, TN), lambda j: (l, 0, COL_U + j)),
                  pl.BlockSpec((None, D_MODEL, TN), lambda j: (l, 0, COL_GB + j)),
                  pl.BlockSpec((None, D_MODEL, TN), lambda j: (l, 0, COL_GC + j)),
                  pl.BlockSpec((None, 3, TN), lambda j: (l, 0, j))],
        out_specs=pl.BlockSpec((ROWS, TN), lambda j: (0, j)),
        out_shape=jax.ShapeDtypeStruct((ROWS, CONV_W), BF16),
        compiler_params=_params(1),
        name="proj_conv",
    )(h, w_in, w_in, w_in, conv_w)


def _window_mean_minus_token(p, half, pos, seq_len):
    fwd = p
    bwd = _shift_rows(p, -1, pos, seq_len)
    m = 1
    while m < half:
        fwd = fwd + _shift_rows(fwd, m, pos, seq_len)
        bwd = bwd + _shift_rows(bwd, -m, pos, seq_len)
        m *= 2
    count = jnp.minimum(pos + half, seq_len) - jnp.maximum(pos - half, 0)
    return (fwd + bwd) / count.astype(F32) - p


def _proj_pool_kernel(h_ref, w_ref, pw_ref, ps_ref, o_ref):
    w = w_ref[...].astype(BF16)

    def body(c, latent):
        rows = _rows(c)
        p = _dot(h_ref[rows, :], w)
        seq_len = DEC_SEQ if latent else SEQ
        pos = _seq_pos(latent, LANES)
        for grp, size in enumerate(POOL_SIZES):
            cols = slice(grp * LANES, (grp + 1) * LANES)
            y = _window_mean_minus_token(p[:, cols], size // 2, pos, seq_len).astype(BF16)
            o_ref[rows, cols] = (_dot(y, pw_ref[grp].astype(BF16)) * ps_ref[:, cols]).astype(BF16)

    _for_each_chunk(body)


def _proj_pool(h, w_in, pool_w, pool_scale, l):
    return pl.pallas_call(
        _proj_pool_kernel,
        grid=(1,),
        in_specs=[_resident((ROWS, D_MODEL), lambda j: (0, 0)),
                  pl.BlockSpec((None, D_MODEL, POOL_W), lambda j: (l, 0, COL_PV * TN // POOL_W)),
                  pl.BlockSpec((None, len(POOL_SIZES), LANES, LANES), lambda j: (l, 0, 0, 0)),
                  pl.BlockSpec((None, 1, POOL_W), lambda j: (l, 0, 0))],
        out_specs=pl.BlockSpec((ROWS, POOL_W), lambda j: (0, 0)),
        out_shape=jax.ShapeDtypeStruct((ROWS, POOL_W), BF16),
        compiler_params=_params(1),
        name="proj_pool",
    )(h, w_in, pool_w, pool_scale.reshape(DEPTH, 1, POOL_W))


def _split_heads(x):
    xr = pltpu.roll(x, HEAD_DIM, axis=1)
    low = lax.broadcasted_iota(jnp.int32, x.shape, 1) < HEAD_DIM
    return (jnp.where(low, x, 0.0).astype(BF16), jnp.where(low, 0.0, xr).astype(BF16),
            jnp.where(low, xr, 0.0).astype(BF16), jnp.where(low, 0.0, x).astype(BF16))


def _attend_pair(q, pieces, sinks):
    out = None
    for parity in range(2):
        scores = []
        for k_lo, k_hi, _, _, mask in pieces:
            s = _dot_nt(q, k_hi if parity else k_lo)
            if mask is not None:
                s = jnp.where(mask, s, NEG)
            scores.append(s)
        m = functools.reduce(jnp.maximum, [jnp.max(s, axis=1, keepdims=True) for s in scores])
        if sinks is not None:
            m = jnp.maximum(m, sinks[parity])
        denom = None
        acc = None
        for s, (_, _, v_lo, v_hi, _) in zip(scores, pieces):
            p = jnp.exp(s - m)
            d = jnp.sum(p, axis=1, keepdims=True)
            a = _dot(p.astype(BF16), v_hi if parity else v_lo)
            denom = d if denom is None else denom + d
            acc = a if acc is None else acc + a
        if sinks is not None:
            denom = denom + jnp.exp(sinks[parity] - m)
        acc = acc / denom
        out = acc if out is None else out + acc
    return out


def _attn_prompt_kernel(sink_ref, q_ref, kv_ref, o_ref, *, layer):
    for mixer in range(2):
        k = _split_heads(kv_ref[:, mixer * 2 * KV_W:mixer * 2 * KV_W + KV_W].astype(F32))
        v = _split_heads(kv_ref[:, mixer * 2 * KV_W + KV_W:(mixer + 1) * 2 * KV_W].astype(F32))
        for pair in range(N_Q_HEADS // 2):
            kvh = pair // 2
            piece = (k[2 * kvh], k[2 * kvh + 1], v[2 * kvh], v[2 * kvh + 1], None)
            sinks = None
            if mixer == 1:
                sinks = (sink_ref[layer, 2 * pair], sink_ref[layer, 2 * pair + 1])
            col = mixer * ATT_W + pair * LANES
            o_ref[:, col:col + LANES] = _attend_pair(q_ref[:, col:col + LANES], [piece], sinks).astype(BF16)


def _attn_prompt(sink, q, kv, l):
    return pl.pallas_call(
        functools.partial(_attn_prompt_kernel, layer=l),
        grid=(BATCH,),
        in_specs=[pl.BlockSpec(memory_space=pltpu.SMEM),
                  pl.BlockSpec((SEQ, 2 * ATT_W), lambda b: (b, 0)),
                  pl.BlockSpec((SEQ, 4 * KV_W), lambda b: (b, 0))],
        out_specs=pl.BlockSpec((SEQ, 2 * ATT_W), lambda b: (b, 0)),
        out_shape=jax.ShapeDtypeStruct((ROWS, 2 * ATT_W), BF16),
        compiler_params=_params(1),
        name="attn_prompt",
    )(sink, q, kv)


TQ = 256
A_KEYS = PAST_LEN + DEC_SEQ
WIN_KEYS = TQ + 2 * WINDOW
C_ROWS = PAST_LEN + WINDOW + DEC_SEQ + WINDOW


def _attn_latent_kernel(sink_ref, q_ref, kv_ref, cka_ref, cva_ref, ckc_ref, cvc_ref, att_in_ref,
                        o_ref, ka_s, va_s, kc_s, vc_s, *, layer):
    del att_in_ref
    qt = pl.program_id(1)

    @pl.when(qt == 0)
    def _fill():
        for dst, ctx_ref, col in ((ka_s, cka_ref, 0), (va_s, cva_ref, KV_W)):
            for i, part in enumerate(_split_heads(ctx_ref[...])):
                dst[i, 0:PAST_LEN, :] = part
            for i, part in enumerate(_split_heads(kv_ref[:, col:col + KV_W].astype(F32))):
                dst[i, PAST_LEN:A_KEYS, :] = part
        for dst, ctx_ref, col in ((kc_s, ckc_ref, 2 * KV_W), (vc_s, cvc_ref, 3 * KV_W)):
            zeros = jnp.zeros((WINDOW, LANES), BF16)
            for i, part in enumerate(_split_heads(ctx_ref[...])):
                dst[i, 0:PAST_LEN, :] = part
                dst[i, PAST_LEN:PAST_LEN + WINDOW, :] = zeros
                dst[i, C_ROWS - WINDOW:C_ROWS, :] = zeros
            for i, part in enumerate(_split_heads(kv_ref[:, col:col + KV_W].astype(F32))):
                dst[i, PAST_LEN + WINDOW:PAST_LEN + WINDOW + DEC_SEQ, :] = part

    q0 = qt * TQ
    win = pl.ds(pl.multiple_of(PAST_LEN + q0, TQ), WIN_KEYS)
    r = lax.broadcasted_iota(jnp.int32, (TQ, WIN_KEYS), 0)
    jk = lax.broadcasted_iota(jnp.int32, (TQ, WIN_KEYS), 1)
    kpos = q0 - WINDOW + jk
    band = jnp.where((jk - r >= 0) & (jk - r <= 2 * WINDOW), kpos, -1)
    mask = (band >= 0) & (band < DEC_SEQ)

    for pair in range(N_Q_HEADS // 2):
        kvh = pair // 2
        lo, hi = 2 * kvh, 2 * kvh + 1
        col = pair * LANES
        piece = (ka_s[lo], ka_s[hi], va_s[lo], va_s[hi], None)
        o_ref[:, col:col + LANES] = _attend_pair(q_ref[:, col:col + LANES], [piece], None).astype(BF16)

        col = ATT_W + pair * LANES
        ctx = (kc_s[lo, 0:PAST_LEN, :], kc_s[hi, 0:PAST_LEN, :],
               vc_s[lo, 0:PAST_LEN, :], vc_s[hi, 0:PAST_LEN, :], None)
        near = (kc_s[lo, win, :], kc_s[hi, win, :], vc_s[lo, win, :], vc_s[hi, win, :], mask)
        sinks = (sink_ref[layer, 2 * pair], sink_ref[layer, 2 * pair + 1])
        o_ref[:, col:col + LANES] = _attend_pair(q_ref[:, col:col + LANES], [ctx, near], sinks).astype(BF16)


def _attn_latent(sink, q, kv, cka, cva, ckc, cvc, att, l):
    n_qt = DEC_SEQ // TQ
    p_tiles = P_ROWS // TQ
    cache_spec = pl.BlockSpec((None, None, PAST_LEN, KV_W), lambda b, t: (b, l, 0, 0))
    return pl.pallas_call(
        functools.partial(_attn_latent_kernel, layer=l),
        grid=(DEC_BATCH, n_qt),
        in_specs=[pl.BlockSpec(memory_space=pltpu.SMEM),
                  pl.BlockSpec((TQ, 2 * ATT_W), lambda b, t: (p_tiles + b * n_qt + t, 0)),
                  pl.BlockSpec((DEC_SEQ, 4 * KV_W), lambda b, t: (N_P_CHUNKS + b, 0)),
                  cache_spec, cache_spec, cache_spec, cache_spec,
                  pl.BlockSpec(memory_space=pl.ANY)],
        out_specs=pl.BlockSpec((TQ, 2 * ATT_W), lambda b, t: (p_tiles + b * n_qt + t, 0)),
        out_shape=jax.ShapeDtypeStruct((ROWS, 2 * ATT_W), BF16),
        scratch_shapes=[pltpu.VMEM((4, A_KEYS, LANES), BF16), pltpu.VMEM((4, A_KEYS, LANES), BF16),
                        pltpu.VMEM((4, C_ROWS, LANES), BF16), pltpu.VMEM((4, C_ROWS, LANES), BF16)],
        input_output_aliases={7: 0},
        compiler_params=_params(2),
        name="attn_latent",
    )(sink, q, kv, cka, cva, ckc, cvc, att)


M_ROWS = 2048


def _merge_kernel(h_ref, att_ref, conv_ref, pool_ref, wg0, wg1, wg2, wg3, wb0, wb1, wb2, wb3, o_ref):
    h = h_ref[...]
    branches = (att_ref[:, :ATT_W], conv_ref[...], att_ref[:, ATT_W:], pool_ref[...])
    acc = None
    for br, wg, wb in zip(branches, (wg0, wg1, wg2, wg3), (wb0, wb1, wb2, wb3)):
        gate = _sigmoid(_dot(h, wg[...].astype(BF16)))
        term = gate * _dot(br, wb[...].astype(BF16))
        acc = term if acc is None else acc + term
    o_ref[...] = acc.astype(BF16)


def _merge(h, att, conv, pool, w_in, w_branch, l):
    gate_specs = [pl.BlockSpec((None, D_MODEL, TN), functools.partial(
        lambda i, n, k: (l, 0, COL_GATE + (D_MODEL // TN) * k + n), k=k)) for k in range(N_BRANCH)]
    br_specs = [pl.BlockSpec((None, None, BRANCH_W, TN), functools.partial(
        lambda i, n, k: (l, k, 0, n), k=k)) for k in range(N_BRANCH)]
    return pl.pallas_call(
        _merge_kernel,
        grid=(ROWS // M_ROWS, D_MODEL // TN),
        in_specs=[pl.BlockSpec((M_ROWS, D_MODEL), lambda i, n: (i, 0)),
                  pl.BlockSpec((M_ROWS, 2 * ATT_W), lambda i, n: (i, 0)),
                  pl.BlockSpec((M_ROWS, CONV_W), lambda i, n: (i, 0)),
                  pl.BlockSpec((M_ROWS, POOL_W), lambda i, n: (i, 0))] + gate_specs + br_specs,
        out_specs=pl.BlockSpec((M_ROWS, TN), lambda i, n: (i, n)),
        out_shape=jax.ShapeDtypeStruct((ROWS, D_MODEL), BF16),
        compiler_params=_params(2),
        name="merge",
    )(h, att, conv, pool, *([w_in] * N_BRANCH), *([w_branch] * N_BRANCH))


def _residual_kernel(a_ref, w_ref, x_ref, g_ref, o_ref):
    w = w_ref[...].astype(BF16)

    def body(c, latent):
        del latent
        rows = _rows(c)
        o_ref[rows, :] = x_ref[rows, :] + g_ref[c] * _dot(a_ref[rows, :], w)

    _for_each_chunk(body)


def _out_proj(mixed, w_out, x, modc, l):
    return pl.pallas_call(
        _residual_kernel,
        grid=(D_MODEL // TN,),
        in_specs=[_resident((ROWS, D_MODEL), lambda n: (0, 0)),
                  pl.BlockSpec((None, D_MODEL, TN), lambda n: (l, 0, n)),
                  pl.BlockSpec((ROWS, TN), lambda n: (0, n)),
                  pl.BlockSpec((None, N_CHUNKS, 1, TN), lambda n: (l, 0, 0, 2 * (D_MODEL // TN) + n))],
        out_specs=pl.BlockSpec((ROWS, TN), lambda n: (0, n)),
        out_shape=jax.ShapeDtypeStruct((ROWS, D_MODEL), F32),
        compiler_params=_params(1),
        name="out_proj",
    )(mixed, w_out, x, modc)


def _ffn_up_kernel(h_ref, wa_ref, wb_ref, o_ref):
    wa = wa_ref[...].astype(BF16)
    wb = wb_ref[...].astype(BF16)

    def body(c, latent):
        del latent
        rows = _rows(c)
        h = h_ref[rows, :]
        a = _dot(h, wa)
        o_ref[rows, :] = (a * _sigmoid(a) * _dot(h, wb)).astype(BF16)

    _for_each_chunk(body)


def _ffn_up(h, w_gate_up, l):
    n = D_FF // TN
    return pl.pallas_call(
        _ffn_up_kernel,
        grid=(n,),
        in_specs=[_resident((ROWS, D_MODEL), lambda j: (0, 0)),
                  pl.BlockSpec((None, D_MODEL, TN), lambda j: (l, 0, j)),
                  pl.BlockSpec((None, D_MODEL, TN), lambda j: (l, 0, n + j))],
        out_specs=pl.BlockSpec((ROWS, TN), lambda j: (0, j)),
        out_shape=jax.ShapeDtypeStruct((ROWS, D_FF), BF16),
        compiler_params=_params(1),
        name="ffn_up",
    )(h, w_gate_up, w_gate_up)


def _ffn_down_kernel(a_ref, w_ref, x_ref, g_ref, o_ref):
    o_ref[...] = x_ref[...] + g_ref[...] * _dot(a_ref[...], w_ref[...].astype(BF16))


def _ffn_down(act, w_down, x, modc, l):
    return pl.pallas_call(
        _ffn_down_kernel,
        grid=(N_CHUNKS,),
        in_specs=[pl.BlockSpec((CHUNK, D_FF), lambda i: (i, 0)),
                  _resident((None, D_FF, D_MODEL), lambda i: (l, 0, 0)),
                  pl.BlockSpec((CHUNK, D_MODEL), lambda i: (i, 0)),
                  pl.BlockSpec((None, None, 1, D_MODEL), lambda i: (l, i, 0, 5))],
        out_specs=pl.BlockSpec((CHUNK, D_MODEL), lambda i: (i, 0)),
        out_shape=jax.ShapeDtypeStruct((ROWS, D_MODEL), F32),
        compiler_params=_params(1),
        name="ffn_down",
    )(act, w_down, x, modc)


def _rope_tables():
    rows = DEC_SEQ // GRID_W
    row = jnp.repeat(jnp.arange(rows, dtype=F32), GRID_W)
    col = jnp.tile(jnp.arange(GRID_W, dtype=F32), rows)
    inv = 1.0 / (ROPE_THETA ** (jnp.arange(N_FREQ, dtype=F32) / N_FREQ))
    cr, sr = jnp.cos(row[:, None] * inv), jnp.sin(row[:, None] * inv)
    cc, sc = jnp.cos(col[:, None] * inv), jnp.sin(col[:, None] * inv)
    cos = jnp.concatenate([cr, cr, cc, cc], axis=1)
    sin = jnp.concatenate([-sr, sr, -sc, sc], axis=1)
    reps = LANES // HEAD_DIM
    return jnp.tile(cos, (1, reps)), jnp.tile(sin, (1, reps))


def kernel(x_prompt, x_sample, cache_k_attn, cache_v_attn, cache_k_win, cache_v_win, c, c_ctx,
           w_mod, b_mod, g_mix, g_ffn, w_in, gq_attn, gk_attn, gq_win, gk_win, sink_win,
           conv_w, pool_w, pool_scale, w_branch, w_out, w_gate_up, w_down):
    x = jnp.concatenate([x_prompt.reshape(P_ROWS, D_MODEL), x_sample.reshape(S_ROWS, D_MODEL)], axis=0)

    cond8 = jnp.zeros((8, D_MODEL), F32).at[0].set(c_ctx).at[1:1 + DEC_BATCH].set(c)
    mod = _modulation(cond8, w_mod, b_mod)
    chunk_cond = np.array([0] * N_P_CHUNKS + [1 + b for b in range(DEC_BATCH)])
    modc = mod[:, chunk_cond].reshape(DEPTH, N_CHUNKS, 1, 6 * D_MODEL)

    cos, sin = _rope_tables()
    gq2 = jnp.stack([jnp.tile(gq_attn, (1, TN // HEAD_DIM)), jnp.tile(gq_win, (1, TN // HEAD_DIM))],
                    axis=1).reshape(DEPTH, 2, 1, TN)
    gk2 = jnp.stack([jnp.tile(gk_attn, (1, KV_W // HEAD_DIM)), jnp.tile(gk_win, (1, KV_W // HEAD_DIM))],
                    axis=1).reshape(DEPTH, 2, 1, KV_W)
    caches = [a.reshape(DEC_BATCH, DEPTH, PAST_LEN, KV_W)
              for a in (cache_k_attn, cache_v_attn, cache_k_win, cache_v_win)]

    new_kv = []
    for l in range(DEPTH):
        h = _norm_mod(x, g_mix, modc, l, 0, 1, "norm_mix")
        q = _proj_q(h, w_in, gq2, cos, sin, l)
        kv, kv_new = _proj_kv(h, w_in, gk2, cos, sin, l)
        conv = _proj_conv(h, w_in, conv_w, l)
        pool = _proj_pool(h, w_in, pool_w, pool_scale, l)
        att = _attn_prompt(sink_win, q, kv, l)
        att = _attn_latent(sink_win, q, kv, *caches, att, l)
        mixed = _merge(h, att, conv, pool, w_in, w_branch, l)
        x = _out_proj(mixed, w_out, x, modc, l)
        h2 = _norm_mod(x, g_ffn, modc, l, 3, 4, "norm_ffn")
        act = _ffn_up(h2, w_gate_up, l)
        x = _ffn_down(act, w_down, x, modc, l)
        new_kv.append(kv_new)

    new = jnp.stack(new_kv, axis=1).reshape(BATCH, SEQ, DEPTH, 4, N_KV_HEADS, HEAD_DIM)
    new = new.transpose(3, 0, 2, 1, 4, 5)
    y_prompt = x[:P_ROWS].reshape(BATCH, SEQ, D_MODEL)
    y_sample = x[P_ROWS:].reshape(DEC_BATCH, DEC_SEQ, D_MODEL)
    return (y_prompt, y_sample, new[0], new[1], new[2], new[3])
```

```python
import functools

import numpy as np
import jax
import jax.numpy as jnp
from jax import lax
from jax.experimental import pallas as pl
from jax.experimental.pallas import tpu as pltpu

D_MODEL = 1024
BATCH = 16
SEQ = 256
DEPTH = 4
DEC_BATCH = 2
DEC_SEQ = 1024
PAST_LEN = 512
GRID_W = 64
HEAD_DIM = 64
N_Q_HEADS = 8
N_KV_HEADS = 2
ATT_W = N_Q_HEADS * HEAD_DIM
KV_W = N_KV_HEADS * HEAD_DIM
N_FREQ = HEAD_DIM // 4
ROPE_THETA = 10000.0
CONV_W = 512
POOL_W = 512
POOL_SIZES = (2, 4, 8, 16)
N_BRANCH = 4
BRANCH_W = 512
D_FF = 2816
WINDOW = 128
EPS = 1e-6
NEG = -1e30

P_ROWS = BATCH * SEQ
S_ROWS = DEC_BATCH * DEC_SEQ
ROWS = P_ROWS + S_ROWS
CHUNK = 1024
N_CHUNKS = ROWS // CHUNK
N_P_CHUNKS = P_ROWS // CHUNK
TN = 256
LANES = 128
VMEM_LIMIT = 56 * 1024 * 1024

COL_QA, COL_KVA, COL_QC, COL_KVC = 0, 2, 3, 5
COL_U, COL_GB, COL_GC, COL_PV, COL_GATE = 6, 8, 10, 12, 14

F32 = jnp.float32
BF16 = jnp.bfloat16


def _params(n_grid):
    return pltpu.CompilerParams(dimension_semantics=("arbitrary",) * n_grid,
                                vmem_limit_bytes=VMEM_LIMIT)


def _resident(shape, index_map):
    return pl.BlockSpec(shape, index_map, pipeline_mode=pl.Buffered(1))


def _dot(a, b):
    return jnp.dot(a, b, preferred_element_type=F32)


def _dot_nt(a, b):
    return lax.dot_general(a, b, (((1,), (1,)), ((), ())), preferred_element_type=F32)


def _sigmoid(x):
    return 1.0 / (1.0 + jnp.exp(-x))


def _rows(c):
    return pl.ds(pl.multiple_of(c * CHUNK, CHUNK), CHUNK)


def _for_each_chunk(body):
    def step(c, carry):
        body(c, False)
        return carry
    lax.fori_loop(0, N_P_CHUNKS, step, 0)
    for c in range(N_P_CHUNKS, N_CHUNKS):
        body(c, True)


def _head_segments(width):
    r = lax.broadcasted_iota(jnp.int32, (width, width), 0) // HEAD_DIM
    c = lax.broadcasted_iota(jnp.int32, (width, width), 1) // HEAD_DIM
    return jnp.where(r == c, 1.0, 0.0).astype(BF16)


def _head_mean_square(y, seg):
    sq = y * y
    hi = sq.astype(BF16)
    lo = (sq - hi.astype(F32)).astype(BF16)
    return (_dot(hi, seg) + _dot(lo, seg)) * (1.0 / HEAD_DIM)


def _swap16(x):
    lane = lax.broadcasted_iota(jnp.int32, x.shape, 1)
    fwd = pltpu.roll(x, LANES - 16, axis=1)
    bwd = pltpu.roll(x, 16, axis=1)
    return jnp.where((lane & 16) == 0, fwd, bwd)


def _rope128(x, cos, sin):
    return x * cos + _swap16(x) * sin


def _mod_kernel(cond_ref, w_ref, b_ref, o_ref):
    c = cond_ref[...]
    s = (c * _sigmoid(c)).astype(BF16)
    o_ref[...] = _dot(s, w_ref[...].astype(BF16)) + b_ref[...]


def _modulation(cond8, w_mod, b_mod):
    tn = 1536
    n = 6 * D_MODEL
    return pl.pallas_call(
        _mod_kernel,
        grid=(DEPTH, n // tn),
        in_specs=[pl.BlockSpec((8, D_MODEL), lambda l, j: (0, 0)),
                  pl.BlockSpec((None, D_MODEL, tn), lambda l, j: (l, 0, j)),
                  pl.BlockSpec((None, 1, tn), lambda l, j: (l, 0, j))],
        out_specs=pl.BlockSpec((None, 8, tn), lambda l, j: (l, 0, j)),
        out_shape=jax.ShapeDtypeStruct((DEPTH, 8, n), F32),
        compiler_params=_params(2),
        name="modulation",
    )(cond8, w_mod, b_mod.reshape(DEPTH, 1, n))


def _norm_kernel(x_ref, g_ref, sc_ref, sh_ref, o_ref):
    x = x_ref[...]
    y = x * lax.rsqrt(jnp.mean(x * x, axis=-1, keepdims=True) + EPS)
    o_ref[...] = ((y * g_ref[...]) * (1.0 + sc_ref[...]) + sh_ref[...]).astype(BF16)


def _norm_mod(x, g, modc, l, k_shift, k_scale, name):
    return pl.pallas_call(
        _norm_kernel,
        grid=(N_CHUNKS,),
        in_specs=[pl.BlockSpec((CHUNK, D_MODEL), lambda i: (i, 0)),
                  pl.BlockSpec((None, 1, D_MODEL), lambda i: (l, 0, 0)),
                  pl.BlockSpec((None, None, 1, D_MODEL), lambda i: (l, i, 0, k_scale)),
                  pl.BlockSpec((None, None, 1, D_MODEL), lambda i: (l, i, 0, k_shift))],
        out_specs=pl.BlockSpec((CHUNK, D_MODEL), lambda i: (i, 0)),
        out_shape=jax.ShapeDtypeStruct((ROWS, D_MODEL), BF16),
        compiler_params=_params(1),
        name=name,
    )(x, g.reshape(DEPTH, 1, D_MODEL), modc, modc)


def _proj_q_kernel(h_ref, w_ref, g_ref, cos_ref, sin_ref, o_ref):
    w = w_ref[...].astype(BF16)
    g = g_ref[...]
    seg = _head_segments(TN)

    def body(c, latent):
        rows = _rows(c)
        y = _dot(h_ref[rows, :], w)
        y = y * lax.rsqrt(_head_mean_square(y, seg) + EPS) * g
        for s in range(TN // LANES):
            part = y[:, s * LANES:(s + 1) * LANES]
            if latent:
                part = _rope128(part, cos_ref[...], sin_ref[...])
            o_ref[rows, s * LANES:(s + 1) * LANES] = (part * (HEAD_DIM ** -0.5)).astype(BF16)

    _for_each_chunk(body)


def _proj_q(h, w_in, gq2, cos, sin, l):
    return pl.pallas_call(
        _proj_q_kernel,
        grid=(4,),
        in_specs=[_resident((ROWS, D_MODEL), lambda j: (0, 0)),
                  pl.BlockSpec((None, D_MODEL, TN), lambda j: (l, 0, j + j // 2)),
                  pl.BlockSpec((None, None, 1, TN), lambda j: (l, j // 2, 0, 0)),
                  _resident((DEC_SEQ, LANES), lambda j: (0, 0)),
                  _resident((DEC_SEQ, LANES), lambda j: (0, 0))],
        out_specs=pl.BlockSpec((ROWS, TN), lambda j: (0, j)),
        out_shape=jax.ShapeDtypeStruct((ROWS, 2 * ATT_W), BF16),
        compiler_params=_params(1),
        name="proj_q",
    )(h, w_in, gq2, cos, sin)


def _proj_kv_kernel(h_ref, w_ref, g_ref, cos_ref, sin_ref, o_ref, new_ref):
    w = w_ref[...].astype(BF16)
    g = g_ref[...]
    seg = _head_segments(KV_W)

    def body(c, latent):
        rows = _rows(c)
        y = _dot(h_ref[rows, :], w)
        k = y[:, :KV_W]
        v = y[:, KV_W:]
        k = k * lax.rsqrt(_head_mean_square(k, seg) + EPS) * g
        if latent:
            k = _rope128(k, cos_ref[...], sin_ref[...])
        else:
            new_ref[rows, :KV_W] = k
            new_ref[rows, KV_W:] = v
        o_ref[rows, :KV_W] = k.astype(BF16)
        o_ref[rows, KV_W:] = v.astype(BF16)

    _for_each_chunk(body)


def _proj_kv(h, w_in, gk2, cos, sin, l):
    return pl.pallas_call(
        _proj_kv_kernel,
        grid=(2,),
        in_specs=[_resident((ROWS, D_MODEL), lambda j: (0, 0)),
                  pl.BlockSpec((None, D_MODEL, TN), lambda j: (l, 0, COL_KVA + 3 * j)),
                  pl.BlockSpec((None, None, 1, KV_W), lambda j: (l, j, 0, 0)),
                  _resident((DEC_SEQ, LANES), lambda j: (0, 0)),
                  _resident((DEC_SEQ, LANES), lambda j: (0, 0))],
        out_specs=[pl.BlockSpec((ROWS, TN), lambda j: (0, j)),
                   pl.BlockSpec((P_ROWS, TN), lambda j: (0, j))],
        out_shape=[jax.ShapeDtypeStruct((ROWS, 4 * KV_W), BF16),
                   jax.ShapeDtypeStruct((P_ROWS, 4 * KV_W), F32)],
        compiler_params=_params(1),
        name="proj_kv",
    )(h, w_in, gk2, cos, sin)


def _seq_pos(latent, width):
    row = lax.broadcasted_iota(jnp.int32, (CHUNK, width), 0)
    return row if latent else row & (SEQ - 1)


def _shift_rows(x, k, pos, seq_len):
    rolled = pltpu.roll(x, (-k) % CHUNK, axis=0)
    ok = (pos + k >= 0) & (pos + k < seq_len)
    return jnp.where(ok, rolled, 0.0)


def _proj_conv_kernel(h_ref, wu_ref, wb_ref, wc_ref, cw_ref, o_ref):
    wu = wu_ref[...].astype(BF16)
    wb = wb_ref[...].astype(BF16)
    wc = wc_ref[...].astype(BF16)
    cw = cw_ref[...]

    def body(c, latent):
        rows = _rows(c)
        h = h_ref[rows, :]
        seq_len = DEC_SEQ if latent else SEQ
        pos = _seq_pos(latent, TN)
        z = _dot(h, wc) * _dot(h, wu)
        y = (_shift_rows(z, -1, pos, seq_len) * cw[0:1, :] + z * cw[1:2, :]
             + _shift_rows(z, 1, pos, seq_len) * cw[2:3, :])
        o_ref[rows, :] = (_dot(h, wb) * y).astype(BF16)

    _for_each_chunk(body)


def _proj_conv(h, w_in, conv_w, l):
    return pl.pallas_call(
        _proj_conv_kernel,
        grid=(CONV_W // TN,),
        in_specs=[_resident((ROWS, D_MODEL), lambda j: (0, 0)),
                  pl.BlockSpec((None, D_MODEL, TN), lambda j: (l, 0, COL_U + j)),
                  pl.BlockSpec((None, D_MODEL, TN), lambda j: (l, 0, COL_GB + j)),
                  pl.BlockSpec((None, D_MODEL, TN), lambda j: (l, 0, COL_GC + j)),
                  pl.BlockSpec((None, 3, TN), lambda j: (l, 0, j))],
        out_specs=pl.BlockSpec((ROWS, TN), lambda j: (0, j)),
        out_shape=jax.ShapeDtypeStruct((ROWS, CONV_W), BF16),
        compiler_params=_params(1),
        name="proj_conv",
    )(h, w_in, w_in, w_in, conv_w)


def _window_mean_minus_token(p, half, pos, seq_len):
    fwd = p
    bwd = _shift_rows(p, -1, pos, seq_len)
    m = 1
    while m < half:
        fwd = fwd + _shift_rows(fwd, m, pos, seq_len)
        bwd = bwd + _shift_rows(bwd, -m, pos, seq_len)
        m *= 2
    count = jnp.minimum(pos + half, seq_len) - jnp.maximum(pos - half, 0)
    return (fwd + bwd) / count.astype(F32) - p


def _proj_pool_kernel(h_ref, w_ref, pw_ref, ps_ref, o_ref):
    w = w_ref[...].astype(BF16)

    def body(c, latent):
        rows = _rows(c)
        p = _dot(h_ref[rows, :], w)
        seq_len = DEC_SEQ if latent else SEQ
        pos = _seq_pos(latent, LANES)
        for grp, size in enumerate(POOL_SIZES):
            cols = slice(grp * LANES, (grp + 1) * LANES)
            y = _window_mean_minus_token(p[:, cols], size // 2, pos, seq_len).astype(BF16)
            o_ref[rows, cols] = (_dot(y, pw_ref[grp].astype(BF16)) * ps_ref[:, cols]).astype(BF16)

    _for_each_chunk(body)


def _proj_pool(h, w_in, pool_w, pool_scale, l):
    return pl.pallas_call(
        _proj_pool_kernel,
        grid=(1,),
        in_specs=[_resident((ROWS, D_MODEL), lambda j: (0, 0)),
                  pl.BlockSpec((None, D_MODEL, POOL_W), lambda j: (l, 0, COL_PV * TN // POOL_W)),
                  pl.BlockSpec((None, len(POOL_SIZES), LANES, LANES), lambda j: (l, 0, 0, 0)),
                  pl.BlockSpec((None, 1, POOL_W), lambda j: (l, 0, 0))],
        out_specs=pl.BlockSpec((ROWS, POOL_W), lambda j: (0, 0)),
        out_shape=jax.ShapeDtypeStruct((ROWS, POOL_W), BF16),
        compiler_params=_params(1),
        name="proj_pool",
    )(h, w_in, pool_w, pool_scale.reshape(DEPTH, 1, POOL_W))


def _split_heads(x):
    xr = pltpu.roll(x, HEAD_DIM, axis=1)
    low = lax.broadcasted_iota(jnp.int32, x.shape, 1) < HEAD_DIM
    return (jnp.where(low, x, 0.0).astype(BF16), jnp.where(low, 0.0, xr).astype(BF16),
            jnp.where(low, xr, 0.0).astype(BF16), jnp.where(low, 0.0, x).astype(BF16))


def _attend_pair(q, pieces, sinks):
    out = None
    for parity in range(2):
        scores = []
        for k_lo, k_hi, _, _, mask in pieces:
            s = _dot_nt(q, k_hi if parity else k_lo)
            if mask is not None:
                s = jnp.where(mask, s, NEG)
            scores.append(s)
        m = functools.reduce(jnp.maximum, [jnp.max(s, axis=1, keepdims=True) for s in scores])
        if sinks is not None:
            m = jnp.maximum(m, sinks[parity])
        denom = None
        acc = None
        for s, (_, _, v_lo, v_hi, _) in zip(scores, pieces):
            p = jnp.exp(s - m)
            d = jnp.sum(p, axis=1, keepdims=True)
            a = _dot(p.astype(BF16), v_hi if parity else v_lo)
            denom = d if denom is None else denom + d
            acc = a if acc is None else acc + a
        if sinks is not None:
            denom = denom + jnp.exp(sinks[parity] - m)
        acc = acc / denom
        out = acc if out is None else out + acc
    return out


def _attn_prompt_kernel(sink_ref, q_ref, kv_ref, o_ref, *, layer):
    for mixer in range(2):
        k = _split_heads(kv_ref[:, mixer * 2 * KV_W:mixer * 2 * KV_W + KV_W].astype(F32))
        v = _split_heads(kv_ref[:, mixer * 2 * KV_W + KV_W:(mixer + 1) * 2 * KV_W].astype(F32))
        for pair in range(N_Q_HEADS // 2):
            kvh = pair // 2
            piece = (k[2 * kvh], k[2 * kvh + 1], v[2 * kvh], v[2 * kvh + 1], None)
            sinks = None
            if mixer == 1:
                sinks = (sink_ref[layer, 2 * pair], sink_ref[layer, 2 * pair + 1])
            col = mixer * ATT_W + pair * LANES
            o_ref[:, col:col + LANES] = _attend_pair(q_ref[:, col:col + LANES], [piece], sinks).astype(BF16)


def _attn_prompt(sink, q, kv, l):
    return pl.pallas_call(
        functools.partial(_attn_prompt_kernel, layer=l),
        grid=(BATCH,),
        in_specs=[pl.BlockSpec(memory_space=pltpu.SMEM),
                  pl.BlockSpec((SEQ, 2 * ATT_W), lambda b: (b, 0)),
                  pl.BlockSpec((SEQ, 4 * KV_W), lambda b: (b, 0))],
        out_specs=pl.BlockSpec((SEQ, 2 * ATT_W), lambda b: (b, 0)),
        out_shape=jax.ShapeDtypeStruct((ROWS, 2 * ATT_W), BF16),
        compiler_params=_params(1),
        name="attn_prompt",
    )(sink, q, kv)


TQ = 256
A_KEYS = PAST_LEN + DEC_SEQ
WIN_KEYS = TQ + 2 * WINDOW
C_ROWS = PAST_LEN + WINDOW + DEC_SEQ + WINDOW


def _attn_latent_kernel(sink_ref, q_ref, kv_ref, cka_ref, cva_ref, ckc_ref, cvc_ref, att_in_ref,
                        o_ref, ka_s, va_s, kc_s, vc_s, *, layer):
    del att_in_ref
    qt = pl.program_id(1)

    @pl.when(qt == 0)
    def _fill():
        for dst, ctx_ref, col in ((ka_s, cka_ref, 0), (va_s, cva_ref, KV_W)):
            for i, part in enumerate(_split_heads(ctx_ref[...])):
                dst[i, 0:PAST_LEN, :] = part
            for i, part in enumerate(_split_heads(kv_ref[:, col:col + KV_W].astype(F32))):
                dst[i, PAST_LEN:A_KEYS, :] = part
        for dst, ctx_ref, col in ((kc_s, ckc_ref, 2 * KV_W), (vc_s, cvc_ref, 3 * KV_W)):
            zeros = jnp.zeros((WINDOW, LANES), BF16)
            for i, part in enumerate(_split_heads(ctx_ref[...])):
                dst[i, 0:PAST_LEN, :] = part
                dst[i, PAST_LEN:PAST_LEN + WINDOW, :] = zeros
                dst[i, C_ROWS - WINDOW:C_ROWS, :] = zeros
            for i, part in enumerate(_split_heads(kv_ref[:, col:col + KV_W].astype(F32))):
                dst[i, PAST_LEN + WINDOW:PAST_LEN + WINDOW + DEC_SEQ, :] = part

    q0 = qt * TQ
    win = pl.ds(pl.multiple_of(PAST_LEN + q0, TQ), WIN_KEYS)
    r = lax.broadcasted_iota(jnp.int32, (TQ, WIN_KEYS), 0)
    jk = lax.broadcasted_iota(jnp.int32, (TQ, WIN_KEYS), 1)
    kpos = q0 - WINDOW + jk
    band = jnp.where((jk - r >= 0) & (jk - r <= 2 * WINDOW), kpos, -1)
    mask = (band >= 0) & (band < DEC_SEQ)

    for pair in range(N_Q_HEADS // 2):
        kvh = pair // 2
        lo, hi = 2 * kvh, 2 * kvh + 1
        col = pair * LANES
        piece = (ka_s[lo], ka_s[hi], va_s[lo], va_s[hi], None)
        o_ref[:, col:col + LANES] = _attend_pair(q_ref[:, col:col + LANES], [piece], None).astype(BF16)

        col = ATT_W + pair * LANES
        ctx = (kc_s[lo, 0:PAST_LEN, :], kc_s[hi, 0:PAST_LEN, :],
               vc_s[lo, 0:PAST_LEN, :], vc_s[hi, 0:PAST_LEN, :], None)
        near = (kc_s[lo, win, :], kc_s[hi, win, :], vc_s[lo, win, :], vc_s[hi, win, :], mask)
        sinks = (sink_ref[layer, 2 * pair], sink_ref[layer, 2 * pair + 1])
        o_ref[:, col:col + LANES] = _attend_pair(q_ref[:, col:col + LANES], [ctx, near], sinks).astype(BF16)


def _attn_latent(sink, q, kv, cka, cva, ckc, cvc, att, l):
    n_qt = DEC_SEQ // TQ
    p_tiles = P_ROWS // TQ
    cache_spec = pl.BlockSpec((None, None, PAST_LEN, KV_W), lambda b, t: (b, l, 0, 0))
    return pl.pallas_call(
        functools.partial(_attn_latent_kernel, layer=l),
        grid=(DEC_BATCH, n_qt),
        in_specs=[pl.BlockSpec(memory_space=pltpu.SMEM),
                  pl.BlockSpec((TQ, 2 * ATT_W), lambda b, t: (p_tiles + b * n_qt + t, 0)),
                  pl.BlockSpec((DEC_SEQ, 4 * KV_W), lambda b, t: (N_P_CHUNKS + b, 0)),
                  cache_spec, cache_spec, cache_spec, cache_spec,
                  pl.BlockSpec(memory_space=pl.ANY)],
        out_specs=pl.BlockSpec((TQ, 2 * ATT_W), lambda b, t: (p_tiles + b * n_qt + t, 0)),
        out_shape=jax.ShapeDtypeStruct((ROWS, 2 * ATT_W), BF16),
        scratch_shapes=[pltpu.VMEM((4, A_KEYS, LANES), BF16), pltpu.VMEM((4, A_KEYS, LANES), BF16),
                        pltpu.VMEM((4, C_ROWS, LANES), BF16), pltpu.VMEM((4, C_ROWS, LANES), BF16)],
        input_output_aliases={7: 0},
        compiler_params=_params(2),
        name="attn_latent",
    )(sink, q, kv, cka, cva, ckc, cvc, att)


M_ROWS = 2048


def _merge_kernel(h_ref, att_ref, conv_ref, pool_ref, wg0, wg1, wg2, wg3, wb0, wb1, wb2, wb3, o_ref):
    h = h_ref[...]
    branches = (att_ref[:, :ATT_W], conv_ref[...], att_ref[:, ATT_W:], pool_ref[...])
    acc = None
    for br, wg, wb in zip(branches, (wg0, wg1, wg2, wg3), (wb0, wb1, wb2, wb3)):
        gate = _sigmoid(_dot(h, wg[...].astype(BF16)))
        term = gate * _dot(br, wb[...].astype(BF16))
        acc = term if acc is None else acc + term
    o_ref[...] = acc.astype(BF16)


def _merge(h, att, conv, pool, w_in, w_branch, l):
    gate_specs = [pl.BlockSpec((None, D_MODEL, TN), functools.partial(
        lambda i, n, k: (l, 0, COL_GATE + (D_MODEL // TN) * k + n), k=k)) for k in range(N_BRANCH)]
    br_specs = [pl.BlockSpec((None, None, BRANCH_W, TN), functools.partial(
        lambda i, n, k: (l, k, 0, n), k=k)) for k in range(N_BRANCH)]
    return pl.pallas_call(
        _merge_kernel,
        grid=(ROWS // M_ROWS, D_MODEL // TN),
        in_specs=[pl.BlockSpec((M_ROWS, D_MODEL), lambda i, n: (i, 0)),
                  pl.BlockSpec((M_ROWS, 2 * ATT_W), lambda i, n: (i, 0)),
                  pl.BlockSpec((M_ROWS, CONV_W), lambda i, n: (i, 0)),
                  pl.BlockSpec((M_ROWS, POOL_W), lambda i, n: (i, 0))] + gate_specs + br_specs,
        out_specs=pl.BlockSpec((M_ROWS, TN), lambda i, n: (i, n)),
        out_shape=jax.ShapeDtypeStruct((ROWS, D_MODEL), BF16),
        compiler_params=_params(2),
        name="merge",
    )(h, att, conv, pool, *([w_in] * N_BRANCH), *([w_branch] * N_BRANCH))


def _residual_kernel(a_ref, w_ref, x_ref, g_ref, o_ref):
    w = w_ref[...].astype(BF16)

    def body(c, latent):
        del latent
        rows = _rows(c)
        o_ref[rows, :] = x_ref[rows, :] + g_ref[c] * _dot(a_ref[rows, :], w)

    _for_each_chunk(body)


def _out_proj(mixed, w_out, x, modc, l):
    return pl.pallas_call(
        _residual_kernel,
        grid=(D_MODEL // TN,),
        in_specs=[_resident((ROWS, D_MODEL), lambda n: (0, 0)),
                  pl.BlockSpec((None, D_MODEL, TN), lambda n: (l, 0, n)),
                  pl.BlockSpec((ROWS, TN), lambda n: (0, n)),
                  pl.BlockSpec((None, N_CHUNKS, 1, TN), lambda n: (l, 0, 0, 2 * (D_MODEL // TN) + n))],
        out_specs=pl.BlockSpec((ROWS, TN), lambda n: (0, n)),
        out_shape=jax.ShapeDtypeStruct((ROWS, D_MODEL), F32),
        compiler_params=_params(1),
        name="out_proj",
    )(mixed, w_out, x, modc)


FF_ROWS = 512
FF_CHUNKS = ROWS // FF_ROWS
FF_TILES = D_FF // TN
FF_STEPS = FF_TILES + FF_CHUNKS - 1


def _ffn_kernel(x_ref, g_ref, sh_ref, sc_ref, gate_ref, wa_ref, wb_ref, wd_ref, o_ref,
                wa_s, wb_s, wd_s, h_s, act_s):
    s = pl.program_id(0)

    @pl.when((s == 0) | (s >= FF_TILES))
    def _norm():
        x = x_ref[...]
        y = x * lax.rsqrt(jnp.mean(x * x, axis=-1, keepdims=True) + EPS)
        h_s[...] = ((y * g_ref[...]) * (1.0 + sc_ref[...]) + sh_ref[...]).astype(BF16)

    def up(wa, wb):
        h = h_s[...]
        a = _dot(h, wa)
        return (a * _sigmoid(a) * _dot(h, wb)).astype(BF16)

    def down():
        o_ref[...] = x_ref[...] + gate_ref[...] * _dot(act_s[...], wd_s[...])

    @pl.when(s < FF_TILES)
    def _stream():
        wa = wa_ref[...].astype(BF16)
        wb = wb_ref[...].astype(BF16)
        wa_s[s] = wa
        wb_s[s] = wb
        wd_s[pl.ds(pl.multiple_of(s * TN, TN), TN), :] = wd_ref[...].astype(BF16)
        v = up(wa, wb)
        for j in range(FF_TILES):
            @pl.when(s == j)
            def _store(j=j):
                act_s[:, j * TN:(j + 1) * TN] = v

    @pl.when(s == FF_TILES - 1)
    def _first_chunk_down():
        down()

    @pl.when(s >= FF_TILES)
    def _chunk():
        for j in range(FF_TILES):
            act_s[:, j * TN:(j + 1) * TN] = up(wa_s[j], wb_s[j])
        down()


def _ffn(x, g_ffn, modc, w_gate_up, w_down, l):
    def chunk(s):
        return jnp.maximum(s - (FF_TILES - 1), 0)

    def tile(s):
        return jnp.minimum(s, FF_TILES - 1)

    def mod_spec(k):
        return pl.BlockSpec((None, None, 1, D_MODEL), lambda s: (l, chunk(s) * FF_ROWS // CHUNK, 0, k))

    return pl.pallas_call(
        _ffn_kernel,
        grid=(FF_STEPS,),
        in_specs=[pl.BlockSpec((FF_ROWS, D_MODEL), lambda s: (chunk(s), 0)),
                  pl.BlockSpec((None, 1, D_MODEL), lambda s: (l, 0, 0)),
                  mod_spec(3), mod_spec(4), mod_spec(5),
                  pl.BlockSpec((None, D_MODEL, TN), lambda s: (l, 0, tile(s))),
                  pl.BlockSpec((None, D_MODEL, TN), lambda s: (l, 0, FF_TILES + tile(s))),
                  pl.BlockSpec((None, TN, D_MODEL), lambda s: (l, tile(s), 0))],
        out_specs=pl.BlockSpec((FF_ROWS, D_MODEL), lambda s: (chunk(s), 0)),
        out_shape=jax.ShapeDtypeStruct((ROWS, D_MODEL), F32),
        scratch_shapes=[pltpu.VMEM((FF_TILES, D_MODEL, TN), BF16),
                        pltpu.VMEM((FF_TILES, D_MODEL, TN), BF16),
                        pltpu.VMEM((D_FF, D_MODEL), BF16),
                        pltpu.VMEM((FF_ROWS, D_MODEL), BF16),
                        pltpu.VMEM((FF_ROWS, D_FF), BF16)],
        compiler_params=_params(1),
        name="ffn",
    )(x, g_ffn.reshape(DEPTH, 1, D_MODEL), modc, modc, modc, w_gate_up, w_gate_up, w_down)


def _rope_tables():
    rows = DEC_SEQ // GRID_W
    row = jnp.repeat(jnp.arange(rows, dtype=F32), GRID_W)
    col = jnp.tile(jnp.arange(GRID_W, dtype=F32), rows)
    inv = 1.0 / (ROPE_THETA ** (jnp.arange(N_FREQ, dtype=F32) / N_FREQ))
    cr, sr = jnp.cos(row[:, None] * inv), jnp.sin(row[:, None] * inv)
    cc, sc = jnp.cos(col[:, None] * inv), jnp.sin(col[:, None] * inv)
    cos = jnp.concatenate([cr, cr, cc, cc], axis=1)
    sin = jnp.concatenate([-sr, sr, -sc, sc], axis=1)
    reps = LANES // HEAD_DIM
    return jnp.tile(cos, (1, reps)), jnp.tile(sin, (1, reps))


def kernel(x_prompt, x_sample, cache_k_attn, cache_v_attn, cache_k_win, cache_v_win, c, c_ctx,
           w_mod, b_mod, g_mix, g_ffn, w_in, gq_attn, gk_attn, gq_win, gk_win, sink_win,
           conv_w, pool_w, pool_scale, w_branch, w_out, w_gate_up, w_down):
    x = jnp.concatenate([x_prompt.reshape(P_ROWS, D_MODEL), x_sample.reshape(S_ROWS, D_MODEL)], axis=0)

    cond8 = jnp.zeros((8, D_MODEL), F32).at[0].set(c_ctx).at[1:1 + DEC_BATCH].set(c)
    mod = _modulation(cond8, w_mod, b_mod)
    chunk_cond = np.array([0] * N_P_CHUNKS + [1 + b for b in range(DEC_BATCH)])
    modc = mod[:, chunk_cond].reshape(DEPTH, N_CHUNKS, 1, 6 * D_MODEL)

    cos, sin = _rope_tables()
    gq2 = jnp.stack([jnp.tile(gq_attn, (1, TN // HEAD_DIM)), jnp.tile(gq_win, (1, TN // HEAD_DIM))],
                    axis=1).reshape(DEPTH, 2, 1, TN)
    gk2 = jnp.stack([jnp.tile(gk_attn, (1, KV_W // HEAD_DIM)), jnp.tile(gk_win, (1, KV_W // HEAD_DIM))],
                    axis=1).reshape(DEPTH, 2, 1, KV_W)
    caches = [a.reshape(DEC_BATCH, DEPTH, PAST_LEN, KV_W)
              for a in (cache_k_attn, cache_v_attn, cache_k_win, cache_v_win)]

    new_kv = []
    for l in range(DEPTH):
        h = _norm_mod(x, g_mix, modc, l, 0, 1, "norm_mix")
        q = _proj_q(h, w_in, gq2, cos, sin, l)
        kv, kv_new = _proj_kv(h, w_in, gk2, cos, sin, l)
        conv = _proj_conv(h, w_in, conv_w, l)
        pool = _proj_pool(h, w_in, pool_w, pool_scale, l)
        att = _attn_prompt(sink_win, q, kv, l)
        att = _attn_latent(sink_win, q, kv, *caches, att, l)
        mixed = _merge(h, att, conv, pool, w_in, w_branch, l)
        x = _out_proj(mixed, w_out, x, modc, l)
        x = _ffn(x, g_ffn, modc, w_gate_up, w_down, l)
        new_kv.append(kv_new)

    new = jnp.stack(new_kv, axis=1).reshape(BATCH, SEQ, DEPTH, 4, N_KV_HEADS, HEAD_DIM)
    new = new.transpose(3, 0, 2, 1, 4, 5)
    y_prompt = x[:P_ROWS].reshape(BATCH, SEQ, D_MODEL)
    y_sample = x[P_ROWS:].reshape(DEC_BATCH, DEC_SEQ, D_MODEL)
    return (y_prompt, y_sample, new[0], new[1], new[2], new[3])
```

```python
import functools

import numpy as np
import jax
import jax.numpy as jnp
from jax import lax
from jax.experimental import pallas as pl
from jax.experimental.pallas import tpu as pltpu

D_MODEL = 1024
BATCH = 16
SEQ = 256
DEPTH = 4
DEC_BATCH = 2
DEC_SEQ = 1024
PAST_LEN = 512
GRID_W = 64
HEAD_DIM = 64
N_Q_HEADS = 8
N_KV_HEADS = 2
ATT_W = N_Q_HEADS * HEAD_DIM
KV_W = N_KV_HEADS * HEAD_DIM
N_FREQ = HEAD_DIM // 4
ROPE_THETA = 10000.0
CONV_W = 512
POOL_W = 512
POOL_SIZES = (2, 4, 8, 16)
N_BRANCH = 4
BRANCH_W = 512
D_FF = 2816
WINDOW = 128
EPS = 1e-6
NEG = -1e30

P_ROWS = BATCH * SEQ
S_ROWS = DEC_BATCH * DEC_SEQ
ROWS = P_ROWS + S_ROWS
CHUNK = 1024
N_CHUNKS = ROWS // CHUNK
N_P_CHUNKS = P_ROWS // CHUNK
TN = 256
LANES = 128
VMEM_LIMIT = 56 * 1024 * 1024

COL_QA, COL_KVA, COL_QC, COL_KVC = 0, 2, 3, 5
COL_U, COL_GB, COL_GC, COL_PV, COL_GATE = 6, 8, 10, 12, 14

F32 = jnp.float32
BF16 = jnp.bfloat16


def _params(n_grid):
    return pltpu.CompilerParams(dimension_semantics=("arbitrary",) * n_grid,
                                vmem_limit_bytes=VMEM_LIMIT)


def _resident(shape, index_map):
    return pl.BlockSpec(shape, index_map, pipeline_mode=pl.Buffered(1))


def _dot(a, b):
    return jnp.dot(a, b, preferred_element_type=F32)


def _dot_nt(a, b):
    return lax.dot_general(a, b, (((1,), (1,)), ((), ())), preferred_element_type=F32)


def _sigmoid(x):
    return 1.0 / (1.0 + jnp.exp(-x))


def _rows(c):
    return pl.ds(pl.multiple_of(c * CHUNK, CHUNK), CHUNK)


def _for_each_chunk(body):
    def step(c, carry):
        body(c, False)
        return carry
    lax.fori_loop(0, N_P_CHUNKS, step, 0)
    for c in range(N_P_CHUNKS, N_CHUNKS):
        body(c, True)


def _head_segments(width):
    r = lax.broadcasted_iota(jnp.int32, (width, width), 0) // HEAD_DIM
    c = lax.broadcasted_iota(jnp.int32, (width, width), 1) // HEAD_DIM
    return jnp.where(r == c, 1.0, 0.0).astype(BF16)


def _head_mean_square(y, seg):
    sq = y * y
    hi = sq.astype(BF16)
    lo = (sq - hi.astype(F32)).astype(BF16)
    return (_dot(hi, seg) + _dot(lo, seg)) * (1.0 / HEAD_DIM)


def _swap16(x):
    lane = lax.broadcasted_iota(jnp.int32, x.shape, 1)
    fwd = pltpu.roll(x, LANES - 16, axis=1)
    bwd = pltpu.roll(x, 16, axis=1)
    return jnp.where((lane & 16) == 0, fwd, bwd)


def _rope128(x, cos, sin):
    return x * cos + _swap16(x) * sin


def _mod_kernel(cond_ref, w_ref, b_ref, o_ref):
    c = cond_ref[...]
    s = (c * _sigmoid(c)).astype(BF16)
    o_ref[...] = _dot(s, w_ref[...].astype(BF16)) + b_ref[...]


def _modulation(cond8, w_mod, b_mod):
    tn = 1536
    n = 6 * D_MODEL
    return pl.pallas_call(
        _mod_kernel,
        grid=(DEPTH, n // tn),
        in_specs=[pl.BlockSpec((8, D_MODEL), lambda l, j: (0, 0)),
                  pl.BlockSpec((None, D_MODEL, tn), lambda l, j: (l, 0, j)),
                  pl.BlockSpec((None, 1, tn), lambda l, j: (l, 0, j))],
        out_specs=pl.BlockSpec((None, 8, tn), lambda l, j: (l, 0, j)),
        out_shape=jax.ShapeDtypeStruct((DEPTH, 8, n), F32),
        compiler_params=_params(2),
        name="modulation",
    )(cond8, w_mod, b_mod.reshape(DEPTH, 1, n))


def _norm_kernel(x_ref, g_ref, sc_ref, sh_ref, o_ref):
    x = x_ref[...]
    y = x * lax.rsqrt(jnp.mean(x * x, axis=-1, keepdims=True) + EPS)
    o_ref[...] = ((y * g_ref[...]) * (1.0 + sc_ref[...]) + sh_ref[...]).astype(BF16)


def _norm_mod(x, g, modc, l, k_shift, k_scale, name):
    return pl.pallas_call(
        _norm_kernel,
        grid=(N_CHUNKS,),
        in_specs=[pl.BlockSpec((CHUNK, D_MODEL), lambda i: (i, 0)),
                  pl.BlockSpec((None, 1, D_MODEL), lambda i: (l, 0, 0)),
                  pl.BlockSpec((None, None, 1, D_MODEL), lambda i: (l, i, 0, k_scale)),
                  pl.BlockSpec((None, None, 1, D_MODEL), lambda i: (l, i, 0, k_shift))],
        out_specs=pl.BlockSpec((CHUNK, D_MODEL), lambda i: (i, 0)),
        out_shape=jax.ShapeDtypeStruct((ROWS, D_MODEL), BF16),
        compiler_params=_params(1),
        name=name,
    )(x, g.reshape(DEPTH, 1, D_MODEL), modc, modc)


def _proj_q_kernel(h_ref, w_ref, g_ref, cos_ref, sin_ref, o_ref):
    w = w_ref[...].astype(BF16)
    g = g_ref[...]
    seg = _head_segments(TN)

    def body(c, latent):
        rows = _rows(c)
        y = _dot(h_ref[rows, :], w)
        y = y * lax.rsqrt(_head_mean_square(y, seg) + EPS) * g
        for s in range(TN // LANES):
            part = y[:, s * LANES:(s + 1) * LANES]
            if latent:
                part = _rope128(part, cos_ref[...], sin_ref[...])
            o_ref[rows, s * LANES:(s + 1) * LANES] = (part * (HEAD_DIM ** -0.5)).astype(BF16)

    _for_each_chunk(body)


def _proj_q(h, w_in, gq2, cos, sin, l):
    return pl.pallas_call(
        _proj_q_kernel,
        grid=(4,),
        in_specs=[_resident((ROWS, D_MODEL), lambda j: (0, 0)),
                  pl.BlockSpec((None, D_MODEL, TN), lambda j: (l, 0, j + j // 2)),
                  pl.BlockSpec((None, None, 1, TN), lambda j: (l, j // 2, 0, 0)),
                  _resident((DEC_SEQ, LANES), lambda j: (0, 0)),
                  _resident((DEC_SEQ, LANES), lambda j: (0, 0))],
        out_specs=pl.BlockSpec((ROWS, TN), lambda j: (0, j)),
        out_shape=jax.ShapeDtypeStruct((ROWS, 2 * ATT_W), BF16),
        compiler_params=_params(1),
        name="proj_q",
    )(h, w_in, gq2, cos, sin)


def _proj_kv_kernel(h_ref, w_ref, g_ref, cos_ref, sin_ref, o_ref, new_ref):
    w = w_ref[...].astype(BF16)
    g = g_ref[...]
    seg = _head_segments(KV_W)

    def body(c, latent):
        rows = _rows(c)
        y = _dot(h_ref[rows, :], w)
        k = y[:, :KV_W]
        v = y[:, KV_W:]
        k = k * lax.rsqrt(_head_mean_square(k, seg) + EPS) * g
        if latent:
            k = _rope128(k, cos_ref[...], sin_ref[...])
        else:
            new_ref[rows, :KV_W] = k
            new_ref[rows, KV_W:] = v
        o_ref[rows, :KV_W] = k.astype(BF16)
        o_ref[rows, KV_W:] = v.astype(BF16)

    _for_each_chunk(body)


def _proj_kv(h, w_in, gk2, cos, sin, l):
    return pl.pallas_call(
        _proj_kv_kernel,
        grid=(2,),
        in_specs=[_resident((ROWS, D_MODEL), lambda j: (0, 0)),
                  pl.BlockSpec((None, D_MODEL, TN), lambda j: (l, 0, COL_KVA + 3 * j)),
                  pl.BlockSpec((None, None, 1, KV_W), lambda j: (l, j, 0, 0)),
                  _resident((DEC_SEQ, LANES), lambda j: (0, 0)),
                  _resident((DEC_SEQ, LANES), lambda j: (0, 0))],
        out_specs=[pl.BlockSpec((ROWS, TN), lambda j: (0, j)),
                   pl.BlockSpec((P_ROWS, TN), lambda j: (0, j))],
        out_shape=[jax.ShapeDtypeStruct((ROWS, 4 * KV_W), BF16),
                   jax.ShapeDtypeStruct((P_ROWS, 4 * KV_W), F32)],
        compiler_params=_params(1),
        name="proj_kv",
    )(h, w_in, gk2, cos, sin)


def _seq_pos(latent, width):
    row = lax.broadcasted_iota(jnp.int32, (CHUNK, width), 0)
    return row if latent else row & (SEQ - 1)


def _shift_rows(x, k, pos, seq_len):
    rolled = pltpu.roll(x, (-k) % CHUNK, axis=0)
    ok = (pos + k >= 0) & (pos + k < seq_len)
    return jnp.where(ok, rolled, 0.0)


def _proj_conv_kernel(h_ref, wu_ref, wb_ref, wc_ref, cw_ref, o_ref):
    wu = wu_ref[...].astype(BF16)
    wb = wb_ref[...].astype(BF16)
    wc = wc_ref[...].astype(BF16)
    cw = cw_ref[...]

    def body(c, latent):
        rows = _rows(c)
        h = h_ref[rows, :]
        seq_len = DEC_SEQ if latent else SEQ
        pos = _seq_pos(latent, TN)
        z = _dot(h, wc) * _dot(h, wu)
        y = (_shift_rows(z, -1, pos, seq_len) * cw[0:1, :] + z * cw[1:2, :]
             + _shift_rows(z, 1, pos, seq_len) * cw[2:3, :])
        o_ref[rows, :] = (_dot(h, wb) * y).astype(BF16)

    _for_each_chunk(body)


def _proj_conv(h, w_in, conv_w, l):
    return pl.pallas_call(
        _proj_conv_kernel,
        grid=(CONV_W // TN,),
        in_specs=[_resident((ROWS, D_MODEL), lambda j: (0, 0)),
                  pl.BlockSpec((None, D_MODEL, TN), lambda j: (l, 0, COL_U + j)),
                  pl.BlockSpec((None, D_MODEL, TN), lambda j: (l, 0, COL_GB + j)),
                  pl.BlockSpec((None, D_MODEL, TN), lambda j: (l, 0, COL_GC + j)),
                  pl.BlockSpec((None, 3, TN), lambda j: (l, 0, j))],
        out_specs=pl.BlockSpec((ROWS, TN), lambda j: (0, j)),
        out_shape=jax.ShapeDtypeStruct((ROWS, CONV_W), BF16),
        compiler_params=_params(1),
        name="proj_conv",
    )(h, w_in, w_in, w_in, conv_w)


def _window_mean_minus_token(p, half, pos, seq_len):
    fwd = p
    bwd = _shift_rows(p, -1, pos, seq_len)
    m = 1
    while m < half:
        fwd = fwd + _shift_rows(fwd, m, pos, seq_len)
        bwd = bwd + _shift_rows(bwd, -m, pos, seq_len)
        m *= 2
    count = jnp.minimum(pos + half, seq_len) - jnp.maximum(pos - half, 0)
    return (fwd + bwd) / count.astype(F32) - p


def _proj_pool_kernel(h_ref, w_ref, pw_ref, ps_ref, o_ref):
    w = w_ref[...].astype(BF16)

    def body(c, latent):
        rows = _rows(c)
        p = _dot(h_ref[rows, :], w)
        seq_len = DEC_SEQ if latent else SEQ
        pos = _seq_pos(latent, LANES)
        for grp, size in enumerate(POOL_SIZES):
            cols = slice(grp * LANES, (grp + 1) * LANES)
            y = _window_mean_minus_token(p[:, cols], size // 2, pos, seq_len).astype(BF16)
            o_ref[rows, cols] = (_dot(y, pw_ref[grp].astype(BF16)) * ps_ref[:, cols]).astype(BF16)

    _for_each_chunk(body)


def _proj_pool(h, w_in, pool_w, pool_scale, l):
    return pl.pallas_call(
        _proj_pool_kernel,
        grid=(1,),
        in_specs=[_resident((ROWS, D_MODEL), lambda j: (0, 0)),
                  pl.BlockSpec((None, D_MODEL, POOL_W), lambda j: (l, 0, COL_PV * TN // POOL_W)),
                  pl.BlockSpec((None, len(POOL_SIZES), LANES, LANES), lambda j: (l, 0, 0, 0)),
                  pl.BlockSpec((None, 1, POOL_W), lambda j: (l, 0, 0))],
        out_specs=pl.BlockSpec((ROWS, POOL_W), lambda j: (0, 0)),
        out_shape=jax.ShapeDtypeStruct((ROWS, POOL_W), BF16),
        compiler_params=_params(1),
        name="proj_pool",
    )(h, w_in, pool_w, pool_scale.reshape(DEPTH, 1, POOL_W))


def _split_heads(x):
    xr = pltpu.roll(x, HEAD_DIM, axis=1)
    low = lax.broadcasted_iota(jnp.int32, x.shape, 1) < HEAD_DIM
    return (jnp.where(low, x, 0.0).astype(BF16), jnp.where(low, 0.0, xr).astype(BF16),
            jnp.where(low, xr, 0.0).astype(BF16), jnp.where(low, 0.0, x).astype(BF16))


def _attend_pair(q, pieces, sinks):
    out = None
    for parity in range(2):
        scores = []
        for k_lo, k_hi, _, _, mask in pieces:
            s = _dot_nt(q, k_hi if parity else k_lo)
            if mask is not None:
                s = jnp.where(mask, s, NEG)
            scores.append(s)
        m = functools.reduce(jnp.maximum, [jnp.max(s, axis=1, keepdims=True) for s in scores])
        if sinks is not None:
            m = jnp.maximum(m, sinks[parity])
        denom = None
        acc = None
        for s, (_, _, v_lo, v_hi, _) in zip(scores, pieces):
            p = jnp.exp(s - m)
            d = jnp.sum(p, axis=1, keepdims=True)
            a = _dot(p.astype(BF16), v_hi if parity else v_lo)
            denom = d if denom is None else denom + d
            acc = a if acc is None else acc + a
        if sinks is not None:
            denom = denom + jnp.exp(sinks[parity] - m)
        acc = acc / denom
        out = acc if out is None else out + acc
    return out


def _attn_prompt_step(sink_ref, q_ref, kv_ref, o_ref, layer):
    for mixer in range(2):
        k = _split_heads(kv_ref[:, mixer * 2 * KV_W:mixer * 2 * KV_W + KV_W].astype(F32))
        v = _split_heads(kv_ref[:, mixer * 2 * KV_W + KV_W:(mixer + 1) * 2 * KV_W].astype(F32))
        for pair in range(N_Q_HEADS // 2):
            kvh = pair // 2
            piece = (k[2 * kvh], k[2 * kvh + 1], v[2 * kvh], v[2 * kvh + 1], None)
            sinks = None
            if mixer == 1:
                sinks = (sink_ref[layer, 2 * pair], sink_ref[layer, 2 * pair + 1])
            col = mixer * ATT_W + pair * LANES
            o_ref[:, col:col + LANES] = _attend_pair(q_ref[:, col:col + LANES], [piece], sinks).astype(BF16)


TQ = SEQ
N_QT = DEC_SEQ // TQ
A_KEYS = PAST_LEN + DEC_SEQ
WIN_KEYS = TQ + 2 * WINDOW
C_ROWS = PAST_LEN + WINDOW + DEC_SEQ + WINDOW
ATT_STEPS = BATCH + DEC_BATCH * N_QT


def _attn_latent_step(qt, sink_ref, q_ref, kv_ref, cka_ref, cva_ref, ckc_ref, cvc_ref,
                      o_ref, ka_s, va_s, kc_s, vc_s, layer):
    @pl.when(qt == 0)
    def _fill():
        for dst, ctx_ref, col in ((ka_s, cka_ref, 0), (va_s, cva_ref, KV_W)):
            for i, part in enumerate(_split_heads(ctx_ref[...])):
                dst[i, 0:PAST_LEN, :] = part
            for i, part in enumerate(_split_heads(kv_ref[:, col:col + KV_W].astype(F32))):
                dst[i, PAST_LEN:A_KEYS, :] = part
        for dst, ctx_ref, col in ((kc_s, ckc_ref, 2 * KV_W), (vc_s, cvc_ref, 3 * KV_W)):
            zeros = jnp.zeros((WINDOW, LANES), BF16)
            for i, part in enumerate(_split_heads(ctx_ref[...])):
                dst[i, 0:PAST_LEN, :] = part
                dst[i, PAST_LEN:PAST_LEN + WINDOW, :] = zeros
                dst[i, C_ROWS - WINDOW:C_ROWS, :] = zeros
            for i, part in enumerate(_split_heads(kv_ref[:, col:col + KV_W].astype(F32))):
                dst[i, PAST_LEN + WINDOW:PAST_LEN + WINDOW + DEC_SEQ, :] = part

    q0 = qt * TQ
    win = pl.ds(pl.multiple_of(PAST_LEN + q0, TQ), WIN_KEYS)
    r = lax.broadcasted_iota(jnp.int32, (TQ, WIN_KEYS), 0)
    jk = lax.broadcasted_iota(jnp.int32, (TQ, WIN_KEYS), 1)
    kpos = q0 - WINDOW + jk
    band = jnp.where((jk - r >= 0) & (jk - r <= 2 * WINDOW), kpos, -1)
    mask = (band >= 0) & (band < DEC_SEQ)

    for pair in range(N_Q_HEADS // 2):
        kvh = pair // 2
        lo, hi = 2 * kvh, 2 * kvh + 1
        col = pair * LANES
        piece = (ka_s[lo], ka_s[hi], va_s[lo], va_s[hi], None)
        o_ref[:, col:col + LANES] = _attend_pair(q_ref[:, col:col + LANES], [piece], None).astype(BF16)

        col = ATT_W + pair * LANES
        ctx = (kc_s[lo, 0:PAST_LEN, :], kc_s[hi, 0:PAST_LEN, :],
               vc_s[lo, 0:PAST_LEN, :], vc_s[hi, 0:PAST_LEN, :], None)
        near = (kc_s[lo, win, :], kc_s[hi, win, :], vc_s[lo, win, :], vc_s[hi, win, :], mask)
        sinks = (sink_ref[layer, 2 * pair], sink_ref[layer, 2 * pair + 1])
        o_ref[:, col:col + LANES] = _attend_pair(q_ref[:, col:col + LANES], [ctx, near], sinks).astype(BF16)


def _attn_kernel(sink_ref, q_ref, kvp_ref, kvl_ref, cka_ref, cva_ref, ckc_ref, cvc_ref,
                 o_ref, ka_s, va_s, kc_s, vc_s, *, layer):
    s = pl.program_id(0)

    @pl.when(s < BATCH)
    def _prompt():
        _attn_prompt_step(sink_ref, q_ref, kvp_ref, o_ref, layer)

    @pl.when(s >= BATCH)
    def _latent():
        _attn_latent_step((s - BATCH) % N_QT, sink_ref, q_ref, kvl_ref, cka_ref, cva_ref, ckc_ref, cvc_ref,
                          o_ref, ka_s, va_s, kc_s, vc_s, layer)


def _attention(sink, q, kv, cka, cva, ckc, cvc, l):
    def latent_batch(s):
        return jnp.maximum(s - BATCH, 0) // N_QT

    cache_spec = pl.BlockSpec((None, None, PAST_LEN, KV_W), lambda s: (latent_batch(s), l, 0, 0))
    return pl.pallas_call(
        functools.partial(_attn_kernel, layer=l),
        grid=(ATT_STEPS,),
        in_specs=[pl.BlockSpec(memory_space=pltpu.SMEM),
                  pl.BlockSpec((TQ, 2 * ATT_W), lambda s: (s, 0)),
                  pl.BlockSpec((SEQ, 4 * KV_W), lambda s: (jnp.minimum(s, BATCH - 1), 0)),
                  pl.BlockSpec((DEC_SEQ, 4 * KV_W), lambda s: (N_P_CHUNKS + latent_batch(s), 0)),
                  cache_spec, cache_spec, cache_spec, cache_spec],
        out_specs=pl.BlockSpec((TQ, 2 * ATT_W), lambda s: (s, 0)),
        out_shape=jax.ShapeDtypeStruct((ROWS, 2 * ATT_W), BF16),
        scratch_shapes=[pltpu.VMEM((4, A_KEYS, LANES), BF16), pltpu.VMEM((4, A_KEYS, LANES), BF16),
                        pltpu.VMEM((4, C_ROWS, LANES), BF16), pltpu.VMEM((4, C_ROWS, LANES), BF16)],
        compiler_params=_params(1),
        name="attention",
    )(sink, q, kv, kv, cka, cva, ckc, cvc)


MG_ROWS = 512
MG_CHUNKS = ROWS // MG_ROWS
MG_TILES = D_MODEL // TN
MG_STEPS = MG_TILES + MG_CHUNKS - 1


def _mixer_out_kernel(x_ref, g_ref, sh_ref, sc_ref, gate_ref, att_ref, conv_ref, pool_ref,
                      wg0, wg1, wg2, wg3, wb0, wb1, wb2, wb3, wo_ref, o_ref,
                      wg_s, wb_s, wo_s, h_s, mix_s):
    s = pl.program_id(0)

    @pl.when((s == 0) | (s >= MG_TILES))
    def _norm():
        x = x_ref[...]
        y = x * lax.rsqrt(jnp.mean(x * x, axis=-1, keepdims=True) + EPS)
        h_s[...] = ((y * g_ref[...]) * (1.0 + sc_ref[...]) + sh_ref[...]).astype(BF16)

    def mix_tile(wgs, wbs):
        h = h_s[...]
        branches = (att_ref[:, :ATT_W], conv_ref[...], att_ref[:, ATT_W:], pool_ref[...])
        acc = None
        for br, wg, wb in zip(branches, wgs, wbs):
            term = _sigmoid(_dot(h, wg)) * _dot(br, wb)
            acc = term if acc is None else acc + term
        return acc.astype(BF16)

    def out():
        o_ref[...] = x_ref[...] + gate_ref[...] * _dot(mix_s[...], wo_s[...])

    @pl.when(s < MG_TILES)
    def _stream():
        wgs = [r[...].astype(BF16) for r in (wg0, wg1, wg2, wg3)]
        wbs = [r[...].astype(BF16) for r in (wb0, wb1, wb2, wb3)]
        for k in range(N_BRANCH):
            wg_s[s * N_BRANCH + k] = wgs[k]
            wb_s[s * N_BRANCH + k] = wbs[k]
        wo_s[pl.ds(pl.multiple_of(s * TN, TN), TN), :] = wo_ref[...].astype(BF16)
        v = mix_tile(wgs, wbs)
        for n in range(MG_TILES):
            @pl.when(s == n)
            def _store(n=n):
                mix_s[:, n * TN:(n + 1) * TN] = v

    @pl.when(s == MG_TILES - 1)
    def _first_chunk_out():
        out()

    @pl.when(s >= MG_TILES)
    def _chunk():
        for n in range(MG_TILES):
            mix_s[:, n * TN:(n + 1) * TN] = mix_tile(
                [wg_s[n * N_BRANCH + k] for k in range(N_BRANCH)],
                [wb_s[n * N_BRANCH + k] for k in range(N_BRANCH)])
        out()


def _mixer_out(x, g_mix, modc, att, conv, pool, w_in, w_branch, w_out, l):
    def chunk(s):
        return jnp.maximum(s - (MG_TILES - 1), 0)

    def tile(s):
        return jnp.minimum(s, MG_TILES - 1)

    def mod_spec(k):
        return pl.BlockSpec((None, None, 1, D_MODEL), lambda s: (l, chunk(s) * MG_ROWS // CHUNK, 0, k))

    gate_specs = [pl.BlockSpec((None, D_MODEL, TN), functools.partial(
        lambda s, k: (l, 0, COL_GATE + MG_TILES * k + tile(s)), k=k)) for k in range(N_BRANCH)]
    br_specs = [pl.BlockSpec((None, None, BRANCH_W, TN), functools.partial(
        lambda s, k: (l, k, 0, tile(s)), k=k)) for k in range(N_BRANCH)]
    return pl.pallas_call(
        _mixer_out_kernel,
        grid=(MG_STEPS,),
        in_specs=[pl.BlockSpec((MG_ROWS, D_MODEL), lambda s: (chunk(s), 0)),
                  pl.BlockSpec((None, 1, D_MODEL), lambda s: (l, 0, 0)),
                  mod_spec(0), mod_spec(1), mod_spec(2),
                  pl.BlockSpec((MG_ROWS, 2 * ATT_W), lambda s: (chunk(s), 0)),
                  pl.BlockSpec((MG_ROWS, CONV_W), lambda s: (chunk(s), 0)),
                  pl.BlockSpec((MG_ROWS, POOL_W), lambda s: (chunk(s), 0))] + gate_specs + br_specs
                 + [pl.BlockSpec((None, TN, D_MODEL), lambda s: (l, tile(s), 0))],
        out_specs=pl.BlockSpec((MG_ROWS, D_MODEL), lambda s: (chunk(s), 0)),
        out_shape=jax.ShapeDtypeStruct((ROWS, D_MODEL), F32),
        scratch_shapes=[pltpu.VMEM((MG_TILES * N_BRANCH, D_MODEL, TN), BF16),
                        pltpu.VMEM((MG_TILES * N_BRANCH, BRANCH_W, TN), BF16),
                        pltpu.VMEM((D_MODEL, D_MODEL), BF16),
                        pltpu.VMEM((MG_ROWS, D_MODEL), BF16),
                        pltpu.VMEM((MG_ROWS, D_MODEL), BF16)],
        compiler_params=_params(1),
        name="mixer_out",
    )(x, g_mix.reshape(DEPTH, 1, D_MODEL), modc, modc, modc, att, conv, pool,
      *([w_in] * N_BRANCH), *([w_branch] * N_BRANCH), w_out)


FF_ROWS = 512
FF_CHUNKS = ROWS // FF_ROWS
FF_TILES = D_FF // TN
FF_STEPS = FF_TILES + FF_CHUNKS - 1


def _ffn_kernel(x_ref, g_ref, sh_ref, sc_ref, gate_ref, wa_ref, wb_ref, wd_ref, o_ref,
                wa_s, wb_s, wd_s, h_s, act_s):
    s = pl.program_id(0)

    @pl.when((s == 0) | (s >= FF_TILES))
    def _norm():
        x = x_ref[...]
        y = x * lax.rsqrt(jnp.mean(x * x, axis=-1, keepdims=True) + EPS)
        h_s[...] = ((y * g_ref[...]) * (1.0 + sc_ref[...]) + sh_ref[...]).astype(BF16)

    def up(wa, wb):
        h = h_s[...]
        a = _dot(h, wa)
        return (a * _sigmoid(a) * _dot(h, wb)).astype(BF16)

    def down():
        o_ref[...] = x_ref[...] + gate_ref[...] * _dot(act_s[...], wd_s[...])

    @pl.when(s < FF_TILES)
    def _stream():
        wa = wa_ref[...].astype(BF16)
        wb = wb_ref[...].astype(BF16)
        wa_s[s] = wa
        wb_s[s] = wb
        wd_s[pl.ds(pl.multiple_of(s * TN, TN), TN), :] = wd_ref[...].astype(BF16)
        v = up(wa, wb)
        for j in range(FF_TILES):
            @pl.when(s == j)
            def _store(j=j):
                act_s[:, j * TN:(j + 1) * TN] = v

    @pl.when(s == FF_TILES - 1)
    def _first_chunk_down():
        down()

    @pl.when(s >= FF_TILES)
    def _chunk():
        for j in range(FF_TILES):
            act_s[:, j * TN:(j + 1) * TN] = up(wa_s[j], wb_s[j])
        down()


def _ffn(x, g_ffn, modc, w_gate_up, w_down, l):
    def chunk(s):
        return jnp.maximum(s - (FF_TILES - 1), 0)

    def tile(s):
        return jnp.minimum(s, FF_TILES - 1)

    def mod_spec(k):
        return pl.BlockSpec((None, None, 1, D_MODEL), lambda s: (l, chunk(s) * FF_ROWS // CHUNK, 0, k))

    return pl.pallas_call(
        _ffn_kernel,
        grid=(FF_STEPS,),
        in_specs=[pl.BlockSpec((FF_ROWS, D_MODEL), lambda s: (chunk(s), 0)),
                  pl.BlockSpec((None, 1, D_MODEL), lambda s: (l, 0, 0)),
                  mod_spec(3), mod_spec(4), mod_spec(5),
                  pl.BlockSpec((None, D_MODEL, TN), lambda s: (l, 0, tile(s))),
                  pl.BlockSpec((None, D_MODEL, TN), lambda s: (l, 0, FF_TILES + tile(s))),
                  pl.BlockSpec((None, TN, D_MODEL), lambda s: (l, tile(s), 0))],
        out_specs=pl.BlockSpec((FF_ROWS, D_MODEL), lambda s: (chunk(s), 0)),
        out_shape=jax.ShapeDtypeStruct((ROWS, D_MODEL), F32),
        scratch_shapes=[pltpu.VMEM((FF_TILES, D_MODEL, TN), BF16),
                        pltpu.VMEM((FF_TILES, D_MODEL, TN), BF16),
                        pltpu.VMEM((D_FF, D_MODEL), BF16),
                        pltpu.VMEM((FF_ROWS, D_MODEL), BF16),
                        pltpu.VMEM((FF_ROWS, D_FF), BF16)],
        compiler_params=_params(1),
        name="ffn",
    )(x, g_ffn.reshape(DEPTH, 1, D_MODEL), modc, modc, modc, w_gate_up, w_gate_up, w_down)


def _rope_tables():
    rows = DEC_SEQ // GRID_W
    row = jnp.repeat(jnp.arange(rows, dtype=F32), GRID_W)
    col = jnp.tile(jnp.arange(GRID_W, dtype=F32), rows)
    inv = 1.0 / (ROPE_THETA ** (jnp.arange(N_FREQ, dtype=F32) / N_FREQ))
    cr, sr = jnp.cos(row[:, None] * inv), jnp.sin(row[:, None] * inv)
    cc, sc = jnp.cos(col[:, None] * inv), jnp.sin(col[:, None] * inv)
    cos = jnp.concatenate([cr, cr, cc, cc], axis=1)
    sin = jnp.concatenate([-sr, sr, -sc, sc], axis=1)
    reps = LANES // HEAD_DIM
    return jnp.tile(cos, (1, reps)), jnp.tile(sin, (1, reps))


def kernel(x_prompt, x_sample, cache_k_attn, cache_v_attn, cache_k_win, cache_v_win, c, c_ctx,
           w_mod, b_mod, g_mix, g_ffn, w_in, gq_attn, gk_attn, gq_win, gk_win, sink_win,
           conv_w, pool_w, pool_scale, w_branch, w_out, w_gate_up, w_down):
    x = jnp.concatenate([x_prompt.reshape(P_ROWS, D_MODEL), x_sample.reshape(S_ROWS, D_MODEL)], axis=0)

    cond8 = jnp.zeros((8, D_MODEL), F32).at[0].set(c_ctx).at[1:1 + DEC_BATCH].set(c)
    mod = _modulation(cond8, w_mod, b_mod)
    chunk_cond = np.array([0] * N_P_CHUNKS + [1 + b for b in range(DEC_BATCH)])
    modc = mod[:, chunk_cond].reshape(DEPTH, N_CHUNKS, 1, 6 * D_MODEL)

    cos, sin = _rope_tables()
    gq2 = jnp.stack([jnp.tile(gq_attn, (1, TN // HEAD_DIM)), jnp.tile(gq_win, (1, TN // HEAD_DIM))],
                    axis=1).reshape(DEPTH, 2, 1, TN)
    gk2 = jnp.stack([jnp.tile(gk_attn, (1, KV_W // HEAD_DIM)), jnp.tile(gk_win, (1, KV_W // HEAD_DIM))],
                    axis=1).reshape(DEPTH, 2, 1, KV_W)
    caches = [a.reshape(DEC_BATCH, DEPTH, PAST_LEN, KV_W)
              for a in (cache_k_attn, cache_v_attn, cache_k_win, cache_v_win)]

    new_kv = []
    for l in range(DEPTH):
        h = _norm_mod(x, g_mix, modc, l, 0, 1, "norm_mix")
        q = _proj_q(h, w_in, gq2, cos, sin, l)
        kv, kv_new = _proj_kv(h, w_in, gk2, cos, sin, l)
        conv = _proj_conv(h, w_in, conv_w, l)
        pool = _proj_pool(h, w_in, pool_w, pool_scale, l)
        att = _attention(sink_win, q, kv, *caches, l)
        x = _mixer_out(x, g_mix, modc, att, conv, pool, w_in, w_branch, w_out, l)
        x = _ffn(x, g_ffn, modc, w_gate_up, w_down, l)
        new_kv.append(kv_new)

    new = jnp.stack(new_kv, axis=1).reshape(BATCH, SEQ, DEPTH, 4, N_KV_HEADS, HEAD_DIM)
    new = new.transpose(3, 0, 2, 1, 4, 5)
    y_prompt = x[:P_ROWS].reshape(BATCH, SEQ, D_MODEL)
    y_sample = x[P_ROWS:].reshape(DEC_BATCH, DEC_SEQ, D_MODEL)
    return (y_prompt, y_sample, new[0], new[1], new[2], new[3])
```

```python
import functools

import numpy as np
import jax
import jax.numpy as jnp
from jax import lax
from jax.experimental import pallas as pl
from jax.experimental.pallas import tpu as pltpu

D_MODEL = 1024
BATCH = 16
SEQ = 256
DEPTH = 4
DEC_BATCH = 2
DEC_SEQ = 1024
PAST_LEN = 512
GRID_W = 64
HEAD_DIM = 64
N_Q_HEADS = 8
N_KV_HEADS = 2
ATT_W = N_Q_HEADS * HEAD_DIM
KV_W = N_KV_HEADS * HEAD_DIM
N_FREQ = HEAD_DIM // 4
ROPE_THETA = 10000.0
CONV_W = 512
POOL_W = 512
POOL_SIZES = (2, 4, 8, 16)
N_BRANCH = 4
BRANCH_W = 512
D_FF = 2816
WINDOW = 128
EPS = 1e-6
NEG = -1e30

P_ROWS = BATCH * SEQ
S_ROWS = DEC_BATCH * DEC_SEQ
ROWS = P_ROWS + S_ROWS
CHUNK = 1024
N_CHUNKS = ROWS // CHUNK
N_P_CHUNKS = P_ROWS // CHUNK
SEQ_PER_CHUNK = CHUNK // SEQ
TN = 256
LANES = 128
VMEM_LIMIT = 58 * 1024 * 1024

COL_QA, COL_KVA, COL_QC, COL_KVC = 0, 2, 3, 5
COL_U, COL_GB, COL_GC, COL_PV, COL_GATE = 6, 8, 10, 12, 14

F32 = jnp.float32
BF16 = jnp.bfloat16


def _params():
    return pltpu.CompilerParams(dimension_semantics=("arbitrary",), vmem_limit_bytes=VMEM_LIMIT)


def _dot(a, b):
    return jnp.dot(a, b, preferred_element_type=F32)


def _dot_nt(a, b):
    return lax.dot_general(a, b, (((1,), (1,)), ((), ())), preferred_element_type=F32)


def _sigmoid(x):
    return 1.0 / (1.0 + jnp.exp(-x))


def _norm_modulate(x, g, scale, shift):
    y = x * lax.rsqrt(jnp.mean(x * x, axis=-1, keepdims=True) + EPS)
    return ((y * g) * (1.0 + scale) + shift).astype(BF16)


def _head_segments(width):
    r = lax.broadcasted_iota(jnp.int32, (width, width), 0) // HEAD_DIM
    c = lax.broadcasted_iota(jnp.int32, (width, width), 1) // HEAD_DIM
    return jnp.where(r == c, 1.0, 0.0).astype(BF16)


def _head_mean_square(y, seg):
    sq = y * y
    hi = sq.astype(BF16)
    lo = (sq - hi.astype(F32)).astype(BF16)
    return (_dot(hi, seg) + _dot(lo, seg)) * (1.0 / HEAD_DIM)


def _swap16(x):
    lane = lax.broadcasted_iota(jnp.int32, x.shape, 1)
    fwd = pltpu.roll(x, LANES - 16, axis=1)
    bwd = pltpu.roll(x, 16, axis=1)
    return jnp.where((lane & 16) == 0, fwd, bwd)


def _rope128(x, cos, sin):
    return x * cos + _swap16(x) * sin


def _mod_kernel(cond_ref, w_ref, b_ref, o_ref):
    c = cond_ref[...]
    s = (c * _sigmoid(c)).astype(BF16)
    o_ref[...] = _dot(s, w_ref[...].astype(BF16)) + b_ref[...]


def _modulation(cond8, w_mod, b_mod):
    tn = 1536
    n = 6 * D_MODEL
    return pl.pallas_call(
        _mod_kernel,
        grid=(DEPTH, n // tn),
        in_specs=[pl.BlockSpec((8, D_MODEL), lambda l, j: (0, 0)),
                  pl.BlockSpec((None, D_MODEL, tn), lambda l, j: (l, 0, j)),
                  pl.BlockSpec((None, 1, tn), lambda l, j: (l, 0, j))],
        out_specs=pl.BlockSpec((None, 8, tn), lambda l, j: (l, 0, j)),
        out_shape=jax.ShapeDtypeStruct((DEPTH, 8, n), F32),
        compiler_params=pltpu.CompilerParams(dimension_semantics=("arbitrary", "arbitrary"),
                                             vmem_limit_bytes=VMEM_LIMIT),
        name="modulation",
    )(cond8, w_mod, b_mod.reshape(DEPTH, 1, n))


PJ_TILES = COL_GATE
PJ_STEPS = PJ_TILES + N_CHUNKS


def _seq_pos(latent, width):
    row = lax.broadcasted_iota(jnp.int32, (CHUNK, width), 0)
    return row if latent else row & (SEQ - 1)


def _shift_rows(x, k, pos, seq_len):
    rolled = pltpu.roll(x, (-k) % CHUNK, axis=0)
    ok = (pos + k >= 0) & (pos + k < seq_len)
    return jnp.where(ok, rolled, 0.0)


def _window_mean_minus_token(p, half, pos, seq_len):
    fwd = p
    bwd = _shift_rows(p, -1, pos, seq_len)
    m = 1
    while m < half:
        fwd = fwd + _shift_rows(fwd, m, pos, seq_len)
        bwd = bwd + _shift_rows(bwd, -m, pos, seq_len)
        m *= 2
    count = jnp.minimum(pos + half, seq_len) - jnp.maximum(pos - half, 0)
    return (fwd + bwd) / count.astype(F32) - p


def _proj_chunk(latent, x_ref, g_ref, sh_ref, sc_ref, gq_ref, gk_ref, cos_ref, sin_ref, cw_ref, pw_ref,
                ps_ref, q_ref, kv_ref, conv_ref, pool_ref, new_ref, w_s, h_s):
    h_s[...] = _norm_modulate(x_ref[...], g_ref[...], sc_ref[...], sh_ref[...])
    seq_len = DEC_SEQ if latent else SEQ

    def mm(t):
        return _dot(h_s[...], w_s[t])

    seg = _head_segments(TN)
    for mixer, col in enumerate((COL_QA, COL_QC)):
        for j in range(ATT_W // TN):
            y = mm(col + j)
            y = y * lax.rsqrt(_head_mean_square(y, seg) + EPS) * gq_ref[mixer]
            for s in range(TN // LANES):
                part = y[:, s * LANES:(s + 1) * LANES]
                if latent:
                    part = _rope128(part, cos_ref[...], sin_ref[...])
                out_col = mixer * ATT_W + j * TN + s * LANES
                q_ref[:, out_col:out_col + LANES] = (part * (HEAD_DIM ** -0.5)).astype(BF16)

    seg = _head_segments(KV_W)
    for mixer, col in enumerate((COL_KVA, COL_KVC)):
        y = mm(col)
        k = y[:, :KV_W]
        v = y[:, KV_W:]
        k = k * lax.rsqrt(_head_mean_square(k, seg) + EPS) * gk_ref[mixer]
        if latent:
            k = _rope128(k, cos_ref[...], sin_ref[...])
        else:
            for b in range(SEQ_PER_CHUNK):
                new_ref[b, 2 * mixer] = k[b * SEQ:(b + 1) * SEQ, :].T
                new_ref[b, 2 * mixer + 1] = v[b * SEQ:(b + 1) * SEQ, :].T
        kv_ref[:, 2 * mixer * KV_W:(2 * mixer + 1) * KV_W] = k.astype(BF16)
        kv_ref[:, (2 * mixer + 1) * KV_W:(2 * mixer + 2) * KV_W] = v.astype(BF16)

    pos = _seq_pos(latent, TN)
    for j in range(CONV_W // TN):
        cols = slice(j * TN, (j + 1) * TN)
        z = mm(COL_GC + j) * mm(COL_U + j)
        y = (_shift_rows(z, -1, pos, seq_len) * cw_ref[0:1, cols] + z * cw_ref[1:2, cols]
             + _shift_rows(z, 1, pos, seq_len) * cw_ref[2:3, cols])
        conv_ref[:, cols] = (mm(COL_GB + j) * y).astype(BF16)

    pos = _seq_pos(latent, LANES)
    for j in range(POOL_W // TN):
        p = mm(COL_PV + j)
        for s in range(TN // LANES):
            grp = j * (TN // LANES) + s
            cols = slice(grp * LANES, (grp + 1) * LANES)
            y = _window_mean_minus_token(p[:, s * LANES:(s + 1) * LANES], POOL_SIZES[grp] // 2, pos, seq_len)
            pool_ref[:, cols] = (_dot(y.astype(BF16), pw_ref[grp].astype(BF16)) * ps_ref[:, cols]).astype(BF16)


def _proj_kernel(xp_ref, xs_ref, g_ref, sh_ref, sc_ref, w_ref, gq_ref, gk_ref, cos_ref, sin_ref,
                 cw_ref, pw_ref, ps_ref, q_ref, kv_ref, conv_ref, pool_ref, new_ref, w_s, h_s):
    s = pl.program_id(0)
    rest = (g_ref, sh_ref, sc_ref, gq_ref, gk_ref, cos_ref, sin_ref, cw_ref, pw_ref, ps_ref,
            q_ref, kv_ref, conv_ref, pool_ref, new_ref, w_s, h_s)

    @pl.when(s < PJ_TILES)
    def _stream():
        w_s[s] = w_ref[...].astype(BF16)

    @pl.when((s >= PJ_TILES) & (s < PJ_TILES + N_P_CHUNKS))
    def _prompt():
        _proj_chunk(False, xp_ref, *rest)

    @pl.when(s >= PJ_TILES + N_P_CHUNKS)
    def _latent():
        _proj_chunk(True, xs_ref, *rest)


def _proj(xp, xs, xs_base, g_mix, modc, w_in, gq2, gk2, cos, sin, conv_w, pool_w, pool_scale, l):
    def chunk(s):
        return jnp.clip(s - PJ_TILES, 0, N_CHUNKS - 1)

    def p_chunk(s):
        return jnp.clip(s - PJ_TILES, 0, N_P_CHUNKS - 1)

    def s_chunk(s):
        return xs_base + jnp.clip(s - PJ_TILES - N_P_CHUNKS, 0, N_CHUNKS - N_P_CHUNKS - 1)

    def mod_spec(k):
        return pl.BlockSpec((None, None, 1, D_MODEL), lambda s: (l, chunk(s), 0, k))

    def const(shape):
        return pl.BlockSpec(shape, lambda s: (l,) + (0,) * (len(shape) - 1))

    def rows_out(width):
        return pl.BlockSpec((CHUNK, width), lambda s: (chunk(s), 0))

    return pl.pallas_call(
        _proj_kernel,
        grid=(PJ_STEPS,),
        in_specs=[pl.BlockSpec((CHUNK, D_MODEL), lambda s: (p_chunk(s), 0)),
                  pl.BlockSpec((CHUNK, D_MODEL), lambda s: (s_chunk(s), 0)),
                  const((None, 1, D_MODEL)), mod_spec(0), mod_spec(1),
                  pl.BlockSpec((None, D_MODEL, TN), lambda s: (l, 0, jnp.minimum(s, PJ_TILES - 1))),
                  const((None, 2, 1, TN)), const((None, 2, 1, KV_W)),
                  pl.BlockSpec((DEC_SEQ, LANES), lambda s: (0, 0)),
                  pl.BlockSpec((DEC_SEQ, LANES), lambda s: (0, 0)),
                  const((None, 3, CONV_W)), const((None, len(POOL_SIZES), LANES, LANES)),
                  const((None, 1, POOL_W))],
        out_specs=[rows_out(2 * ATT_W), rows_out(4 * KV_W), rows_out(CONV_W), rows_out(POOL_W),
                   pl.BlockSpec((SEQ_PER_CHUNK, 4, KV_W, SEQ), lambda s: (p_chunk(s), 0, 0, 0))],
        out_shape=[jax.ShapeDtypeStruct((ROWS, 2 * ATT_W), BF16),
                   jax.ShapeDtypeStruct((ROWS, 4 * KV_W), BF16),
                   jax.ShapeDtypeStruct((ROWS, CONV_W), BF16),
                   jax.ShapeDtypeStruct((ROWS, POOL_W), BF16),
                   jax.ShapeDtypeStruct((BATCH, 4, KV_W, SEQ), F32)],
        scratch_shapes=[pltpu.VMEM((PJ_TILES, D_MODEL, TN), BF16),
                        pltpu.VMEM((CHUNK, D_MODEL), BF16)],
        compiler_params=_params(),
        name="proj",
    )(xp, xs, g_mix.reshape(DEPTH, 1, D_MODEL), modc, modc, w_in, gq2, gk2, cos, sin,
      conv_w, pool_w, pool_scale.reshape(DEPTH, 1, POOL_W))


TQ = SEQ
N_QT = DEC_SEQ // TQ
WIN_KEYS = TQ + 2 * WINDOW
PAD_SEQ = WINDOW + DEC_SEQ + WINDOW
ATT_STEPS = BATCH + DEC_BATCH * N_QT


def _split_lanes(x):
    xr = pltpu.roll(x, HEAD_DIM, axis=1)
    low = lax.broadcasted_iota(jnp.int32, x.shape, 1) < HEAD_DIM
    return (jnp.where(low, x, 0.0).astype(BF16), jnp.where(low, 0.0, xr).astype(BF16),
            jnp.where(low, xr, 0.0).astype(BF16), jnp.where(low, 0.0, x).astype(BF16))


def _split_rows(xt):
    z = jnp.zeros((HEAD_DIM, xt.shape[1]), F32)
    h0, h1 = xt[:HEAD_DIM], xt[HEAD_DIM:]
    return tuple(jnp.concatenate(p, axis=0).astype(BF16) for p in ((h0, z), (z, h0), (h1, z), (z, h1)))


def _attend_pair(q, pieces, sinks):
    out = None
    for parity in range(2):
        scores = []
        for k_lo, k_hi, k_t, _, _, _, mask in pieces:
            k = k_hi if parity else k_lo
            s = _dot(q, k) if k_t else _dot_nt(q, k)
            if mask is not None:
                s = jnp.where(mask, s, NEG)
            scores.append(s)
        m = functools.reduce(jnp.maximum, [jnp.max(s, axis=1, keepdims=True) for s in scores])
        if sinks is not None:
            m = jnp.maximum(m, sinks[parity])
        denom = None
        acc = None
        for s, (_, _, _, v_lo, v_hi, v_t, _) in zip(scores, pieces):
            p = jnp.exp(s - m)
            d = jnp.sum(p, axis=1, keepdims=True)
            v = v_hi if parity else v_lo
            a = _dot_nt(p.astype(BF16), v) if v_t else _dot(p.astype(BF16), v)
            denom = d if denom is None else denom + d
            acc = a if acc is None else acc + a
        if sinks is not None:
            denom = denom + jnp.exp(sinks[parity] - m)
        acc = acc / denom
        out = acc if out is None else out + acc
    return out


def _attn_prompt_step(sink_ref, q_ref, kv_ref, o_ref, layer):
    for mixer in range(2):
        k = _split_lanes(kv_ref[:, 2 * mixer * KV_W:(2 * mixer + 1) * KV_W].astype(F32))
        v = _split_lanes(kv_ref[:, (2 * mixer + 1) * KV_W:(2 * mixer + 2) * KV_W].astype(F32))
        for pair in range(N_Q_HEADS // 2):
            lo = 2 * (pair // 2)
            piece = (k[lo], k[lo + 1], False, v[lo], v[lo + 1], False, None)
            sinks = None
            if mixer == 1:
                sinks = (sink_ref[layer, 2 * pair], sink_ref[layer, 2 * pair + 1])
            col = mixer * ATT_W + pair * LANES
            o_ref[:, col:col + LANES] = _attend_pair(q_ref[:, col:col + LANES], [piece], sinks).astype(BF16)


def _attn_latent_step(qt, sink_ref, q_ref, kv_ref, cka_ref, cva_ref, ckc_ref, cvc_ref, o_ref,
                      ctx_s, ka_s, va_s, kc_s, vc_s, layer):
    @pl.when(qt == 0)
    def _fill():
        for i, ref in enumerate((cka_ref, cva_ref, ckc_ref, cvc_ref)):
            for j, part in enumerate(_split_rows(ref[...])):
                ctx_s[4 * i + j] = part
        for dst, col in ((ka_s, 0), (va_s, KV_W)):
            for j, part in enumerate(_split_lanes(kv_ref[:, col:col + KV_W].astype(F32))):
                dst[j] = part
        zeros = jnp.zeros((WINDOW, LANES), BF16)
        for dst, col in ((kc_s, 2 * KV_W), (vc_s, 3 * KV_W)):
            for j, part in enumerate(_split_lanes(kv_ref[:, col:col + KV_W].astype(F32))):
                dst[j, 0:WINDOW, :] = zeros
                dst[j, WINDOW:WINDOW + DEC_SEQ, :] = part
                dst[j, WINDOW + DEC_SEQ:PAD_SEQ, :] = zeros

    q0 = qt * TQ
    win = pl.ds(pl.multiple_of(q0, TQ), WIN_KEYS)
    r = lax.broadcasted_iota(jnp.int32, (TQ, WIN_KEYS), 0)
    jk = lax.broadcasted_iota(jnp.int32, (TQ, WIN_KEYS), 1)
    kpos = q0 - WINDOW + jk
    band = jnp.where((jk - r >= 0) & (jk - r <= 2 * WINDOW), kpos, -1)
    mask = (band >= 0) & (band < DEC_SEQ)

    for pair in range(N_Q_HEADS // 2):
        lo = 2 * (pair // 2)
        hi = lo + 1
        col = pair * LANES
        ctx = (ctx_s[lo], ctx_s[hi], True, ctx_s[4 + lo], ctx_s[4 + hi], True, None)
        cur = (ka_s[lo], ka_s[hi], False, va_s[lo], va_s[hi], False, None)
        o_ref[:, col:col + LANES] = _attend_pair(q_ref[:, col:col + LANES], [ctx, cur], None).astype(BF16)

        col = ATT_W + pair * LANES
        ctx = (ctx_s[8 + lo], ctx_s[8 + hi], True, ctx_s[12 + lo], ctx_s[12 + hi], True, None)
        near = (kc_s[lo, win, :], kc_s[hi, win, :], False, vc_s[lo, win, :], vc_s[hi, win, :], False, mask)
        sinks = (sink_ref[layer, 2 * pair], sink_ref[layer, 2 * pair + 1])
        o_ref[:, col:col + LANES] = _attend_pair(q_ref[:, col:col + LANES], [ctx, near], sinks).astype(BF16)


def _attn_kernel(sink_ref, q_ref, kvp_ref, kvl_ref, cka_ref, cva_ref, ckc_ref, cvc_ref,
                 o_ref, ctx_s, ka_s, va_s, kc_s, vc_s, *, layer):
    s = pl.program_id(0)

    @pl.when(s < BATCH)
    def _prompt():
        _attn_prompt_step(sink_ref, q_ref, kvp_ref, o_ref, layer)

    @pl.when(s >= BATCH)
    def _latent():
        _attn_latent_step((s - BATCH) % N_QT, sink_ref, q_ref, kvl_ref, cka_ref, cva_ref, ckc_ref, cvc_ref,
                          o_ref, ctx_s, ka_s, va_s, kc_s, vc_s, layer)


def _attention(sink, q, kv, cka, cva, ckc, cvc, l):
    def latent_batch(s):
        return jnp.maximum(s - BATCH, 0) // N_QT

    cache_spec = pl.BlockSpec((None, None, KV_W, PAST_LEN), lambda s: (latent_batch(s), l, 0, 0))
    return pl.pallas_call(
        functools.partial(_attn_kernel, layer=l),
        grid=(ATT_STEPS,),
        in_specs=[pl.BlockSpec(memory_space=pltpu.SMEM),
                  pl.BlockSpec((TQ, 2 * ATT_W), lambda s: (s, 0)),
                  pl.BlockSpec((SEQ, 4 * KV_W), lambda s: (jnp.minimum(s, BATCH - 1), 0)),
                  pl.BlockSpec((DEC_SEQ, 4 * KV_W), lambda s: (N_P_CHUNKS + latent_batch(s), 0)),
                  cache_spec, cache_spec, cache_spec, cache_spec],
        out_specs=pl.BlockSpec((TQ, 2 * ATT_W), lambda s: (s, 0)),
        out_shape=jax.ShapeDtypeStruct((ROWS, 2 * ATT_W), BF16),
        scratch_shapes=[pltpu.VMEM((16, KV_W, PAST_LEN), BF16),
                        pltpu.VMEM((4, DEC_SEQ, LANES), BF16), pltpu.VMEM((4, DEC_SEQ, LANES), BF16),
                        pltpu.VMEM((4, PAD_SEQ, LANES), BF16), pltpu.VMEM((4, PAD_SEQ, LANES), BF16)],
        compiler_params=_params(),
        name="attention",
    )(sink, q, kv, kv, cka, cva, ckc, cvc)


MG_ROWS = 512
MG_CHUNKS = ROWS // MG_ROWS
MG_P_CHUNKS = P_ROWS // MG_ROWS
MG_TILES = D_MODEL // TN
MG_STEPS = MG_TILES + MG_CHUNKS - 1
MG_FIRST_LATENT_STEP = MG_TILES - 1 + MG_P_CHUNKS


def _mixer_out_kernel(xp_ref, xs_ref, g_ref, sh_ref, sc_ref, gate_ref, att_ref, conv_ref, pool_ref,
                      wg0, wg1, wg2, wg3, wb0, wb1, wb2, wb3, wo_ref, o_ref,
                      wg_s, wb_s, wo_s, h_s, mix_s):
    s = pl.program_id(0)
    prompt = s < MG_FIRST_LATENT_STEP

    def norm(x_ref):
        h_s[...] = _norm_modulate(x_ref[...], g_ref[...], sc_ref[...], sh_ref[...])

    @pl.when(((s == 0) | (s >= MG_TILES)) & prompt)
    def _norm_prompt():
        norm(xp_ref)

    @pl.when(jnp.logical_not(prompt))
    def _norm_latent():
        norm(xs_ref)

    def mix_tile(wgs, wbs):
        h = h_s[...]
        branches = (att_ref[:, :ATT_W], conv_ref[...], att_ref[:, ATT_W:], pool_ref[...])
        acc = None
        for br, wg, wb in zip(branches, wgs, wbs):
            term = _sigmoid(_dot(h, wg)) * _dot(br, wb)
            acc = term if acc is None else acc + term
        return acc.astype(BF16)

    def out(x_ref):
        o_ref[...] = x_ref[...] + gate_ref[...] * _dot(mix_s[...], wo_s[...])

    @pl.when(s < MG_TILES)
    def _stream():
        wgs = [r[...].astype(BF16) for r in (wg0, wg1, wg2, wg3)]
        wbs = [r[...].astype(BF16) for r in (wb0, wb1, wb2, wb3)]
        for k in range(N_BRANCH):
            wg_s[s * N_BRANCH + k] = wgs[k]
            wb_s[s * N_BRANCH + k] = wbs[k]
        wo_s[pl.ds(pl.multiple_of(s * TN, TN), TN), :] = wo_ref[...].astype(BF16)
        v = mix_tile(wgs, wbs)
        for n in range(MG_TILES):
            @pl.when(s == n)
            def _store(n=n):
                mix_s[:, n * TN:(n + 1) * TN] = v

    @pl.when(s == MG_TILES - 1)
    def _first_chunk_out():
        out(xp_ref)

    @pl.when(s >= MG_TILES)
    def _chunk():
        for n in range(MG_TILES):
            mix_s[:, n * TN:(n + 1) * TN] = mix_tile(
                [wg_s[n * N_BRANCH + k] for k in range(N_BRANCH)],
                [wb_s[n * N_BRANCH + k] for k in range(N_BRANCH)])

        @pl.when(prompt)
        def _out_prompt():
            out(xp_ref)

        @pl.when(jnp.logical_not(prompt))
        def _out_latent():
            out(xs_ref)


def _mixer_out(xp, xs, xs_base, g_mix, modc, att, conv, pool, w_in, w_branch, w_out, l):
    def chunk(s):
        return jnp.maximum(s - (MG_TILES - 1), 0)

    def tile(s):
        return jnp.minimum(s, MG_TILES - 1)

    def mod_spec(k):
        return pl.BlockSpec((None, None, 1, D_MODEL), lambda s: (l, chunk(s) * MG_ROWS // CHUNK, 0, k))

    def rows_in(width):
        return pl.BlockSpec((MG_ROWS, width), lambda s: (chunk(s), 0))

    gate_specs = [pl.BlockSpec((None, D_MODEL, TN), functools.partial(
        lambda s, k: (l, 0, COL_GATE + MG_TILES * k + tile(s)), k=k)) for k in range(N_BRANCH)]
    br_specs = [pl.BlockSpec((None, None, BRANCH_W, TN), functools.partial(
        lambda s, k: (l, k, 0, tile(s)), k=k)) for k in range(N_BRANCH)]
    return pl.pallas_call(
        _mixer_out_kernel,
        grid=(MG_STEPS,),
        in_specs=[pl.BlockSpec((MG_ROWS, D_MODEL), lambda s: (jnp.minimum(chunk(s), MG_P_CHUNKS - 1), 0)),
                  pl.BlockSpec((MG_ROWS, D_MODEL),
                               lambda s: (xs_base + jnp.maximum(chunk(s) - MG_P_CHUNKS, 0), 0)),
                  pl.BlockSpec((None, 1, D_MODEL), lambda s: (l, 0, 0)),
                  mod_spec(0), mod_spec(1), mod_spec(2),
                  rows_in(2 * ATT_W), rows_in(CONV_W), rows_in(POOL_W)] + gate_specs + br_specs
                 + [pl.BlockSpec((None, TN, D_MODEL), lambda s: (l, tile(s), 0))],
        out_specs=pl.BlockSpec((MG_ROWS, D_MODEL), lambda s: (chunk(s), 0)),
        out_shape=jax.ShapeDtypeStruct((ROWS, D_MODEL), F32),
        scratch_shapes=[pltpu.VMEM((MG_TILES * N_BRANCH, D_MODEL, TN), BF16),
                        pltpu.VMEM((MG_TILES * N_BRANCH, BRANCH_W, TN), BF16),
                        pltpu.VMEM((D_MODEL, D_MODEL), BF16),
                        pltpu.VMEM((MG_ROWS, D_MODEL), BF16),
                        pltpu.VMEM((MG_ROWS, D_MODEL), BF16)],
        compiler_params=_params(),
        name="mixer_out",
    )(xp, xs, g_mix.reshape(DEPTH, 1, D_MODEL), modc, modc, modc, att, conv, pool,
      *([w_in] * N_BRANCH), *([w_branch] * N_BRANCH), w_out)


FF_ROWS = 512
FF_CHUNKS = ROWS // FF_ROWS
FF_P_CHUNKS = P_ROWS // FF_ROWS
FF_TILES = D_FF // TN
FF_STEPS = FF_TILES + FF_CHUNKS - 1
FF_FIRST_LATENT_STEP = FF_TILES - 1 + FF_P_CHUNKS


def _ffn_kernel(x_ref, g_ref, sh_ref, sc_ref, gate_ref, wa_ref, wb_ref, wd_ref, *rest, split_out):
    if split_out:
        op_ref, os_ref, wa_s, wb_s, wd_s, h_s, act_s = rest
    else:
        op_ref, wa_s, wb_s, wd_s, h_s, act_s = rest
        os_ref = op_ref
    s = pl.program_id(0)

    @pl.when((s == 0) | (s >= FF_TILES))
    def _norm():
        h_s[...] = _norm_modulate(x_ref[...], g_ref[...], sc_ref[...], sh_ref[...])

    def up(wa, wb):
        h = h_s[...]
        a = _dot(h, wa)
        return (a * _sigmoid(a) * _dot(h, wb)).astype(BF16)

    def down(o_ref):
        o_ref[...] = x_ref[...] + gate_ref[...] * _dot(act_s[...], wd_s[...])

    @pl.when(s < FF_TILES)
    def _stream():
        wa = wa_ref[...].astype(BF16)
        wb = wb_ref[...].astype(BF16)
        wa_s[s] = wa
        wb_s[s] = wb
        wd_s[pl.ds(pl.multiple_of(s * TN, TN), TN), :] = wd_ref[...].astype(BF16)
        v = up(wa, wb)
        for j in range(FF_TILES):
            @pl.when(s == j)
            def _store(j=j):
                act_s[:, j * TN:(j + 1) * TN] = v

    @pl.when(s == FF_TILES - 1)
    def _first_chunk_down():
        down(op_ref)

    @pl.when(s >= FF_TILES)
    def _chunk():
        for j in range(FF_TILES):
            act_s[:, j * TN:(j + 1) * TN] = up(wa_s[j], wb_s[j])
        if split_out:
            @pl.when(s < FF_FIRST_LATENT_STEP)
            def _down_prompt():
                down(op_ref)

            @pl.when(s >= FF_FIRST_LATENT_STEP)
            def _down_latent():
                down(os_ref)
        else:
            down(op_ref)


def _ffn(x, g_ffn, modc, w_gate_up, w_down, l, split_out):
    def chunk(s):
        return jnp.maximum(s - (FF_TILES - 1), 0)

    def tile(s):
        return jnp.minimum(s, FF_TILES - 1)

    def mod_spec(k):
        return pl.BlockSpec((None, None, 1, D_MODEL), lambda s: (l, chunk(s) * FF_ROWS // CHUNK, 0, k))

    if split_out:
        out_specs = [pl.BlockSpec((FF_ROWS, D_MODEL), lambda s: (jnp.minimum(chunk(s), FF_P_CHUNKS - 1), 0)),
                     pl.BlockSpec((FF_ROWS, D_MODEL), lambda s: (jnp.maximum(chunk(s) - FF_P_CHUNKS, 0), 0))]
        out_shape = [jax.ShapeDtypeStruct((P_ROWS, D_MODEL), F32), jax.ShapeDtypeStruct((S_ROWS, D_MODEL), F32)]
    else:
        out_specs = pl.BlockSpec((FF_ROWS, D_MODEL), lambda s: (chunk(s), 0))
        out_shape = jax.ShapeDtypeStruct((ROWS, D_MODEL), F32)
    return pl.pallas_call(
        functools.partial(_ffn_kernel, split_out=split_out),
        grid=(FF_STEPS,),
        in_specs=[pl.BlockSpec((FF_ROWS, D_MODEL), lambda s: (chunk(s), 0)),
                  pl.BlockSpec((None, 1, D_MODEL), lambda s: (l, 0, 0)),
                  mod_spec(3), mod_spec(4), mod_spec(5),
                  pl.BlockSpec((None, D_MODEL, TN), lambda s: (l, 0, tile(s))),
                  pl.BlockSpec((None, D_MODEL, TN), lambda s: (l, 0, FF_TILES + tile(s))),
                  pl.BlockSpec((None, TN, D_MODEL), lambda s: (l, tile(s), 0))],
        out_specs=out_specs,
        out_shape=out_shape,
        scratch_shapes=[pltpu.VMEM((FF_TILES, D_MODEL, TN), BF16),
                        pltpu.VMEM((FF_TILES, D_MODEL, TN), BF16),
                        pltpu.VMEM((D_FF, D_MODEL), BF16),
                        pltpu.VMEM((FF_ROWS, D_MODEL), BF16),
                        pltpu.VMEM((FF_ROWS, D_FF), BF16)],
        compiler_params=_params(),
        name="ffn",
    )(x, g_ffn.reshape(DEPTH, 1, D_MODEL), modc, modc, modc, w_gate_up, w_gate_up, w_down)


def _rope_tables():
    rows = DEC_SEQ // GRID_W
    row = jnp.repeat(jnp.arange(rows, dtype=F32), GRID_W)
    col = jnp.tile(jnp.arange(GRID_W, dtype=F32), rows)
    inv = 1.0 / (ROPE_THETA ** (jnp.arange(N_FREQ, dtype=F32) / N_FREQ))
    cr, sr = jnp.cos(row[:, None] * inv), jnp.sin(row[:, None] * inv)
    cc, sc = jnp.cos(col[:, None] * inv), jnp.sin(col[:, None] * inv)
    cos = jnp.concatenate([cr, cr, cc, cc], axis=1)
    sin = jnp.concatenate([-sr, sr, -sc, sc], axis=1)
    reps = LANES // HEAD_DIM
    return jnp.tile(cos, (1, reps)), jnp.tile(sin, (1, reps))


def kernel(x_prompt, x_sample, cache_k_attn, cache_v_attn, cache_k_win, cache_v_win, c, c_ctx,
           w_mod, b_mod, g_mix, g_ffn, w_in, gq_attn, gk_attn, gq_win, gk_win, sink_win,
           conv_w, pool_w, pool_scale, w_branch, w_out, w_gate_up, w_down):
    cond8 = jnp.zeros((8, D_MODEL), F32).at[0].set(c_ctx).at[1:1 + DEC_BATCH].set(c)
    mod = _modulation(cond8, w_mod, b_mod)
    chunk_cond = np.array([0] * N_P_CHUNKS + [1 + b for b in range(DEC_BATCH)])
    modc = mod[:, chunk_cond].reshape(DEPTH, N_CHUNKS, 1, 6 * D_MODEL)

    cos, sin = _rope_tables()
    gq2 = jnp.stack([jnp.tile(gq_attn, (1, TN // HEAD_DIM)), jnp.tile(gq_win, (1, TN // HEAD_DIM))],
                    axis=1).reshape(DEPTH, 2, 1, TN)
    gk2 = jnp.stack([jnp.tile(gk_attn, (1, KV_W // HEAD_DIM)), jnp.tile(gk_win, (1, KV_W // HEAD_DIM))],
                    axis=1).reshape(DEPTH, 2, 1, KV_W)
    caches = [a.transpose(0, 1, 3, 4, 2).reshape(DEC_BATCH, DEPTH, KV_W, PAST_LEN)
              for a in (cache_k_attn, cache_v_attn, cache_k_win, cache_v_win)]

    xp = x_prompt.reshape(P_ROWS, D_MODEL)
    xs = x_sample.reshape(S_ROWS, D_MODEL)
    xs_chunks, xs_mg = 0, 0
    new_kv = []
    for l in range(DEPTH):
        q, kv, conv, pool, new = _proj(xp, xs, xs_chunks, g_mix, modc, w_in, gq2, gk2, cos, sin,
                                       conv_w, pool_w, pool_scale, l)
        new_kv.append(new)
        att = _attention(sink_win, q, kv, *caches, l)
        x = _mixer_out(xp, xs, xs_mg, g_mix, modc, att, conv, pool, w_in, w_branch, w_out, l)
        if l + 1 < DEPTH:
            x = _ffn(x, g_ffn, modc, w_gate_up, w_down, l, False)
            xp = xs = x
            xs_chunks, xs_mg = N_P_CHUNKS, MG_P_CHUNKS
        else:
            y_prompt, y_sample = _ffn(x, g_ffn, modc, w_gate_up, w_down, l, True)

    outs = []
    for kind in range(4):
        a = jnp.stack([n[:, kind] for n in new_kv], axis=1)
        outs.append(a.reshape(BATCH, DEPTH, N_KV_HEADS, HEAD_DIM, SEQ).transpose(0, 1, 4, 2, 3))
    return (y_prompt.reshape(BATCH, SEQ, D_MODEL), y_sample.reshape(DEC_BATCH, DEC_SEQ, D_MODEL), *outs)
```

```python
import functools

import numpy as np
import jax
import jax.numpy as jnp
from jax import lax
from jax.experimental import pallas as pl
from jax.experimental.pallas import tpu as pltpu

D_MODEL = 1024
BATCH = 16
SEQ = 256
DEPTH = 4
DEC_BATCH = 2
DEC_SEQ = 1024
PAST_LEN = 512
GRID_W = 64
HEAD_DIM = 64
N_Q_HEADS = 8
N_KV_HEADS = 2
ATT_W = N_Q_HEADS * HEAD_DIM
KV_W = N_KV_HEADS * HEAD_DIM
N_FREQ = HEAD_DIM // 4
ROPE_THETA = 10000.0
CONV_W = 512
POOL_W = 512
POOL_SIZES = (2, 4, 8, 16)
N_BRANCH = 4
BRANCH_W = 512
D_FF = 2816
WINDOW = 128
EPS = 1e-6
NEG = -1e30

P_ROWS = BATCH * SEQ
S_ROWS = DEC_BATCH * DEC_SEQ
ROWS = P_ROWS + S_ROWS
CHUNK = 1024
N_CHUNKS = ROWS // CHUNK
N_P_CHUNKS = P_ROWS // CHUNK
SEQ_PER_CHUNK = CHUNK // SEQ
TN = 256
LANES = 128
VMEM_LIMIT = 58 * 1024 * 1024

COL_QA, COL_KVA, COL_QC, COL_KVC = 0, 2, 3, 5
COL_U, COL_GB, COL_GC, COL_PV, COL_GATE = 6, 8, 10, 12, 14

F32 = jnp.float32
BF16 = jnp.bfloat16


def _params():
    return pltpu.CompilerParams(dimension_semantics=("arbitrary",), vmem_limit_bytes=VMEM_LIMIT)


def _dot(a, b):
    return jnp.dot(a, b, preferred_element_type=F32)


def _dot_nt(a, b):
    return lax.dot_general(a, b, (((1,), (1,)), ((), ())), preferred_element_type=F32)


def _sigmoid(x):
    return 1.0 / (1.0 + jnp.exp(-x))


def _norm_modulate(x, g, scale, shift):
    y = x * lax.rsqrt(jnp.mean(x * x, axis=-1, keepdims=True) + EPS)
    return ((y * g) * (1.0 + scale) + shift).astype(BF16)


def _head_segments(width):
    r = lax.broadcasted_iota(jnp.int32, (width, width), 0) // HEAD_DIM
    c = lax.broadcasted_iota(jnp.int32, (width, width), 1) // HEAD_DIM
    return jnp.where(r == c, 1.0 / HEAD_DIM, 0.0).astype(BF16)


def _head_mean_square(y, seg):
    sq = y * y
    hi = sq.astype(BF16)
    lo = (sq - hi.astype(F32)).astype(BF16)
    return _dot(hi, seg) + _dot(lo, seg)


def _swap16(x):
    lane = lax.broadcasted_iota(jnp.int32, x.shape, 1)
    fwd = pltpu.roll(x, LANES - 16, axis=1)
    bwd = pltpu.roll(x, 16, axis=1)
    return jnp.where((lane & 16) == 0, fwd, bwd)


def _rope128(x, cos, sin):
    return x * cos + _swap16(x) * sin


def _mod_kernel(cond_ref, w_ref, b_ref, o_ref):
    c = cond_ref[...]
    s = (c * _sigmoid(c)).astype(BF16)
    o_ref[...] = _dot(s, w_ref[...].astype(BF16)) + b_ref[...]


def _modulation(cond8, w_mod, b_mod):
    tn = 1536
    n = 6 * D_MODEL
    return pl.pallas_call(
        _mod_kernel,
        grid=(DEPTH, n // tn),
        in_specs=[pl.BlockSpec((8, D_MODEL), lambda l, j: (0, 0)),
                  pl.BlockSpec((None, D_MODEL, tn), lambda l, j: (l, 0, j)),
                  pl.BlockSpec((None, 1, tn), lambda l, j: (l, 0, j))],
        out_specs=pl.BlockSpec((None, 8, tn), lambda l, j: (l, 0, j)),
        out_shape=jax.ShapeDtypeStruct((DEPTH, 8, n), F32),
        compiler_params=pltpu.CompilerParams(dimension_semantics=("arbitrary", "arbitrary"),
                                             vmem_limit_bytes=VMEM_LIMIT),
        name="modulation",
    )(cond8, w_mod, b_mod.reshape(DEPTH, 1, n))


PJ_TILES = COL_GATE
PJ_STEPS = PJ_TILES + N_CHUNKS


def _seq_pos(latent, width):
    row = lax.broadcasted_iota(jnp.int32, (CHUNK, width), 0)
    return row if latent else row & (SEQ - 1)


def _shift_rows(x, k, pos, seq_len):
    rolled = pltpu.roll(x, (-k) % CHUNK, axis=0)
    ok = (pos + k >= 0) & (pos + k < seq_len)
    return jnp.where(ok, rolled, 0.0)


def _window_mean_minus_token(p, half, pos, seq_len):
    fwd = p
    bwd = _shift_rows(p, -1, pos, seq_len)
    m = 1
    while m < half:
        fwd = fwd + _shift_rows(fwd, m, pos, seq_len)
        bwd = bwd + _shift_rows(bwd, -m, pos, seq_len)
        m *= 2
    count = jnp.minimum(pos + half, seq_len) - jnp.maximum(pos - half, 0)
    return (fwd + bwd) / count.astype(F32) - p


def _proj_chunk(latent, x_ref, g_ref, sh_ref, sc_ref, gq_ref, gk_ref, cos_ref, sin_ref, cw_ref, pw_ref,
                ps_ref, q_ref, kv_ref, conv_ref, pool_ref, new_ref, w_s, h_s):
    h_s[...] = _norm_modulate(x_ref[...], g_ref[...], sc_ref[...], sh_ref[...])
    seq_len = DEC_SEQ if latent else SEQ
    seg_q = _head_segments(TN)
    seg_k = _head_segments(KV_W)
    pos_wide = _seq_pos(latent, TN)
    pos = _seq_pos(latent, LANES)

    def q_tile(mixer, j, y):
        y = y * lax.rsqrt(_head_mean_square(y, seg_q) + EPS) * (gq_ref[mixer] * (HEAD_DIM ** -0.5))
        for s in range(TN // LANES):
            part = y[:, s * LANES:(s + 1) * LANES]
            if latent:
                part = _rope128(part, cos_ref[...], sin_ref[...])
            out_col = mixer * ATT_W + j * TN + s * LANES
            q_ref[:, out_col:out_col + LANES] = part.astype(BF16)

    def kv_tile(mixer, y):
        k = y[:, :KV_W]
        v = y[:, KV_W:]
        k = k * lax.rsqrt(_head_mean_square(k, seg_k) + EPS) * gk_ref[mixer]
        if latent:
            k = _rope128(k, cos_ref[...], sin_ref[...])
        else:
            for b in range(SEQ_PER_CHUNK):
                new_ref[b, 2 * mixer] = k[b * SEQ:(b + 1) * SEQ, :].T
                new_ref[b, 2 * mixer + 1] = v[b * SEQ:(b + 1) * SEQ, :].T
        kv_ref[:, 2 * mixer * KV_W:(2 * mixer + 1) * KV_W] = k.astype(BF16)
        kv_ref[:, (2 * mixer + 1) * KV_W:(2 * mixer + 2) * KV_W] = v.astype(BF16)

    def conv_tile(j, gc, u, gb):
        cols = slice(j * TN, (j + 1) * TN)
        z = gc * u
        y = (_shift_rows(z, -1, pos_wide, seq_len) * cw_ref[0:1, cols] + z * cw_ref[1:2, cols]
             + _shift_rows(z, 1, pos_wide, seq_len) * cw_ref[2:3, cols])
        conv_ref[:, cols] = (gb * y).astype(BF16)

    def pool_tile(j, p):
        for s in range(TN // LANES):
            grp = j * (TN // LANES) + s
            cols = slice(grp * LANES, (grp + 1) * LANES)
            y = _window_mean_minus_token(p[:, s * LANES:(s + 1) * LANES], POOL_SIZES[grp] // 2, pos, seq_len)
            pool_ref[:, cols] = (_dot(y.astype(BF16), pw_ref[grp].astype(BF16)) * ps_ref[:, cols]).astype(BF16)

    tasks = []
    for mixer, col in enumerate((COL_QA, COL_QC)):
        for j in range(ATT_W // TN):
            tasks.append(((col + j,), functools.partial(q_tile, mixer, j)))
    for mixer, col in enumerate((COL_KVA, COL_KVC)):
        tasks.append(((col,), functools.partial(kv_tile, mixer)))
    for j in range(CONV_W // TN):
        tasks.append(((COL_GC + j, COL_U + j, COL_GB + j), functools.partial(conv_tile, j)))
    for j in range(POOL_W // TN):
        tasks.append(((COL_PV + j,), functools.partial(pool_tile, j)))

    def matmuls(tiles):
        return [_dot(h_s[...], w_s[t]) for t in tiles]

    ys = matmuls(tasks[0][0])
    for i, (_, epilogue) in enumerate(tasks):
        nxt = matmuls(tasks[i + 1][0]) if i + 1 < len(tasks) else None
        epilogue(*ys)
        ys = nxt


def _proj_kernel(xp_ref, xs_ref, g_ref, sh_ref, sc_ref, w_ref, gq_ref, gk_ref, cos_ref, sin_ref,
                 cw_ref, pw_ref, ps_ref, q_ref, kv_ref, conv_ref, pool_ref, new_ref, w_s, h_s):
    s = pl.program_id(0)
    rest = (g_ref, sh_ref, sc_ref, gq_ref, gk_ref, cos_ref, sin_ref, cw_ref, pw_ref, ps_ref,
            q_ref, kv_ref, conv_ref, pool_ref, new_ref, w_s, h_s)

    @pl.when(s < PJ_TILES)
    def _stream():
        w_s[s] = w_ref[...].astype(BF16)

    @pl.when((s >= PJ_TILES) & (s < PJ_TILES + N_P_CHUNKS))
    def _prompt():
        _proj_chunk(False, xp_ref, *rest)

    @pl.when(s >= PJ_TILES + N_P_CHUNKS)
    def _latent():
        _proj_chunk(True, xs_ref, *rest)


def _proj(xp, xs, xs_base, g_mix, modc, w_in, gq2, gk2, cos, sin, conv_w, pool_w, pool_scale, l):
    def chunk(s):
        return jnp.clip(s - PJ_TILES, 0, N_CHUNKS - 1)

    def p_chunk(s):
        return jnp.clip(s - PJ_TILES, 0, N_P_CHUNKS - 1)

    def s_chunk(s):
        return xs_base + jnp.clip(s - PJ_TILES - N_P_CHUNKS, 0, N_CHUNKS - N_P_CHUNKS - 1)

    def mod_spec(k):
        return pl.BlockSpec((None, None, 1, D_MODEL), lambda s: (l, chunk(s), 0, k))

    def const(shape):
        return pl.BlockSpec(shape, lambda s: (l,) + (0,) * (len(shape) - 1))

    def rows_out(width):
        return pl.BlockSpec((CHUNK, width), lambda s: (chunk(s), 0))

    return pl.pallas_call(
        _proj_kernel,
        grid=(PJ_STEPS,),
        in_specs=[pl.BlockSpec((CHUNK, D_MODEL), lambda s: (p_chunk(s), 0)),
                  pl.BlockSpec((CHUNK, D_MODEL), lambda s: (s_chunk(s), 0)),
                  const((None, 1, D_MODEL)), mod_spec(0), mod_spec(1),
                  pl.BlockSpec((None, D_MODEL, TN), lambda s: (l, 0, jnp.minimum(s, PJ_TILES - 1))),
                  const((None, 2, 1, TN)), const((None, 2, 1, KV_W)),
                  pl.BlockSpec((DEC_SEQ, LANES), lambda s: (0, 0)),
                  pl.BlockSpec((DEC_SEQ, LANES), lambda s: (0, 0)),
                  const((None, 3, CONV_W)), const((None, len(POOL_SIZES), LANES, LANES)),
                  const((None, 1, POOL_W))],
        out_specs=[rows_out(2 * ATT_W), rows_out(4 * KV_W), rows_out(CONV_W), rows_out(POOL_W),
                   pl.BlockSpec((SEQ_PER_CHUNK, 4, KV_W, SEQ), lambda s: (p_chunk(s), 0, 0, 0))],
        out_shape=[jax.ShapeDtypeStruct((ROWS, 2 * ATT_W), BF16),
                   jax.ShapeDtypeStruct((ROWS, 4 * KV_W), BF16),
                   jax.ShapeDtypeStruct((ROWS, CONV_W), BF16),
                   jax.ShapeDtypeStruct((ROWS, POOL_W), BF16),
                   jax.ShapeDtypeStruct((BATCH, 4, KV_W, SEQ), F32)],
        scratch_shapes=[pltpu.VMEM((PJ_TILES, D_MODEL, TN), BF16),
                        pltpu.VMEM((CHUNK, D_MODEL), BF16)],
        compiler_params=_params(),
        name="proj",
    )(xp, xs, g_mix.reshape(DEPTH, 1, D_MODEL), modc, modc, w_in, gq2, gk2, cos, sin,
      conv_w, pool_w, pool_scale.reshape(DEPTH, 1, POOL_W))


TQ = SEQ
N_QT = DEC_SEQ // TQ
WIN_KEYS = TQ + 2 * WINDOW
PAD_SEQ = WINDOW + DEC_SEQ + WINDOW
ATT_STEPS = BATCH + DEC_BATCH * N_QT


def _split_lanes(x):
    xr = pltpu.roll(x, HEAD_DIM, axis=1)
    low = lax.broadcasted_iota(jnp.int32, x.shape, 1) < HEAD_DIM
    return (jnp.where(low, x, 0.0).astype(BF16), jnp.where(low, 0.0, xr).astype(BF16),
            jnp.where(low, xr, 0.0).astype(BF16), jnp.where(low, 0.0, x).astype(BF16))


def _split_rows(xt):
    z = jnp.zeros((HEAD_DIM, xt.shape[1]), F32)
    h0, h1 = xt[:HEAD_DIM], xt[HEAD_DIM:]
    return tuple(jnp.concatenate(p, axis=0).astype(BF16) for p in ((h0, z), (z, h0), (h1, z), (z, h1)))


def _attend_pairs(tasks):
    units = [(t, parity) for t in range(len(tasks)) for parity in range(2)]

    def scores_of(unit):
        t, parity = unit
        q, pieces, _, _ = tasks[t]
        out = []
        for k_lo, k_hi, k_t, _, _, _, mask in pieces:
            k = k_hi if parity else k_lo
            s = _dot(q, k) if k_t else _dot_nt(q, k)
            if mask is not None:
                s = jnp.where(mask, s, NEG)
            out.append(s)
        return out

    scores = scores_of(units[0])
    even = None
    for i, (t, parity) in enumerate(units):
        nxt = scores_of(units[i + 1]) if i + 1 < len(units) else None
        _, pieces, sinks, write = tasks[t]
        m = functools.reduce(jnp.maximum, [jnp.max(s, axis=1, keepdims=True) for s in scores])
        if sinks is not None:
            m = jnp.maximum(m, sinks[parity])
        denom = None
        acc = None
        for s, (_, _, _, v_lo, v_hi, v_t, _) in zip(scores, pieces):
            p = jnp.exp(s - m)
            d = jnp.sum(p, axis=1, keepdims=True)
            v = v_hi if parity else v_lo
            a = _dot_nt(p.astype(BF16), v) if v_t else _dot(p.astype(BF16), v)
            denom = d if denom is None else denom + d
            acc = a if acc is None else acc + a
        if sinks is not None:
            denom = denom + jnp.exp(sinks[parity] - m)
        acc = acc / denom
        if parity == 0:
            even = acc
        else:
            write(even + acc)
        scores = nxt


def _attn_prompt_step(sink_ref, q_ref, kv_ref, o_ref, layer):
    def writer(col):
        def write(out):
            o_ref[:, col:col + LANES] = out.astype(BF16)
        return write

    tasks = []
    for mixer in range(2):
        k = _split_lanes(kv_ref[:, 2 * mixer * KV_W:(2 * mixer + 1) * KV_W].astype(F32))
        v = _split_lanes(kv_ref[:, (2 * mixer + 1) * KV_W:(2 * mixer + 2) * KV_W].astype(F32))
        for pair in range(N_Q_HEADS // 2):
            lo = 2 * (pair // 2)
            piece = (k[lo], k[lo + 1], False, v[lo], v[lo + 1], False, None)
            sinks = None
            if mixer == 1:
                sinks = (sink_ref[layer, 2 * pair], sink_ref[layer, 2 * pair + 1])
            col = mixer * ATT_W + pair * LANES
            tasks.append((q_ref[:, col:col + LANES], [piece], sinks, writer(col)))
    _attend_pairs(tasks)


def _attn_latent_step(qt, sink_ref, q_ref, kv_ref, cka_ref, cva_ref, ckc_ref, cvc_ref, o_ref,
                      ctx_s, ka_s, va_s, kc_s, vc_s, layer):
    @pl.when(qt == 0)
    def _fill():
        for i, ref in enumerate((cka_ref, cva_ref, ckc_ref, cvc_ref)):
            for j, part in enumerate(_split_rows(ref[...])):
                ctx_s[4 * i + j] = part
        for dst, col in ((ka_s, 0), (va_s, KV_W)):
            for j, part in enumerate(_split_lanes(kv_ref[:, col:col + KV_W].astype(F32))):
                dst[j] = part
        zeros = jnp.zeros((WINDOW, LANES), BF16)
        for dst, col in ((kc_s, 2 * KV_W), (vc_s, 3 * KV_W)):
            for j, part in enumerate(_split_lanes(kv_ref[:, col:col + KV_W].astype(F32))):
                dst[j, 0:WINDOW, :] = zeros
                dst[j, WINDOW:WINDOW + DEC_SEQ, :] = part
                dst[j, WINDOW + DEC_SEQ:PAD_SEQ, :] = zeros

    q0 = qt * TQ
    win = pl.ds(pl.multiple_of(q0, TQ), WIN_KEYS)
    r = lax.broadcasted_iota(jnp.int32, (TQ, WIN_KEYS), 0)
    jk = lax.broadcasted_iota(jnp.int32, (TQ, WIN_KEYS), 1)
    kpos = q0 - WINDOW + jk
    band = jnp.where((jk - r >= 0) & (jk - r <= 2 * WINDOW), kpos, -1)
    mask = (band >= 0) & (band < DEC_SEQ)

    def writer(col):
        def write(out):
            o_ref[:, col:col + LANES] = out.astype(BF16)
        return write

    tasks = []
    for pair in range(N_Q_HEADS // 2):
        lo = 2 * (pair // 2)
        hi = lo + 1
        col = pair * LANES
        ctx = (ctx_s[lo], ctx_s[hi], True, ctx_s[4 + lo], ctx_s[4 + hi], True, None)
        cur = (ka_s[lo], ka_s[hi], False, va_s[lo], va_s[hi], False, None)
        tasks.append((q_ref[:, col:col + LANES], [ctx, cur], None, writer(col)))

        col = ATT_W + pair * LANES
        ctx = (ctx_s[8 + lo], ctx_s[8 + hi], True, ctx_s[12 + lo], ctx_s[12 + hi], True, None)
        near = (kc_s[lo, win, :], kc_s[hi, win, :], False, vc_s[lo, win, :], vc_s[hi, win, :], False, mask)
        sinks = (sink_ref[layer, 2 * pair], sink_ref[layer, 2 * pair + 1])
        tasks.append((q_ref[:, col:col + LANES], [ctx, near], sinks, writer(col)))
    _attend_pairs(tasks)


def _attn_kernel(sink_ref, q_ref, kvp_ref, kvl_ref, cka_ref, cva_ref, ckc_ref, cvc_ref,
                 o_ref, ctx_s, ka_s, va_s, kc_s, vc_s, *, layer):
    s = pl.program_id(0)

    @pl.when(s < BATCH)
    def _prompt():
        _attn_prompt_step(sink_ref, q_ref, kvp_ref, o_ref, layer)

    @pl.when(s >= BATCH)
    def _latent():
        _attn_latent_step((s - BATCH) % N_QT, sink_ref, q_ref, kvl_ref, cka_ref, cva_ref, ckc_ref, cvc_ref,
                          o_ref, ctx_s, ka_s, va_s, kc_s, vc_s, layer)


def _attention(sink, q, kv, cka, cva, ckc, cvc, l):
    def latent_batch(s):
        return jnp.maximum(s - BATCH, 0) // N_QT

    cache_spec = pl.BlockSpec((None, None, KV_W, PAST_LEN), lambda s: (latent_batch(s), l, 0, 0))
    return pl.pallas_call(
        functools.partial(_attn_kernel, layer=l),
        grid=(ATT_STEPS,),
        in_specs=[pl.BlockSpec(memory_space=pltpu.SMEM),
                  pl.BlockSpec((TQ, 2 * ATT_W), lambda s: (s, 0)),
                  pl.BlockSpec((SEQ, 4 * KV_W), lambda s: (jnp.minimum(s, BATCH - 1), 0)),
                  pl.BlockSpec((DEC_SEQ, 4 * KV_W), lambda s: (N_P_CHUNKS + latent_batch(s), 0)),
                  cache_spec, cache_spec, cache_spec, cache_spec],
        out_specs=pl.BlockSpec((TQ, 2 * ATT_W), lambda s: (s, 0)),
        out_shape=jax.ShapeDtypeStruct((ROWS, 2 * ATT_W), BF16),
        scratch_shapes=[pltpu.VMEM((16, KV_W, PAST_LEN), BF16),
                        pltpu.VMEM((4, DEC_SEQ, LANES), BF16), pltpu.VMEM((4, DEC_SEQ, LANES), BF16),
                        pltpu.VMEM((4, PAD_SEQ, LANES), BF16), pltpu.VMEM((4, PAD_SEQ, LANES), BF16)],
        compiler_params=_params(),
        name="attention",
    )(sink, q, kv, kv, cka, cva, ckc, cvc)


MG_ROWS = 512
MG_CHUNKS = ROWS // MG_ROWS
MG_P_CHUNKS = P_ROWS // MG_ROWS
MG_TILES = D_MODEL // TN
MG_STEPS = MG_TILES + MG_CHUNKS - 1
MG_FIRST_LATENT_STEP = MG_TILES - 1 + MG_P_CHUNKS


def _mixer_out_kernel(xp_ref, xs_ref, g_ref, sh_ref, sc_ref, gate_ref, att_ref, conv_ref, pool_ref,
                      wg0, wg1, wg2, wg3, wb0, wb1, wb2, wb3, wo_ref, o_ref,
                      wg_s, wb_s, wo_s, h_s, mix_s):
    s = pl.program_id(0)
    prompt = s < MG_FIRST_LATENT_STEP

    def norm(x_ref):
        h_s[...] = _norm_modulate(x_ref[...], g_ref[...], sc_ref[...], sh_ref[...])

    @pl.when(((s == 0) | (s >= MG_TILES)) & prompt)
    def _norm_prompt():
        norm(xp_ref)

    @pl.when(jnp.logical_not(prompt))
    def _norm_latent():
        norm(xs_ref)

    def mix_tile(wgs, wbs):
        h = h_s[...]
        branches = (att_ref[:, :ATT_W], conv_ref[...], att_ref[:, ATT_W:], pool_ref[...])
        acc = None
        for br, wg, wb in zip(branches, wgs, wbs):
            term = _sigmoid(_dot(h, wg)) * _dot(br, wb)
            acc = term if acc is None else acc + term
        return acc.astype(BF16)

    def out(x_ref):
        o_ref[...] = x_ref[...] + gate_ref[...] * _dot(mix_s[...], wo_s[...])

    @pl.when(s < MG_TILES)
    def _stream():
        wgs = [r[...].astype(BF16) for r in (wg0, wg1, wg2, wg3)]
        wbs = [r[...].astype(BF16) for r in (wb0, wb1, wb2, wb3)]
        for k in range(N_BRANCH):
            wg_s[s * N_BRANCH + k] = wgs[k]
            wb_s[s * N_BRANCH + k] = wbs[k]
        wo_s[pl.ds(pl.multiple_of(s * TN, TN), TN), :] = wo_ref[...].astype(BF16)
        v = mix_tile(wgs, wbs)
        for n in range(MG_TILES):
            @pl.when(s == n)
            def _store(n=n):
                mix_s[:, n * TN:(n + 1) * TN] = v

    @pl.when(s == MG_TILES - 1)
    def _first_chunk_out():
        out(xp_ref)

    @pl.when(s >= MG_TILES)
    def _chunk():
        for n in range(MG_TILES):
            mix_s[:, n * TN:(n + 1) * TN] = mix_tile(
                [wg_s[n * N_BRANCH + k] for k in range(N_BRANCH)],
                [wb_s[n * N_BRANCH + k] for k in range(N_BRANCH)])

        @pl.when(prompt)
        def _out_prompt():
            out(xp_ref)

        @pl.when(jnp.logical_not(prompt))
        def _out_latent():
            out(xs_ref)


def _mixer_out(xp, xs, xs_base, g_mix, modc, att, conv, pool, w_in, w_branch, w_out, l):
    def chunk(s):
        return jnp.maximum(s - (MG_TILES - 1), 0)

    def tile(s):
        return jnp.minimum(s, MG_TILES - 1)

    def mod_spec(k):
        return pl.BlockSpec((None, None, 1, D_MODEL), lambda s: (l, chunk(s) * MG_ROWS // CHUNK, 0, k))

    def rows_in(width):
        return pl.BlockSpec((MG_ROWS, width), lambda s: (chunk(s), 0))

    gate_specs = [pl.BlockSpec((None, D_MODEL, TN), functools.partial(
        lambda s, k: (l, 0, COL_GATE + MG_TILES * k + tile(s)), k=k)) for k in range(N_BRANCH)]
    br_specs = [pl.BlockSpec((None, None, BRANCH_W, TN), functools.partial(
        lambda s, k: (l, k, 0, tile(s)), k=k)) for k in range(N_BRANCH)]
    return pl.pallas_call(
        _mixer_out_kernel,
        grid=(MG_STEPS,),
        in_specs=[pl.BlockSpec((MG_ROWS, D_MODEL), lambda s: (jnp.minimum(chunk(s), MG_P_CHUNKS - 1), 0)),
                  pl.BlockSpec((MG_ROWS, D_MODEL),
                               lambda s: (xs_base + jnp.maximum(chunk(s) - MG_P_CHUNKS, 0), 0)),
                  pl.BlockSpec((None, 1, D_MODEL), lambda s: (l, 0, 0)),
                  mod_spec(0), mod_spec(1), mod_spec(2),
                  rows_in(2 * ATT_W), rows_in(CONV_W), rows_in(POOL_W)] + gate_specs + br_specs
                 + [pl.BlockSpec((None, TN, D_MODEL), lambda s: (l, tile(s), 0))],
        out_specs=pl.BlockSpec((MG_ROWS, D_MODEL), lambda s: (chunk(s), 0)),
        out_shape=jax.ShapeDtypeStruct((ROWS, D_MODEL), F32),
        scratch_shapes=[pltpu.VMEM((MG_TILES * N_BRANCH, D_MODEL, TN), BF16),
                        pltpu.VMEM((MG_TILES * N_BRANCH, BRANCH_W, TN), BF16),
                        pltpu.VMEM((D_MODEL, D_MODEL), BF16),
                        pltpu.VMEM((MG_ROWS, D_MODEL), BF16),
                        pltpu.VMEM((MG_ROWS, D_MODEL), BF16)],
        compiler_params=_params(),
        name="mixer_out",
    )(xp, xs, g_mix.reshape(DEPTH, 1, D_MODEL), modc, modc, modc, att, conv, pool,
      *([w_in] * N_BRANCH), *([w_branch] * N_BRANCH), w_out)


FF_ROWS = 512
FF_CHUNKS = ROWS // FF_ROWS
FF_P_CHUNKS = P_ROWS // FF_ROWS
FF_TILES = D_FF // TN
FF_STEPS = FF_TILES + FF_CHUNKS - 1
FF_FIRST_LATENT_STEP = FF_TILES - 1 + FF_P_CHUNKS


def _ffn_kernel(x_ref, g_ref, sh_ref, sc_ref, gate_ref, wa_ref, wb_ref, wd_ref, *rest, split_out):
    if split_out:
        op_ref, os_ref, wa_s, wb_s, wd_s, h_s, act_s = rest
    else:
        op_ref, wa_s, wb_s, wd_s, h_s, act_s = rest
        os_ref = op_ref
    s = pl.program_id(0)

    @pl.when((s == 0) | (s >= FF_TILES))
    def _norm():
        h_s[...] = _norm_modulate(x_ref[...], g_ref[...], sc_ref[...], sh_ref[...])

    def up(wa, wb):
        h = h_s[...]
        a = _dot(h, wa)
        return (a * _sigmoid(a) * _dot(h, wb)).astype(BF16)

    def down(o_ref):
        o_ref[...] = x_ref[...] + gate_ref[...] * _dot(act_s[...], wd_s[...])

    @pl.when(s < FF_TILES)
    def _stream():
        wa = wa_ref[...].astype(BF16)
        wb = wb_ref[...].astype(BF16)
        wa_s[s] = wa
        wb_s[s] = wb
        wd_s[pl.ds(pl.multiple_of(s * TN, TN), TN), :] = wd_ref[...].astype(BF16)
        v = up(wa, wb)
        for j in range(FF_TILES):
            @pl.when(s == j)
            def _store(j=j):
                act_s[:, j * TN:(j + 1) * TN] = v

    @pl.when(s == FF_TILES - 1)
    def _first_chunk_down():
        down(op_ref)

    @pl.when(s >= FF_TILES)
    def _chunk():
        for j in range(FF_TILES):
            act_s[:, j * TN:(j + 1) * TN] = up(wa_s[j], wb_s[j])
        if split_out:
            @pl.when(s < FF_FIRST_LATENT_STEP)
            def _down_prompt():
                down(op_ref)

            @pl.when(s >= FF_FIRST_LATENT_STEP)
            def _down_latent():
                down(os_ref)
        else:
            down(op_ref)


def _ffn(x, g_ffn, modc, w_gate_up, w_down, l, split_out):
    def chunk(s):
        return jnp.maximum(s - (FF_TILES - 1), 0)

    def tile(s):
        return jnp.minimum(s, FF_TILES - 1)

    def mod_spec(k):
        return pl.BlockSpec((None, None, 1, D_MODEL), lambda s: (l, chunk(s) * FF_ROWS // CHUNK, 0, k))

    if split_out:
        out_specs = [pl.BlockSpec((FF_ROWS, D_MODEL), lambda s: (jnp.minimum(chunk(s), FF_P_CHUNKS - 1), 0)),
                     pl.BlockSpec((FF_ROWS, D_MODEL), lambda s: (jnp.maximum(chunk(s) - FF_P_CHUNKS, 0), 0))]
        out_shape = [jax.ShapeDtypeStruct((P_ROWS, D_MODEL), F32), jax.ShapeDtypeStruct((S_ROWS, D_MODEL), F32)]
    else:
        out_specs = pl.BlockSpec((FF_ROWS, D_MODEL), lambda s: (chunk(s), 0))
        out_shape = jax.ShapeDtypeStruct((ROWS, D_MODEL), F32)
    return pl.pallas_call(
        functools.partial(_ffn_kernel, split_out=split_out),
        grid=(FF_STEPS,),
        in_specs=[pl.BlockSpec((FF_ROWS, D_MODEL), lambda s: (chunk(s), 0)),
                  pl.BlockSpec((None, 1, D_MODEL), lambda s: (l, 0, 0)),
                  mod_spec(3), mod_spec(4), mod_spec(5),
                  pl.BlockSpec((None, D_MODEL, TN), lambda s: (l, 0, tile(s))),
                  pl.BlockSpec((None, D_MODEL, TN), lambda s: (l, 0, FF_TILES + tile(s))),
                  pl.BlockSpec((None, TN, D_MODEL), lambda s: (l, tile(s), 0))],
        out_specs=out_specs,
        out_shape=out_shape,
        scratch_shapes=[pltpu.VMEM((FF_TILES, D_MODEL, TN), BF16),
                        pltpu.VMEM((FF_TILES, D_MODEL, TN), BF16),
                        pltpu.VMEM((D_FF, D_MODEL), BF16),
                        pltpu.VMEM((FF_ROWS, D_MODEL), BF16),
                        pltpu.VMEM((FF_ROWS, D_FF), BF16)],
        compiler_params=_params(),
        name="ffn",
    )(x, g_ffn.reshape(DEPTH, 1, D_MODEL), modc, modc, modc, w_gate_up, w_gate_up, w_down)


def _rope_tables():
    rows = DEC_SEQ // GRID_W
    row = jnp.repeat(jnp.arange(rows, dtype=F32), GRID_W)
    col = jnp.tile(jnp.arange(GRID_W, dtype=F32), rows)
    inv = 1.0 / (ROPE_THETA ** (jnp.arange(N_FREQ, dtype=F32) / N_FREQ))
    cr, sr = jnp.cos(row[:, None] * inv), jnp.sin(row[:, None] * inv)
    cc, sc = jnp.cos(col[:, None] * inv), jnp.sin(col[:, None] * inv)
    cos = jnp.concatenate([cr, cr, cc, cc], axis=1)
    sin = jnp.concatenate([-sr, sr, -sc, sc], axis=1)
    reps = LANES // HEAD_DIM
    return jnp.tile(cos, (1, reps)), jnp.tile(sin, (1, reps))


def kernel(x_prompt, x_sample, cache_k_attn, cache_v_attn, cache_k_win, cache_v_win, c, c_ctx,
           w_mod, b_mod, g_mix, g_ffn, w_in, gq_attn, gk_attn, gq_win, gk_win, sink_win,
           conv_w, pool_w, pool_scale, w_branch, w_out, w_gate_up, w_down):
    cond8 = jnp.zeros((8, D_MODEL), F32).at[0].set(c_ctx).at[1:1 + DEC_BATCH].set(c)
    mod = _modulation(cond8, w_mod, b_mod)
    chunk_cond = np.array([0] * N_P_CHUNKS + [1 + b for b in range(DEC_BATCH)])
    modc = mod[:, chunk_cond].reshape(DEPTH, N_CHUNKS, 1, 6 * D_MODEL)

    cos, sin = _rope_tables()
    gq2 = jnp.stack([jnp.tile(gq_attn, (1, TN // HEAD_DIM)), jnp.tile(gq_win, (1, TN // HEAD_DIM))],
                    axis=1).reshape(DEPTH, 2, 1, TN)
    gk2 = jnp.stack([jnp.tile(gk_attn, (1, KV_W // HEAD_DIM)), jnp.tile(gk_win, (1, KV_W // HEAD_DIM))],
                    axis=1).reshape(DEPTH, 2, 1, KV_W)
    caches = [a.transpose(0, 1, 3, 4, 2).reshape(DEC_BATCH, DEPTH, KV_W, PAST_LEN)
              for a in (cache_k_attn, cache_v_attn, cache_k_win, cache_v_win)]

    xp = x_prompt.reshape(P_ROWS, D_MODEL)
    xs = x_sample.reshape(S_ROWS, D_MODEL)
    xs_chunks, xs_mg = 0, 0
    new_kv = []
    for l in range(DEPTH):
        q, kv, conv, pool, new = _proj(xp, xs, xs_chunks, g_mix, modc, w_in, gq2, gk2, cos, sin,
                                       conv_w, pool_w, pool_scale, l)
        new_kv.append(new)
        att = _attention(sink_win, q, kv, *caches, l)
        x = _mixer_out(xp, xs, xs_mg, g_mix, modc, att, conv, pool, w_in, w_branch, w_out, l)
        if l + 1 < DEPTH:
            x = _ffn(x, g_ffn, modc, w_gate_up, w_down, l, False)
            xp = xs = x
            xs_chunks, xs_mg = N_P_CHUNKS, MG_P_CHUNKS
        else:
            y_prompt, y_sample = _ffn(x, g_ffn, modc, w_gate_up, w_down, l, True)

    outs = []
    for kind in range(4):
        a = jnp.stack([n[:, kind] for n in new_kv], axis=1)
        outs.append(a.reshape(BATCH, DEPTH, N_KV_HEADS, HEAD_DIM, SEQ).transpose(0, 1, 4, 2, 3))
    return (y_prompt.reshape(BATCH, SEQ, D_MODEL), y_sample.reshape(DEC_BATCH, DEC_SEQ, D_MODEL), *outs)
```

```python
import functools

import numpy as np
import jax
import jax.numpy as jnp
from jax import lax
from jax.experimental import pallas as pl
from jax.experimental.pallas import tpu as pltpu

D_MODEL = 1024
BATCH = 16
SEQ = 256
DEPTH = 4
DEC_BATCH = 2
DEC_SEQ = 1024
PAST_LEN = 512
GRID_W = 64
HEAD_DIM = 64
N_Q_HEADS = 8
N_KV_HEADS = 2
ATT_W = N_Q_HEADS * HEAD_DIM
KV_W = N_KV_HEADS * HEAD_DIM
N_FREQ = HEAD_DIM // 4
ROPE_THETA = 10000.0
CONV_W = 512
POOL_W = 512
POOL_SIZES = (2, 4, 8, 16)
N_BRANCH = 4
BRANCH_W = 512
D_FF = 2816
WINDOW = 128
EPS = 1e-6
NEG = -1e30

P_ROWS = BATCH * SEQ
S_ROWS = DEC_BATCH * DEC_SEQ
ROWS = P_ROWS + S_ROWS
CHUNK = 1024
N_CHUNKS = ROWS // CHUNK
N_P_CHUNKS = P_ROWS // CHUNK
SEQ_PER_CHUNK = CHUNK // SEQ
TN = 256
LANES = 128
VMEM_LIMIT = 58 * 1024 * 1024

COL_QA, COL_KVA, COL_QC, COL_KVC = 0, 2, 3, 5
COL_U, COL_GB, COL_GC, COL_PV, COL_GATE = 6, 8, 10, 12, 14

F32 = jnp.float32
BF16 = jnp.bfloat16


def _params():
    return pltpu.CompilerParams(dimension_semantics=("arbitrary",), vmem_limit_bytes=VMEM_LIMIT)


def _dot(a, b):
    return jnp.dot(a, b, preferred_element_type=F32)


def _dot_nt(a, b):
    return lax.dot_general(a, b, (((1,), (1,)), ((), ())), preferred_element_type=F32)


def _sigmoid(x):
    return 1.0 / (1.0 + jnp.exp(-x))


def _norm_modulate(x, g, scale, shift):
    y = x * lax.rsqrt(jnp.mean(x * x, axis=-1, keepdims=True) + EPS)
    return ((y * g) * (1.0 + scale) + shift).astype(BF16)


def _head_segments(width):
    r = lax.broadcasted_iota(jnp.int32, (width, width), 0) // HEAD_DIM
    c = lax.broadcasted_iota(jnp.int32, (width, width), 1) // HEAD_DIM
    return jnp.where(r == c, 1.0 / HEAD_DIM, 0.0).astype(BF16)


def _head_mean_square(y, seg):
    sq = y * y
    hi = sq.astype(BF16)
    lo = (sq - hi.astype(F32)).astype(BF16)
    return _dot(hi, seg) + _dot(lo, seg)


def _swap16(x):
    lane = lax.broadcasted_iota(jnp.int32, x.shape, 1)
    fwd = pltpu.roll(x, LANES - 16, axis=1)
    bwd = pltpu.roll(x, 16, axis=1)
    return jnp.where((lane & 16) == 0, fwd, bwd)


def _rope128(x, cos, sin):
    return x * cos + _swap16(x) * sin


N_MOD = 6 * D_MODEL
MOD_TN = 512
MOD_TILES = N_MOD // MOD_TN


def _mod_tile(cond_ref, w_ref, b_ref, o_ref):
    c = cond_ref[...]
    s = (c * _sigmoid(c)).astype(BF16)
    o_ref[...] = _dot(s, w_ref[...].astype(BF16)) + b_ref[...]


def _modulation(cond8, w_mod, b_mod, l):
    tn = 3 * MOD_TN
    return pl.pallas_call(
        _mod_tile,
        grid=(N_MOD // tn,),
        in_specs=[pl.BlockSpec((8, D_MODEL), lambda j: (0, 0)),
                  pl.BlockSpec((None, D_MODEL, tn), lambda j: (l, 0, j)),
                  pl.BlockSpec((None, 1, tn), lambda j: (l, 0, j))],
        out_specs=pl.BlockSpec((8, tn), lambda j: (0, j)),
        out_shape=jax.ShapeDtypeStruct((8, N_MOD), F32),
        compiler_params=_params(),
        name="modulation",
    )(cond8, w_mod, b_mod.reshape(DEPTH, 1, N_MOD))


PJ_TILES = COL_GATE
PJ_STEPS = PJ_TILES + N_CHUNKS


def _seq_pos(latent, width):
    row = lax.broadcasted_iota(jnp.int32, (CHUNK, width), 0)
    return row if latent else row & (SEQ - 1)


def _shift_rows(x, k, pos, seq_len):
    rolled = pltpu.roll(x, (-k) % CHUNK, axis=0)
    ok = (pos + k >= 0) & (pos + k < seq_len)
    return jnp.where(ok, rolled, 0.0)


def _window_mean_minus_token(p, half, pos, seq_len):
    fwd = p
    bwd = _shift_rows(p, -1, pos, seq_len)
    m = 1
    while m < half:
        fwd = fwd + _shift_rows(fwd, m, pos, seq_len)
        bwd = bwd + _shift_rows(bwd, -m, pos, seq_len)
        m *= 2
    count = jnp.minimum(pos + half, seq_len) - jnp.maximum(pos - half, 0)
    return (fwd + bwd) / count.astype(F32) - p


def _proj_chunk(latent, x_ref, g_ref, sh_ref, sc_ref, gq_ref, gk_ref, cos_ref, sin_ref, cw_ref, pw_ref,
                ps_ref, q_ref, kv_ref, conv_ref, pool_ref, new_ref, w_s, h_s):
    h_s[...] = _norm_modulate(x_ref[...], g_ref[...], sc_ref[...], sh_ref[...])
    seq_len = DEC_SEQ if latent else SEQ
    seg_q = _head_segments(TN)
    seg_k = _head_segments(KV_W)
    pos_wide = _seq_pos(latent, TN)
    pos = _seq_pos(latent, LANES)

    def q_tile(mixer, j, y):
        y = y * lax.rsqrt(_head_mean_square(y, seg_q) + EPS) * (gq_ref[mixer] * (HEAD_DIM ** -0.5))
        for s in range(TN // LANES):
            part = y[:, s * LANES:(s + 1) * LANES]
            if latent:
                part = _rope128(part, cos_ref[...], sin_ref[...])
            out_col = mixer * ATT_W + j * TN + s * LANES
            q_ref[:, out_col:out_col + LANES] = part.astype(BF16)

    def kv_tile(mixer, y):
        k = y[:, :KV_W]
        v = y[:, KV_W:]
        k = k * lax.rsqrt(_head_mean_square(k, seg_k) + EPS) * gk_ref[mixer]
        if latent:
            k = _rope128(k, cos_ref[...], sin_ref[...])
        else:
            for b in range(SEQ_PER_CHUNK):
                new_ref[b, 2 * mixer] = k[b * SEQ:(b + 1) * SEQ, :].T
                new_ref[b, 2 * mixer + 1] = v[b * SEQ:(b + 1) * SEQ, :].T
        kv_ref[:, 2 * mixer * KV_W:(2 * mixer + 1) * KV_W] = k.astype(BF16)
        kv_ref[:, (2 * mixer + 1) * KV_W:(2 * mixer + 2) * KV_W] = v.astype(BF16)

    def conv_tile(j, gc, u, gb):
        cols = slice(j * TN, (j + 1) * TN)
        z = gc * u
        y = (_shift_rows(z, -1, pos_wide, seq_len) * cw_ref[0:1, cols] + z * cw_ref[1:2, cols]
             + _shift_rows(z, 1, pos_wide, seq_len) * cw_ref[2:3, cols])
        conv_ref[:, cols] = (gb * y).astype(BF16)

    def pool_tile(j, p):
        for s in range(TN // LANES):
            grp = j * (TN // LANES) + s
            cols = slice(grp * LANES, (grp + 1) * LANES)
            y = _window_mean_minus_token(p[:, s * LANES:(s + 1) * LANES], POOL_SIZES[grp] // 2, pos, seq_len)
            pool_ref[:, cols] = (_dot(y.astype(BF16), pw_ref[grp].astype(BF16)) * ps_ref[:, cols]).astype(BF16)

    tasks = []
    for j in range(POOL_W // TN):
        tasks.append(((COL_PV + j,), functools.partial(pool_tile, j)))
        tasks.append(((COL_GC + j, COL_U + j, COL_GB + j), functools.partial(conv_tile, j)))
    for mixer, col in enumerate((COL_QA, COL_QC)):
        for j in range(ATT_W // TN):
            tasks.append(((col + j,), functools.partial(q_tile, mixer, j)))
    for mixer, col in enumerate((COL_KVA, COL_KVC)):
        tasks.append(((col,), functools.partial(kv_tile, mixer)))

    def matmuls(tiles):
        return [_dot(h_s[...], w_s[t]) for t in tiles]

    ys = matmuls(tasks[0][0])
    for i, (_, epilogue) in enumerate(tasks):
        nxt = matmuls(tasks[i + 1][0]) if i + 1 < len(tasks) else None
        epilogue(*ys)
        ys = nxt


def _proj_kernel(xp_ref, xs_ref, g_ref, sh_ref, sc_ref, w_ref, gq_ref, gk_ref, cos_ref, sin_ref,
                 cw_ref, pw_ref, ps_ref, q_ref, kv_ref, conv_ref, pool_ref, new_ref, w_s, h_s):
    s = pl.program_id(0)
    rest = (g_ref, sh_ref, sc_ref, gq_ref, gk_ref, cos_ref, sin_ref, cw_ref, pw_ref, ps_ref,
            q_ref, kv_ref, conv_ref, pool_ref, new_ref, w_s, h_s)

    @pl.when(s < PJ_TILES)
    def _stream():
        w_s[s] = w_ref[...].astype(BF16)

    @pl.when((s >= PJ_TILES) & (s < PJ_TILES + N_P_CHUNKS))
    def _prompt():
        _proj_chunk(False, xp_ref, *rest)

    @pl.when(s >= PJ_TILES + N_P_CHUNKS)
    def _latent():
        _proj_chunk(True, xs_ref, *rest)


def _proj(xp, xs, xs_base, g_mix, modc, w_in, gq2, gk2, cos, sin, conv_w, pool_w, pool_scale, l):
    def chunk(s):
        return jnp.clip(s - PJ_TILES, 0, N_CHUNKS - 1)

    def p_chunk(s):
        return jnp.clip(s - PJ_TILES, 0, N_P_CHUNKS - 1)

    def s_chunk(s):
        return xs_base + jnp.clip(s - PJ_TILES - N_P_CHUNKS, 0, N_CHUNKS - N_P_CHUNKS - 1)

    def mod_spec(k):
        return pl.BlockSpec((None, 1, D_MODEL), lambda s: (chunk(s), 0, k))

    def const(shape):
        return pl.BlockSpec(shape, lambda s: (l,) + (0,) * (len(shape) - 1))

    def rows_out(width):
        return pl.BlockSpec((CHUNK, width), lambda s: (chunk(s), 0))

    return pl.pallas_call(
        _proj_kernel,
        grid=(PJ_STEPS,),
        in_specs=[pl.BlockSpec((CHUNK, D_MODEL), lambda s: (p_chunk(s), 0)),
                  pl.BlockSpec((CHUNK, D_MODEL), lambda s: (s_chunk(s), 0)),
                  const((None, 1, D_MODEL)), mod_spec(0), mod_spec(1),
                  pl.BlockSpec((None, D_MODEL, TN), lambda s: (l, 0, jnp.minimum(s, PJ_TILES - 1))),
                  const((None, 2, 1, TN)), const((None, 2, 1, KV_W)),
                  pl.BlockSpec((DEC_SEQ, LANES), lambda s: (0, 0)),
                  pl.BlockSpec((DEC_SEQ, LANES), lambda s: (0, 0)),
                  const((None, 3, CONV_W)), const((None, len(POOL_SIZES), LANES, LANES)),
                  const((None, 1, POOL_W))],
        out_specs=[rows_out(2 * ATT_W), rows_out(4 * KV_W), rows_out(CONV_W), rows_out(POOL_W),
                   pl.BlockSpec((SEQ_PER_CHUNK, 4, KV_W, SEQ), lambda s: (p_chunk(s), 0, 0, 0))],
        out_shape=[jax.ShapeDtypeStruct((ROWS, 2 * ATT_W), BF16),
                   jax.ShapeDtypeStruct((ROWS, 4 * KV_W), BF16),
                   jax.ShapeDtypeStruct((ROWS, CONV_W), BF16),
                   jax.ShapeDtypeStruct((ROWS, POOL_W), BF16),
                   jax.ShapeDtypeStruct((BATCH, 4, KV_W, SEQ), F32)],
        scratch_shapes=[pltpu.VMEM((PJ_TILES, D_MODEL, TN), BF16),
                        pltpu.VMEM((CHUNK, D_MODEL), BF16)],
        compiler_params=_params(),
        name="proj",
    )(xp, xs, g_mix.reshape(DEPTH, 1, D_MODEL), modc, modc, w_in, gq2, gk2, cos, sin,
      conv_w, pool_w, pool_scale.reshape(DEPTH, 1, POOL_W))


TQ = SEQ
N_QT = DEC_SEQ // TQ
WIN_KEYS = TQ + 2 * WINDOW
PAD_SEQ = WINDOW + DEC_SEQ + WINDOW
ATT_STEPS = BATCH + DEC_BATCH * N_QT


def _split_lanes(x):
    xr = pltpu.roll(x, HEAD_DIM, axis=1)
    low = lax.broadcasted_iota(jnp.int32, x.shape, 1) < HEAD_DIM
    return (jnp.where(low, x, 0.0).astype(BF16), jnp.where(low, 0.0, xr).astype(BF16),
            jnp.where(low, xr, 0.0).astype(BF16), jnp.where(low, 0.0, x).astype(BF16))


def _split_rows(xt):
    z = jnp.zeros((HEAD_DIM, xt.shape[1]), F32)
    h0, h1 = xt[:HEAD_DIM], xt[HEAD_DIM:]
    return tuple(jnp.concatenate(p, axis=0).astype(BF16) for p in ((h0, z), (z, h0), (h1, z), (z, h1)))


def _attend_pairs(tasks):
    units = [(t, parity) for t in range(len(tasks)) for parity in range(2)]

    def scores_of(unit):
        t, parity = unit
        q, pieces, _, _ = tasks[t]
        out = []
        for k_lo, k_hi, k_t, _, _, _, mask in pieces:
            k = k_hi if parity else k_lo
            s = _dot(q, k) if k_t else _dot_nt(q, k)
            if mask is not None:
                s = jnp.where(mask, s, NEG)
            out.append(s)
        return out

    scores = scores_of(units[0])
    even = None
    for i, (t, parity) in enumerate(units):
        nxt = scores_of(units[i + 1]) if i + 1 < len(units) else None
        _, pieces, sinks, write = tasks[t]
        m = functools.reduce(jnp.maximum, [jnp.max(s, axis=1, keepdims=True) for s in scores])
        if sinks is not None:
            m = jnp.maximum(m, sinks[parity])
        denom = None
        acc = None
        for s, (_, _, _, v_lo, v_hi, v_t, _) in zip(scores, pieces):
            p = jnp.exp(s - m)
            d = jnp.sum(p, axis=1, keepdims=True)
            v = v_hi if parity else v_lo
            a = _dot_nt(p.astype(BF16), v) if v_t else _dot(p.astype(BF16), v)
            denom = d if denom is None else denom + d
            acc = a if acc is None else acc + a
        if sinks is not None:
            denom = denom + jnp.exp(sinks[parity] - m)
        acc = acc / denom
        if parity == 0:
            even = acc
        else:
            write(even + acc)
        scores = nxt


def _attn_prompt_step(sink_ref, q_ref, kv_ref, o_ref, layer):
    def writer(col):
        def write(out):
            o_ref[:, col:col + LANES] = out.astype(BF16)
        return write

    tasks = []
    for mixer in range(2):
        k = _split_lanes(kv_ref[:, 2 * mixer * KV_W:(2 * mixer + 1) * KV_W].astype(F32))
        v = _split_lanes(kv_ref[:, (2 * mixer + 1) * KV_W:(2 * mixer + 2) * KV_W].astype(F32))
        for pair in range(N_Q_HEADS // 2):
            lo = 2 * (pair // 2)
            piece = (k[lo], k[lo + 1], False, v[lo], v[lo + 1], False, None)
            sinks = None
            if mixer == 1:
                sinks = (sink_ref[layer, 2 * pair], sink_ref[layer, 2 * pair + 1])
            col = mixer * ATT_W + pair * LANES
            tasks.append((q_ref[:, col:col + LANES], [piece], sinks, writer(col)))
    _attend_pairs(tasks)


def _attn_latent_step(qt, sink_ref, q_ref, kv_ref, cka_ref, cva_ref, ckc_ref, cvc_ref, o_ref,
                      ctx_s, ka_s, va_s, kc_s, vc_s, layer):
    @pl.when(qt == 0)
    def _fill():
        for i, ref in enumerate((cka_ref, cva_ref, ckc_ref, cvc_ref)):
            for j, part in enumerate(_split_rows(ref[...])):
                ctx_s[4 * i + j] = part
        for dst, col in ((ka_s, 0), (va_s, KV_W)):
            for j, part in enumerate(_split_lanes(kv_ref[:, col:col + KV_W].astype(F32))):
                dst[j] = part
        zeros = jnp.zeros((WINDOW, LANES), BF16)
        for dst, col in ((kc_s, 2 * KV_W), (vc_s, 3 * KV_W)):
            for j, part in enumerate(_split_lanes(kv_ref[:, col:col + KV_W].astype(F32))):
                dst[j, 0:WINDOW, :] = zeros
                dst[j, WINDOW:WINDOW + DEC_SEQ, :] = part
                dst[j, WINDOW + DEC_SEQ:PAD_SEQ, :] = zeros

    q0 = qt * TQ
    win = pl.ds(pl.multiple_of(q0, TQ), WIN_KEYS)
    r = lax.broadcasted_iota(jnp.int32, (TQ, WIN_KEYS), 0)
    jk = lax.broadcasted_iota(jnp.int32, (TQ, WIN_KEYS), 1)
    kpos = q0 - WINDOW + jk
    band = jnp.where((jk - r >= 0) & (jk - r <= 2 * WINDOW), kpos, -1)
    mask = (band >= 0) & (band < DEC_SEQ)

    def writer(col):
        def write(out):
            o_ref[:, col:col + LANES] = out.astype(BF16)
        return write

    tasks = []
    for pair in range(N_Q_HEADS // 2):
        lo = 2 * (pair // 2)
        hi = lo + 1
        col = pair * LANES
        ctx = (ctx_s[lo], ctx_s[hi], True, ctx_s[4 + lo], ctx_s[4 + hi], True, None)
        cur = (ka_s[lo], ka_s[hi], False, va_s[lo], va_s[hi], False, None)
        tasks.append((q_ref[:, col:col + LANES], [ctx, cur], None, writer(col)))

        col = ATT_W + pair * LANES
        ctx = (ctx_s[8 + lo], ctx_s[8 + hi], True, ctx_s[12 + lo], ctx_s[12 + hi], True, None)
        near = (kc_s[lo, win, :], kc_s[hi, win, :], False, vc_s[lo, win, :], vc_s[hi, win, :], False, mask)
        sinks = (sink_ref[layer, 2 * pair], sink_ref[layer, 2 * pair + 1])
        tasks.append((q_ref[:, col:col + LANES], [ctx, near], sinks, writer(col)))
    _attend_pairs(tasks)


def _attn_kernel(sink_ref, q_ref, kvp_ref, kvl_ref, cka_ref, cva_ref, ckc_ref, cvc_ref, *rest,
                 layer, n_new):
    new_refs = rest[:n_new]
    o_ref = rest[n_new]
    kind_refs = rest[n_new + 1:n_new + 1 + (4 if n_new else 0)]
    ctx_s, ka_s, va_s, kc_s, vc_s = rest[-5:]
    s = pl.program_id(0)

    @pl.when(s < BATCH)
    def _prompt():
        _attn_prompt_step(sink_ref, q_ref, kvp_ref, o_ref, layer)
        for kind, dst in enumerate(kind_refs):
            for depth, src in enumerate(new_refs):
                dst[0, depth] = src[0, kind]

    @pl.when(s >= BATCH)
    def _latent():
        _attn_latent_step((s - BATCH) % N_QT, sink_ref, q_ref, kvl_ref, cka_ref, cva_ref, ckc_ref, cvc_ref,
                          o_ref, ctx_s, ka_s, va_s, kc_s, vc_s, layer)


def _attention(sink, q, kv, cka, cva, ckc, cvc, l, new_kv=()):
    def latent_batch(s):
        return jnp.maximum(s - BATCH, 0) // N_QT

    def prompt_seq(s):
        return jnp.minimum(s, BATCH - 1)

    n_new = len(new_kv)
    cache_spec = pl.BlockSpec((None, None, KV_W, PAST_LEN), lambda s: (latent_batch(s), l, 0, 0))
    out_specs = [pl.BlockSpec((TQ, 2 * ATT_W), lambda s: (s, 0))]
    out_shape = [jax.ShapeDtypeStruct((ROWS, 2 * ATT_W), BF16)]
    if n_new:
        out_specs += [pl.BlockSpec((1, n_new, KV_W, SEQ), lambda s: (prompt_seq(s), 0, 0, 0))] * 4
        out_shape += [jax.ShapeDtypeStruct((BATCH, n_new, KV_W, SEQ), F32)] * 4
    outs = pl.pallas_call(
        functools.partial(_attn_kernel, layer=l, n_new=n_new),
        grid=(ATT_STEPS,),
        in_specs=[pl.BlockSpec(memory_space=pltpu.SMEM),
                  pl.BlockSpec((TQ, 2 * ATT_W), lambda s: (s, 0)),
                  pl.BlockSpec((SEQ, 4 * KV_W), lambda s: (prompt_seq(s), 0)),
                  pl.BlockSpec((DEC_SEQ, 4 * KV_W), lambda s: (N_P_CHUNKS + latent_batch(s), 0)),
                  cache_spec, cache_spec, cache_spec, cache_spec]
                 + [pl.BlockSpec((1, 4, KV_W, SEQ), lambda s: (prompt_seq(s), 0, 0, 0))] * n_new,
        out_specs=out_specs,
        out_shape=out_shape,
        scratch_shapes=[pltpu.VMEM((16, KV_W, PAST_LEN), BF16),
                        pltpu.VMEM((4, DEC_SEQ, LANES), BF16), pltpu.VMEM((4, DEC_SEQ, LANES), BF16),
                        pltpu.VMEM((4, PAD_SEQ, LANES), BF16), pltpu.VMEM((4, PAD_SEQ, LANES), BF16)],
        compiler_params=_params(),
        name="attention",
    )(sink, q, kv, kv, cka, cva, ckc, cvc, *new_kv)
    return outs if n_new else outs[0]


MG_ROWS = 512
MG_CHUNKS = ROWS // MG_ROWS
MG_P_CHUNKS = P_ROWS // MG_ROWS
MG_TILES = D_MODEL // TN
MG_STEPS = MG_TILES + MG_CHUNKS - 1
MG_FIRST_LATENT_STEP = MG_TILES - 1 + MG_P_CHUNKS


def _mixer_out_kernel(xp_ref, xs_ref, g_ref, sh_ref, sc_ref, gate_ref, att_ref, conv_ref, pool_ref,
                      wg0, wg1, wg2, wg3, wb0, wb1, wb2, wb3, wo_ref, o_ref,
                      wg_s, wb_s, wo_s, h_s, mix_s):
    s = pl.program_id(0)
    prompt = s < MG_FIRST_LATENT_STEP

    def norm(x_ref):
        h_s[...] = _norm_modulate(x_ref[...], g_ref[...], sc_ref[...], sh_ref[...])

    @pl.when(((s == 0) | (s >= MG_TILES)) & prompt)
    def _norm_prompt():
        norm(xp_ref)

    @pl.when(jnp.logical_not(prompt))
    def _norm_latent():
        norm(xs_ref)

    def mix_tile(wgs, wbs):
        h = h_s[...]
        branches = (att_ref[:, :ATT_W], conv_ref[...], att_ref[:, ATT_W:], pool_ref[...])
        acc = None
        for br, wg, wb in zip(branches, wgs, wbs):
            term = _sigmoid(_dot(h, wg)) * _dot(br, wb)
            acc = term if acc is None else acc + term
        return acc.astype(BF16)

    def out(x_ref):
        o_ref[...] = x_ref[...] + gate_ref[...] * _dot(mix_s[...], wo_s[...])

    @pl.when(s < MG_TILES)
    def _stream():
        wgs = [r[...].astype(BF16) for r in (wg0, wg1, wg2, wg3)]
        wbs = [r[...].astype(BF16) for r in (wb0, wb1, wb2, wb3)]
        for k in range(N_BRANCH):
            wg_s[s * N_BRANCH + k] = wgs[k]
            wb_s[s * N_BRANCH + k] = wbs[k]
        wo_s[pl.ds(pl.multiple_of(s * TN, TN), TN), :] = wo_ref[...].astype(BF16)
        v = mix_tile(wgs, wbs)
        for n in range(MG_TILES):
            @pl.when(s == n)
            def _store(n=n):
                mix_s[:, n * TN:(n + 1) * TN] = v

    @pl.when(s == MG_TILES - 1)
    def _first_chunk_out():
        out(xp_ref)

    @pl.when(s >= MG_TILES)
    def _chunk():
        for n in range(MG_TILES):
            mix_s[:, n * TN:(n + 1) * TN] = mix_tile(
                [wg_s[n * N_BRANCH + k] for k in range(N_BRANCH)],
                [wb_s[n * N_BRANCH + k] for k in range(N_BRANCH)])

        @pl.when(prompt)
        def _out_prompt():
            out(xp_ref)

        @pl.when(jnp.logical_not(prompt))
        def _out_latent():
            out(xs_ref)


def _mixer_out(xp, xs, xs_base, g_mix, modc, att, conv, pool, w_in, w_branch, w_out, l):
    def chunk(s):
        return jnp.maximum(s - (MG_TILES - 1), 0)

    def tile(s):
        return jnp.minimum(s, MG_TILES - 1)

    def mod_spec(k):
        return pl.BlockSpec((None, 1, D_MODEL), lambda s: (chunk(s) * MG_ROWS // CHUNK, 0, k))

    def rows_in(width):
        return pl.BlockSpec((MG_ROWS, width), lambda s: (chunk(s), 0))

    gate_specs = [pl.BlockSpec((None, D_MODEL, TN), functools.partial(
        lambda s, k: (l, 0, COL_GATE + MG_TILES * k + tile(s)), k=k)) for k in range(N_BRANCH)]
    br_specs = [pl.BlockSpec((None, None, BRANCH_W, TN), functools.partial(
        lambda s, k: (l, k, 0, tile(s)), k=k)) for k in range(N_BRANCH)]
    return pl.pallas_call(
        _mixer_out_kernel,
        grid=(MG_STEPS,),
        in_specs=[pl.BlockSpec((MG_ROWS, D_MODEL), lambda s: (jnp.minimum(chunk(s), MG_P_CHUNKS - 1), 0)),
                  pl.BlockSpec((MG_ROWS, D_MODEL),
                               lambda s: (xs_base + jnp.maximum(chunk(s) - MG_P_CHUNKS, 0), 0)),
                  pl.BlockSpec((None, 1, D_MODEL), lambda s: (l, 0, 0)),
                  mod_spec(0), mod_spec(1), mod_spec(2),
                  rows_in(2 * ATT_W), rows_in(CONV_W), rows_in(POOL_W)] + gate_specs + br_specs
                 + [pl.BlockSpec((None, TN, D_MODEL), lambda s: (l, tile(s), 0))],
        out_specs=pl.BlockSpec((MG_ROWS, D_MODEL), lambda s: (chunk(s), 0)),
        out_shape=jax.ShapeDtypeStruct((ROWS, D_MODEL), F32),
        scratch_shapes=[pltpu.VMEM((MG_TILES * N_BRANCH, D_MODEL, TN), BF16),
                        pltpu.VMEM((MG_TILES * N_BRANCH, BRANCH_W, TN), BF16),
                        pltpu.VMEM((D_MODEL, D_MODEL), BF16),
                        pltpu.VMEM((MG_ROWS, D_MODEL), BF16),
                        pltpu.VMEM((MG_ROWS, D_MODEL), BF16)],
        compiler_params=_params(),
        name="mixer_out",
    )(xp, xs, g_mix.reshape(DEPTH, 1, D_MODEL), modc, modc, modc, att, conv, pool,
      *([w_in] * N_BRANCH), *([w_branch] * N_BRANCH), w_out)


FF_ROWS = 512
FF_CHUNKS = ROWS // FF_ROWS
FF_P_CHUNKS = P_ROWS // FF_ROWS
FF_TILES = D_FF // TN
FF_STEPS = FF_TILES + FF_CHUNKS - 1
FF_FIRST_LATENT_STEP = FF_TILES - 1 + FF_P_CHUNKS
FF_MOD_FIRST_STEP = FF_STEPS - MOD_TILES


def _ffn_kernel(x_ref, g_ref, sh_ref, sc_ref, gate_ref, wa_ref, wb_ref, wd_ref, *rest, last):
    if last:
        op_ref, os_ref, wa_s, wb_s, wd_s, h_s, act_s = rest
    else:
        cond_ref, wm_ref, bm_ref, op_ref, mod_ref, wa_s, wb_s, wd_s, h_s, act_s = rest
        os_ref = op_ref
    s = pl.program_id(0)

    @pl.when((s == 0) | (s >= FF_TILES))
    def _norm():
        h_s[...] = _norm_modulate(x_ref[...], g_ref[...], sc_ref[...], sh_ref[...])

    def up(wa, wb):
        h = h_s[...]
        a = _dot(h, wa)
        return (a * _sigmoid(a) * _dot(h, wb)).astype(BF16)

    def down(o_ref):
        o_ref[...] = x_ref[...] + gate_ref[...] * _dot(act_s[...], wd_s[...])

    @pl.when(s < FF_TILES)
    def _stream():
        wa = wa_ref[...].astype(BF16)
        wb = wb_ref[...].astype(BF16)
        wa_s[s] = wa
        wb_s[s] = wb
        wd_s[pl.ds(pl.multiple_of(s * TN, TN), TN), :] = wd_ref[...].astype(BF16)
        v = up(wa, wb)
        for j in range(FF_TILES):
            @pl.when(s == j)
            def _store(j=j):
                act_s[:, j * TN:(j + 1) * TN] = v

    @pl.when(s == FF_TILES - 1)
    def _first_chunk_down():
        down(op_ref)

    @pl.when(s >= FF_TILES)
    def _chunk():
        for j in range(FF_TILES):
            act_s[:, j * TN:(j + 1) * TN] = up(wa_s[j], wb_s[j])
        if last:
            @pl.when(s < FF_FIRST_LATENT_STEP)
            def _down_prompt():
                down(op_ref)

            @pl.when(s >= FF_FIRST_LATENT_STEP)
            def _down_latent():
                down(os_ref)
        else:
            down(op_ref)

    if not last:
        @pl.when(s >= FF_MOD_FIRST_STEP)
        def _next_modulation():
            _mod_tile(cond_ref, wm_ref, bm_ref, mod_ref)


def _ffn(x, g_ffn, modc, w_gate_up, w_down, l, cond8=None, w_mod=None, b_mod=None):
    last = cond8 is None

    def chunk(s):
        return jnp.maximum(s - (FF_TILES - 1), 0)

    def tile(s):
        return jnp.minimum(s, FF_TILES - 1)

    def mod_tile(s):
        return jnp.maximum(s - FF_MOD_FIRST_STEP, 0)

    def mod_spec(k):
        return pl.BlockSpec((None, 1, D_MODEL), lambda s: (chunk(s) * FF_ROWS // CHUNK, 0, k))

    in_specs = [pl.BlockSpec((FF_ROWS, D_MODEL), lambda s: (chunk(s), 0)),
                pl.BlockSpec((None, 1, D_MODEL), lambda s: (l, 0, 0)),
                mod_spec(3), mod_spec(4), mod_spec(5),
                pl.BlockSpec((None, D_MODEL, TN), lambda s: (l, 0, tile(s))),
                pl.BlockSpec((None, D_MODEL, TN), lambda s: (l, 0, FF_TILES + tile(s))),
                pl.BlockSpec((None, TN, D_MODEL), lambda s: (l, tile(s), 0))]
    args = [x, g_ffn.reshape(DEPTH, 1, D_MODEL), modc, modc, modc, w_gate_up, w_gate_up, w_down]
    if last:
        out_specs = [pl.BlockSpec((FF_ROWS, D_MODEL), lambda s: (jnp.minimum(chunk(s), FF_P_CHUNKS - 1), 0)),
                     pl.BlockSpec((FF_ROWS, D_MODEL), lambda s: (jnp.maximum(chunk(s) - FF_P_CHUNKS, 0), 0))]
        out_shape = [jax.ShapeDtypeStruct((P_ROWS, D_MODEL), F32), jax.ShapeDtypeStruct((S_ROWS, D_MODEL), F32)]
    else:
        in_specs += [pl.BlockSpec((8, D_MODEL), lambda s: (0, 0)),
                     pl.BlockSpec((None, D_MODEL, MOD_TN), lambda s: (l + 1, 0, mod_tile(s))),
                     pl.BlockSpec((None, 1, MOD_TN), lambda s: (l + 1, 0, mod_tile(s)))]
        args += [cond8, w_mod, b_mod.reshape(DEPTH, 1, N_MOD)]
        out_specs = [pl.BlockSpec((FF_ROWS, D_MODEL), lambda s: (chunk(s), 0)),
                     pl.BlockSpec((8, MOD_TN), lambda s: (0, mod_tile(s)))]
        out_shape = [jax.ShapeDtypeStruct((ROWS, D_MODEL), F32), jax.ShapeDtypeStruct((8, N_MOD), F32)]
    return pl.pallas_call(
        functools.partial(_ffn_kernel, last=last),
        grid=(FF_STEPS,),
        in_specs=in_specs,
        out_specs=out_specs,
        out_shape=out_shape,
        scratch_shapes=[pltpu.VMEM((FF_TILES, D_MODEL, TN), BF16),
                        pltpu.VMEM((FF_TILES, D_MODEL, TN), BF16),
                        pltpu.VMEM((D_FF, D_MODEL), BF16),
                        pltpu.VMEM((FF_ROWS, D_MODEL), BF16),
                        pltpu.VMEM((FF_ROWS, D_FF), BF16)],
        compiler_params=_params(),
        name="ffn",
    )(*args)


def _rope_tables():
    rows = DEC_SEQ // GRID_W
    row = jnp.repeat(jnp.arange(rows, dtype=F32), GRID_W)
    col = jnp.tile(jnp.arange(GRID_W, dtype=F32), rows)
    inv = 1.0 / (ROPE_THETA ** (jnp.arange(N_FREQ, dtype=F32) / N_FREQ))
    cr, sr = jnp.cos(row[:, None] * inv), jnp.sin(row[:, None] * inv)
    cc, sc = jnp.cos(col[:, None] * inv), jnp.sin(col[:, None] * inv)
    cos = jnp.concatenate([cr, cr, cc, cc], axis=1)
    sin = jnp.concatenate([-sr, sr, -sc, sc], axis=1)
    reps = LANES // HEAD_DIM
    return jnp.tile(cos, (1, reps)), jnp.tile(sin, (1, reps))


def kernel(x_prompt, x_sample, cache_k_attn, cache_v_attn, cache_k_win, cache_v_win, c, c_ctx,
           w_mod, b_mod, g_mix, g_ffn, w_in, gq_attn, gk_attn, gq_win, gk_win, sink_win,
           conv_w, pool_w, pool_scale, w_branch, w_out, w_gate_up, w_down):
    cond8 = jnp.zeros((8, D_MODEL), F32).at[0].set(c_ctx).at[1:1 + DEC_BATCH].set(c)
    chunk_cond = np.array([0] * N_P_CHUNKS + [1 + b for b in range(DEC_BATCH)])

    def per_chunk(mod):
        return mod[chunk_cond].reshape(N_CHUNKS, 1, N_MOD)

    modc = per_chunk(_modulation(cond8, w_mod, b_mod, 0))

    cos, sin = _rope_tables()
    gq2 = jnp.stack([jnp.tile(gq_attn, (1, TN // HEAD_DIM)), jnp.tile(gq_win, (1, TN // HEAD_DIM))],
                    axis=1).reshape(DEPTH, 2, 1, TN)
    gk2 = jnp.stack([jnp.tile(gk_attn, (1, KV_W // HEAD_DIM)), jnp.tile(gk_win, (1, KV_W // HEAD_DIM))],
                    axis=1).reshape(DEPTH, 2, 1, KV_W)
    caches = [a.transpose(0, 1, 3, 4, 2).reshape(DEC_BATCH, DEPTH, KV_W, PAST_LEN)
              for a in (cache_k_attn, cache_v_attn, cache_k_win, cache_v_win)]

    xp = x_prompt.reshape(P_ROWS, D_MODEL)
    xs = x_sample.reshape(S_ROWS, D_MODEL)
    xs_chunks, xs_mg = 0, 0
    new_kv = []
    for l in range(DEPTH):
        q, kv, conv, pool, new = _proj(xp, xs, xs_chunks, g_mix, modc, w_in, gq2, gk2, cos, sin,
                                       conv_w, pool_w, pool_scale, l)
        new_kv.append(new)
        if l + 1 < DEPTH:
            att = _attention(sink_win, q, kv, *caches, l)
        else:
            att, *by_kind = _attention(sink_win, q, kv, *caches, l, new_kv)
        x = _mixer_out(xp, xs, xs_mg, g_mix, modc, att, conv, pool, w_in, w_branch, w_out, l)
        if l + 1 < DEPTH:
            x, mod_next = _ffn(x, g_ffn, modc, w_gate_up, w_down, l, cond8, w_mod, b_mod)
            modc = per_chunk(mod_next)
            xp = xs = x
            xs_chunks, xs_mg = N_P_CHUNKS, MG_P_CHUNKS
        else:
            y_prompt, y_sample = _ffn(x, g_ffn, modc, w_gate_up, w_down, l)

    outs = [a.reshape(BATCH, DEPTH, N_KV_HEADS, HEAD_DIM, SEQ).transpose(0, 1, 4, 2, 3) for a in by_kind]
    return (y_prompt.reshape(BATCH, SEQ, D_MODEL), y_sample.reshape(DEC_BATCH, DEC_SEQ, D_MODEL), *outs)
```

```python
import functools

import numpy as np
import jax
import jax.numpy as jnp
from jax import lax
from jax.experimental import pallas as pl
from jax.experimental.pallas import tpu as pltpu

D_MODEL = 1024
BATCH = 16
SEQ = 256
DEPTH = 4
DEC_BATCH = 2
DEC_SEQ = 1024
PAST_LEN = 512
GRID_W = 64
HEAD_DIM = 64
N_Q_HEADS = 8
N_KV_HEADS = 2
ATT_W = N_Q_HEADS * HEAD_DIM
KV_W = N_KV_HEADS * HEAD_DIM
N_FREQ = HEAD_DIM // 4
ROPE_THETA = 10000.0
CONV_W = 512
POOL_W = 512
POOL_SIZES = (2, 4, 8, 16)
N_BRANCH = 4
BRANCH_W = 512
D_FF = 2816
WINDOW = 128
EPS = 1e-6
NEG = -1e30

P_ROWS = BATCH * SEQ
S_ROWS = DEC_BATCH * DEC_SEQ
ROWS = P_ROWS + S_ROWS
CHUNK = 1024
N_CHUNKS = ROWS // CHUNK
N_P_CHUNKS = P_ROWS // CHUNK
SEQ_PER_CHUNK = CHUNK // SEQ
TN = 256
LANES = 128
VMEM_LIMIT = 58 * 1024 * 1024

COL_QA, COL_KVA, COL_QC, COL_KVC = 0, 2, 3, 5
COL_U, COL_GB, COL_GC, COL_PV, COL_GATE = 6, 8, 10, 12, 14

F32 = jnp.float32
BF16 = jnp.bfloat16


def _params():
    return pltpu.CompilerParams(dimension_semantics=("arbitrary",), vmem_limit_bytes=VMEM_LIMIT)


def _dot(a, b):
    return jnp.dot(a, b, preferred_element_type=F32)


def _dot_nt(a, b):
    return lax.dot_general(a, b, (((1,), (1,)), ((), ())), preferred_element_type=F32)


def _sigmoid(x):
    return 1.0 / (1.0 + jnp.exp(-x))


def _norm_modulate(x, g, scale, shift):
    y = x * lax.rsqrt(jnp.mean(x * x, axis=-1, keepdims=True) + EPS)
    return ((y * g) * (1.0 + scale) + shift).astype(BF16)


def _head_segments(width):
    r = lax.broadcasted_iota(jnp.int32, (width, width), 0) // HEAD_DIM
    c = lax.broadcasted_iota(jnp.int32, (width, width), 1) // HEAD_DIM
    return jnp.where(r == c, 1.0 / HEAD_DIM, 0.0).astype(BF16)


def _head_mean_square(y, seg):
    sq = y * y
    hi = sq.astype(BF16)
    lo = (sq - hi.astype(F32)).astype(BF16)
    return _dot(hi, seg) + _dot(lo, seg)


def _swap16(x):
    lane = lax.broadcasted_iota(jnp.int32, x.shape, 1)
    fwd = pltpu.roll(x, LANES - 16, axis=1)
    bwd = pltpu.roll(x, 16, axis=1)
    return jnp.where((lane & 16) == 0, fwd, bwd)


def _rope128(x, cos, sin):
    return x * cos + _swap16(x) * sin


N_MOD = 6 * D_MODEL
MOD_TN = 768
MOD_TILES = N_MOD // MOD_TN


def _mod_tile(cond_ref, w_ref, b_ref, o_ref):
    c = cond_ref[...]
    s = (c * _sigmoid(c)).astype(BF16)
    o_ref[...] = _dot(s, w_ref[...].astype(BF16)) + b_ref[...]


def _modulation(cond8, w_mod, b_mod, l):
    tn = 2 * MOD_TN
    return pl.pallas_call(
        _mod_tile,
        grid=(N_MOD // tn,),
        in_specs=[pl.BlockSpec((8, D_MODEL), lambda j: (0, 0)),
                  pl.BlockSpec((None, D_MODEL, tn), lambda j: (l, 0, j)),
                  pl.BlockSpec((None, 1, tn), lambda j: (l, 0, j))],
        out_specs=pl.BlockSpec((8, tn), lambda j: (0, j)),
        out_shape=jax.ShapeDtypeStruct((8, N_MOD), F32),
        compiler_params=_params(),
        name="modulation",
    )(cond8, w_mod, b_mod.reshape(DEPTH, 1, N_MOD))


PJ_TILES = COL_GATE
PJ_STEPS = PJ_TILES + N_CHUNKS


def _seq_pos(latent, width):
    row = lax.broadcasted_iota(jnp.int32, (CHUNK, width), 0)
    return row if latent else row & (SEQ - 1)


def _shift_rows(x, k, pos, seq_len):
    rolled = pltpu.roll(x, (-k) % CHUNK, axis=0)
    ok = (pos + k >= 0) & (pos + k < seq_len)
    return jnp.where(ok, rolled, 0.0)


def _window_mean_minus_token(p, half, pos, seq_len):
    fwd = p
    bwd = _shift_rows(p, -1, pos, seq_len)
    m = 1
    while m < half:
        fwd = fwd + _shift_rows(fwd, m, pos, seq_len)
        bwd = bwd + _shift_rows(bwd, -m, pos, seq_len)
        m *= 2
    count = jnp.minimum(pos + half, seq_len) - jnp.maximum(pos - half, 0)
    return (fwd + bwd) / count.astype(F32) - p


def _proj_chunk(latent, x_ref, g_ref, sh_ref, sc_ref, gq_ref, gk_ref, cos_ref, sin_ref, cw_ref, pw_ref,
                ps_ref, q_ref, kv_ref, conv_ref, pool_ref, new_ref, w_s, h_s):
    h_s[...] = _norm_modulate(x_ref[...], g_ref[...], sc_ref[...], sh_ref[...])
    seq_len = DEC_SEQ if latent else SEQ
    seg_q = _head_segments(TN)
    seg_k = _head_segments(KV_W)
    pos_wide = _seq_pos(latent, TN)
    pos = _seq_pos(latent, LANES)

    def q_tile(mixer, j, y):
        y = y * lax.rsqrt(_head_mean_square(y, seg_q) + EPS) * (gq_ref[mixer] * (HEAD_DIM ** -0.5))
        for s in range(TN // LANES):
            part = y[:, s * LANES:(s + 1) * LANES]
            if latent:
                part = _rope128(part, cos_ref[...], sin_ref[...])
            out_col = mixer * ATT_W + j * TN + s * LANES
            q_ref[:, out_col:out_col + LANES] = part.astype(BF16)

    def kv_tile(mixer, y):
        k = y[:, :KV_W]
        v = y[:, KV_W:]
        k = k * lax.rsqrt(_head_mean_square(k, seg_k) + EPS) * gk_ref[mixer]
        if latent:
            k = _rope128(k, cos_ref[...], sin_ref[...])
        else:
            for b in range(SEQ_PER_CHUNK):
                new_ref[b, 2 * mixer] = k[b * SEQ:(b + 1) * SEQ, :].T
                new_ref[b, 2 * mixer + 1] = v[b * SEQ:(b + 1) * SEQ, :].T
        kv_ref[:, 2 * mixer * KV_W:(2 * mixer + 1) * KV_W] = k.astype(BF16)
        kv_ref[:, (2 * mixer + 1) * KV_W:(2 * mixer + 2) * KV_W] = v.astype(BF16)

    def conv_tile(j, gc, u, gb):
        cols = slice(j * TN, (j + 1) * TN)
        z = gc * u
        y = (_shift_rows(z, -1, pos_wide, seq_len) * cw_ref[0:1, cols] + z * cw_ref[1:2, cols]
             + _shift_rows(z, 1, pos_wide, seq_len) * cw_ref[2:3, cols])
        conv_ref[:, cols] = (gb * y).astype(BF16)

    def pool_tile(j, p):
        for s in range(TN // LANES):
            grp = j * (TN // LANES) + s
            cols = slice(grp * LANES, (grp + 1) * LANES)
            y = _window_mean_minus_token(p[:, s * LANES:(s + 1) * LANES], POOL_SIZES[grp] // 2, pos, seq_len)
            pool_ref[:, cols] = (_dot(y.astype(BF16), pw_ref[grp].astype(BF16)) * ps_ref[:, cols]).astype(BF16)

    tasks = []
    for j in range(POOL_W // TN):
        tasks.append(((COL_PV + j,), functools.partial(pool_tile, j)))
        tasks.append(((COL_GC + j, COL_U + j, COL_GB + j), functools.partial(conv_tile, j)))
    for mixer, col in enumerate((COL_QA, COL_QC)):
        for j in range(ATT_W // TN):
            tasks.append(((col + j,), functools.partial(q_tile, mixer, j)))
    for mixer, col in enumerate((COL_KVA, COL_KVC)):
        tasks.append(((col,), functools.partial(kv_tile, mixer)))

    def matmuls(tiles):
        return [_dot(h_s[...], w_s[t]) for t in tiles]

    ys = matmuls(tasks[0][0])
    for i, (_, epilogue) in enumerate(tasks):
        nxt = matmuls(tasks[i + 1][0]) if i + 1 < len(tasks) else None
        epilogue(*ys)
        ys = nxt


def _proj_kernel(*refs, first):
    if first:
        xp_ref, xs_ref, *refs = refs
    else:
        xp_ref = xs_ref = refs[0]
        refs = refs[1:]
    (g_ref, sh_ref, sc_ref, w_ref, gq_ref, gk_ref, cos_ref, sin_ref, cw_ref, pw_ref, ps_ref,
     q_ref, kv_ref, conv_ref, pool_ref, new_ref, *refs) = refs
    xcat_ref = refs[0] if first else None
    w_s, h_s = refs[-2:]
    s = pl.program_id(0)
    rest = (g_ref, sh_ref, sc_ref, gq_ref, gk_ref, cos_ref, sin_ref, cw_ref, pw_ref, ps_ref,
            q_ref, kv_ref, conv_ref, pool_ref, new_ref, w_s, h_s)

    @pl.when(s < PJ_TILES)
    def _stream():
        w_s[s] = w_ref[...].astype(BF16)

    @pl.when((s >= PJ_TILES) & (s < PJ_TILES + N_P_CHUNKS))
    def _prompt():
        if first:
            xcat_ref[...] = xp_ref[...]
        _proj_chunk(False, xp_ref, *rest)

    @pl.when(s >= PJ_TILES + N_P_CHUNKS)
    def _latent():
        if first:
            xcat_ref[...] = xs_ref[...]
        _proj_chunk(True, xs_ref, *rest)


def _proj(xs, g_mix, modc, w_in, gq2, gk2, cos, sin, conv_w, pool_w, pool_scale, l):
    first = len(xs) == 2

    def chunk(s):
        return jnp.clip(s - PJ_TILES, 0, N_CHUNKS - 1)

    def p_chunk(s):
        return jnp.clip(s - PJ_TILES, 0, N_P_CHUNKS - 1)

    def s_chunk(s):
        return jnp.clip(s - PJ_TILES - N_P_CHUNKS, 0, N_CHUNKS - N_P_CHUNKS - 1)

    def mod_spec(k):
        return pl.BlockSpec((None, 1, D_MODEL), lambda s: (chunk(s), 0, k))

    def const(shape):
        return pl.BlockSpec(shape, lambda s: (l,) + (0,) * (len(shape) - 1))

    def rows_out(width):
        return pl.BlockSpec((CHUNK, width), lambda s: (chunk(s), 0))

    if first:
        x_specs = [pl.BlockSpec((CHUNK, D_MODEL), lambda s: (p_chunk(s), 0)),
                   pl.BlockSpec((CHUNK, D_MODEL), lambda s: (s_chunk(s), 0))]
    else:
        x_specs = [rows_out(D_MODEL)]
    out_specs = [rows_out(2 * ATT_W), rows_out(4 * KV_W), rows_out(CONV_W), rows_out(POOL_W),
                 pl.BlockSpec((SEQ_PER_CHUNK, 4, KV_W, SEQ), lambda s: (p_chunk(s), 0, 0, 0))]
    out_shape = [jax.ShapeDtypeStruct((ROWS, 2 * ATT_W), BF16),
                 jax.ShapeDtypeStruct((ROWS, 4 * KV_W), BF16),
                 jax.ShapeDtypeStruct((ROWS, CONV_W), BF16),
                 jax.ShapeDtypeStruct((ROWS, POOL_W), BF16),
                 jax.ShapeDtypeStruct((BATCH, 4, KV_W, SEQ), F32)]
    if first:
        out_specs.append(rows_out(D_MODEL))
        out_shape.append(jax.ShapeDtypeStruct((ROWS, D_MODEL), F32))
    return pl.pallas_call(
        functools.partial(_proj_kernel, first=first),
        grid=(PJ_STEPS,),
        in_specs=x_specs + [
            const((None, 1, D_MODEL)), mod_spec(0), mod_spec(1),
            pl.BlockSpec((None, D_MODEL, TN), lambda s: (l, 0, jnp.minimum(s, PJ_TILES - 1))),
            const((None, 2, 1, TN)), const((None, 2, 1, KV_W)),
            pl.BlockSpec((DEC_SEQ, LANES), lambda s: (0, 0)),
            pl.BlockSpec((DEC_SEQ, LANES), lambda s: (0, 0)),
            const((None, 3, CONV_W)), const((None, len(POOL_SIZES), LANES, LANES)),
            const((None, 1, POOL_W))],
        out_specs=out_specs,
        out_shape=out_shape,
        scratch_shapes=[pltpu.VMEM((PJ_TILES, D_MODEL, TN), BF16),
                        pltpu.VMEM((CHUNK, D_MODEL), BF16)],
        compiler_params=_params(),
        name="proj",
    )(*xs, g_mix.reshape(DEPTH, 1, D_MODEL), modc, modc, w_in, gq2, gk2, cos, sin,
      conv_w, pool_w, pool_scale.reshape(DEPTH, 1, POOL_W))


TQ = SEQ
N_QT = DEC_SEQ // TQ
WIN_KEYS = TQ + 2 * WINDOW
PAD_SEQ = WINDOW + DEC_SEQ + WINDOW
ATT_STEPS = BATCH + DEC_BATCH * N_QT


def _split_lanes(x):
    xr = pltpu.roll(x, HEAD_DIM, axis=1)
    low = lax.broadcasted_iota(jnp.int32, x.shape, 1) < HEAD_DIM
    return (jnp.where(low, x, 0.0).astype(BF16), jnp.where(low, 0.0, xr).astype(BF16),
            jnp.where(low, xr, 0.0).astype(BF16), jnp.where(low, 0.0, x).astype(BF16))


def _split_rows(xt):
    z = jnp.zeros((HEAD_DIM, xt.shape[1]), F32)
    h0, h1 = xt[:HEAD_DIM], xt[HEAD_DIM:]
    return tuple(jnp.concatenate(p, axis=0).astype(BF16) for p in ((h0, z), (z, h0), (h1, z), (z, h1)))


def _attend_pairs(tasks):
    units = [(t, parity) for t in range(len(tasks)) for parity in range(2)]

    def scores_of(unit):
        t, parity = unit
        q, pieces, _, _ = tasks[t]
        out = []
        for k_lo, k_hi, k_t, _, _, _, mask in pieces:
            k = k_hi if parity else k_lo
            s = _dot(q, k) if k_t else _dot_nt(q, k)
            if mask is not None:
                s = jnp.where(mask, s, NEG)
            out.append(s)
        return out

    scores = scores_of(units[0])
    even = None
    for i, (t, parity) in enumerate(units):
        nxt = scores_of(units[i + 1]) if i + 1 < len(units) else None
        _, pieces, sinks, write = tasks[t]
        m = functools.reduce(jnp.maximum, [jnp.max(s, axis=1, keepdims=True) for s in scores])
        if sinks is not None:
            m = jnp.maximum(m, sinks[parity])
        denom = None
        acc = None
        for s, (_, _, _, v_lo, v_hi, v_t, _) in zip(scores, pieces):
            p = jnp.exp(s - m)
            d = jnp.sum(p, axis=1, keepdims=True)
            v = v_hi if parity else v_lo
            a = _dot_nt(p.astype(BF16), v) if v_t else _dot(p.astype(BF16), v)
            denom = d if denom is None else denom + d
            acc = a if acc is None else acc + a
        if sinks is not None:
            denom = denom + jnp.exp(sinks[parity] - m)
        acc = acc / denom
        if parity == 0:
            even = acc
        else:
            write(even + acc)
        scores = nxt


def _attn_prompt_step(sink_ref, q_ref, kv_ref, o_ref, layer):
    def writer(col):
        def write(out):
            o_ref[:, col:col + LANES] = out.astype(BF16)
        return write

    tasks = []
    for mixer in range(2):
        k = _split_lanes(kv_ref[:, 2 * mixer * KV_W:(2 * mixer + 1) * KV_W].astype(F32))
        v = _split_lanes(kv_ref[:, (2 * mixer + 1) * KV_W:(2 * mixer + 2) * KV_W].astype(F32))
        for pair in range(N_Q_HEADS // 2):
            lo = 2 * (pair // 2)
            piece = (k[lo], k[lo + 1], False, v[lo], v[lo + 1], False, None)
            sinks = None
            if mixer == 1:
                sinks = (sink_ref[layer, 2 * pair], sink_ref[layer, 2 * pair + 1])
            col = mixer * ATT_W + pair * LANES
            tasks.append((q_ref[:, col:col + LANES], [piece], sinks, writer(col)))
    _attend_pairs(tasks)


def _attn_latent_step(qt, sink_ref, q_ref, kv_ref, cka_ref, cva_ref, ckc_ref, cvc_ref, o_ref,
                      ctx_s, ka_s, va_s, kc_s, vc_s, layer):
    @pl.when(qt == 0)
    def _fill():
        for i, ref in enumerate((cka_ref, cva_ref, ckc_ref, cvc_ref)):
            for j, part in enumerate(_split_rows(ref[...])):
                ctx_s[4 * i + j] = part
        for dst, col in ((ka_s, 0), (va_s, KV_W)):
            for j, part in enumerate(_split_lanes(kv_ref[:, col:col + KV_W].astype(F32))):
                dst[j] = part
        zeros = jnp.zeros((WINDOW, LANES), BF16)
        for dst, col in ((kc_s, 2 * KV_W), (vc_s, 3 * KV_W)):
            for j, part in enumerate(_split_lanes(kv_ref[:, col:col + KV_W].astype(F32))):
                dst[j, 0:WINDOW, :] = zeros
                dst[j, WINDOW:WINDOW + DEC_SEQ, :] = part
                dst[j, WINDOW + DEC_SEQ:PAD_SEQ, :] = zeros

    q0 = qt * TQ
    win = pl.ds(pl.multiple_of(q0, TQ), WIN_KEYS)
    r = lax.broadcasted_iota(jnp.int32, (TQ, WIN_KEYS), 0)
    jk = lax.broadcasted_iota(jnp.int32, (TQ, WIN_KEYS), 1)
    kpos = q0 - WINDOW + jk
    band = jnp.where((jk - r >= 0) & (jk - r <= 2 * WINDOW), kpos, -1)
    mask = (band >= 0) & (band < DEC_SEQ)

    def writer(col):
        def write(out):
            o_ref[:, col:col + LANES] = out.astype(BF16)
        return write

    tasks = []
    for pair in range(N_Q_HEADS // 2):
        lo = 2 * (pair // 2)
        hi = lo + 1
        col = pair * LANES
        ctx = (ctx_s[lo], ctx_s[hi], True, ctx_s[4 + lo], ctx_s[4 + hi], True, None)
        cur = (ka_s[lo], ka_s[hi], False, va_s[lo], va_s[hi], False, None)
        tasks.append((q_ref[:, col:col + LANES], [ctx, cur], None, writer(col)))

        col = ATT_W + pair * LANES
        ctx = (ctx_s[8 + lo], ctx_s[8 + hi], True, ctx_s[12 + lo], ctx_s[12 + hi], True, None)
        near = (kc_s[lo, win, :], kc_s[hi, win, :], False, vc_s[lo, win, :], vc_s[hi, win, :], False, mask)
        sinks = (sink_ref[layer, 2 * pair], sink_ref[layer, 2 * pair + 1])
        tasks.append((q_ref[:, col:col + LANES], [ctx, near], sinks, writer(col)))
    _attend_pairs(tasks)


def _attn_kernel(sink_ref, q_ref, kvp_ref, kvl_ref, cka_ref, cva_ref, ckc_ref, cvc_ref, *rest,
                 layer, n_new):
    new_refs = rest[:n_new]
    o_ref = rest[n_new]
    kind_refs = rest[n_new + 1:n_new + 1 + (4 if n_new else 0)]
    ctx_s, ka_s, va_s, kc_s, vc_s = rest[-5:]
    s = pl.program_id(0)

    @pl.when(s < BATCH)
    def _prompt():
        _attn_prompt_step(sink_ref, q_ref, kvp_ref, o_ref, layer)
        for kind, dst in enumerate(kind_refs):
            for depth, src in enumerate(new_refs):
                dst[0, depth] = src[0, kind]

    @pl.when(s >= BATCH)
    def _latent():
        _attn_latent_step((s - BATCH) % N_QT, sink_ref, q_ref, kvl_ref, cka_ref, cva_ref, ckc_ref, cvc_ref,
                          o_ref, ctx_s, ka_s, va_s, kc_s, vc_s, layer)


def _attention(sink, q, kv, cka, cva, ckc, cvc, l, new_kv=()):
    def latent_batch(s):
        return jnp.maximum(s - BATCH, 0) // N_QT

    def prompt_seq(s):
        return jnp.minimum(s, BATCH - 1)

    n_new = len(new_kv)
    cache_spec = pl.BlockSpec((None, None, KV_W, PAST_LEN), lambda s: (latent_batch(s), l, 0, 0))
    out_specs = [pl.BlockSpec((TQ, 2 * ATT_W), lambda s: (s, 0))]
    out_shape = [jax.ShapeDtypeStruct((ROWS, 2 * ATT_W), BF16)]
    if n_new:
        out_specs += [pl.BlockSpec((1, n_new, KV_W, SEQ), lambda s: (prompt_seq(s), 0, 0, 0))] * 4
        out_shape += [jax.ShapeDtypeStruct((BATCH, n_new, KV_W, SEQ), F32)] * 4
    outs = pl.pallas_call(
        functools.partial(_attn_kernel, layer=l, n_new=n_new),
        grid=(ATT_STEPS,),
        in_specs=[pl.BlockSpec(memory_space=pltpu.SMEM),
                  pl.BlockSpec((TQ, 2 * ATT_W), lambda s: (s, 0)),
                  pl.BlockSpec((SEQ, 4 * KV_W), lambda s: (prompt_seq(s), 0)),
                  pl.BlockSpec((DEC_SEQ, 4 * KV_W), lambda s: (N_P_CHUNKS + latent_batch(s), 0)),
                  cache_spec, cache_spec, cache_spec, cache_spec]
                 + [pl.BlockSpec((1, 4, KV_W, SEQ), lambda s: (prompt_seq(s), 0, 0, 0))] * n_new,
        out_specs=out_specs,
        out_shape=out_shape,
        scratch_shapes=[pltpu.VMEM((16, KV_W, PAST_LEN), BF16),
                        pltpu.VMEM((4, DEC_SEQ, LANES), BF16), pltpu.VMEM((4, DEC_SEQ, LANES), BF16),
                        pltpu.VMEM((4, PAD_SEQ, LANES), BF16), pltpu.VMEM((4, PAD_SEQ, LANES), BF16)],
        compiler_params=_params(),
        name="attention",
    )(sink, q, kv, kv, cka, cva, ckc, cvc, *new_kv)
    return outs if n_new else outs[0]


MG_ROWS = 512
MG_CHUNKS = ROWS // MG_ROWS
MG_TILES = D_MODEL // TN
MG_STEPS = MG_TILES + MG_CHUNKS


def _mixer_out_kernel(x_ref, xn_ref, g_ref, sh_ref, sc_ref, gate_ref, att_ref, conv_ref, pool_ref,
                      wg0, wg1, wg2, wg3, wb0, wb1, wb2, wb3, wo_ref, o_ref,
                      wg_s, wb_s, wo_s, h_s, mix_s):
    s = pl.program_id(0)

    def norm():
        h_s[...] = _norm_modulate(xn_ref[...], g_ref[...], sc_ref[...], sh_ref[...])

    @pl.when(s == 0)
    def _first_norm():
        norm()

    def mix_tile(wgs, wbs):
        h = h_s[...]
        branches = (att_ref[:, :ATT_W], conv_ref[...], att_ref[:, ATT_W:], pool_ref[...])
        acc = None
        for br, wg, wb in zip(branches, wgs, wbs):
            term = _sigmoid(_dot(h, wg)) * _dot(br, wb)
            acc = term if acc is None else acc + term
        return acc.astype(BF16)

    def out():
        o_ref[...] = x_ref[...] + gate_ref[...] * _dot(mix_s[...], wo_s[...])

    @pl.when(s < MG_TILES)
    def _stream():
        wgs = [r[...].astype(BF16) for r in (wg0, wg1, wg2, wg3)]
        wbs = [r[...].astype(BF16) for r in (wb0, wb1, wb2, wb3)]
        for k in range(N_BRANCH):
            wg_s[s * N_BRANCH + k] = wgs[k]
            wb_s[s * N_BRANCH + k] = wbs[k]
        wo_s[pl.ds(pl.multiple_of(s * TN, TN), TN), :] = wo_ref[...].astype(BF16)
        v = mix_tile(wgs, wbs)
        for n in range(MG_TILES):
            @pl.when(s == n)
            def _store(n=n):
                mix_s[:, n * TN:(n + 1) * TN] = v

    @pl.when((s >= MG_TILES) & (s < MG_STEPS - 1))
    def _chunk():
        out()
        norm()
        for n in range(MG_TILES):
            mix_s[:, n * TN:(n + 1) * TN] = mix_tile(
                [wg_s[n * N_BRANCH + k] for k in range(N_BRANCH)],
                [wb_s[n * N_BRANCH + k] for k in range(N_BRANCH)])

    @pl.when(s == MG_STEPS - 1)
    def _last_chunk_out():
        out()


def _mixer_out(x, g_mix, modc, att, conv, pool, w_in, w_branch, w_out, l):
    def chunk(s):
        return jnp.maximum(s - MG_TILES, 0)

    def mix_chunk(s):
        return jnp.clip(s - (MG_TILES - 1), 0, MG_CHUNKS - 1)

    def tile(s):
        return jnp.minimum(s, MG_TILES - 1)

    def mod_spec(k, which):
        return pl.BlockSpec((None, 1, D_MODEL), lambda s: (which(s) * MG_ROWS // CHUNK, 0, k))

    def rows_in(width):
        return pl.BlockSpec((MG_ROWS, width), lambda s: (mix_chunk(s), 0))

    gate_specs = [pl.BlockSpec((None, D_MODEL, TN), functools.partial(
        lambda s, k: (l, 0, COL_GATE + MG_TILES * k + tile(s)), k=k)) for k in range(N_BRANCH)]
    br_specs = [pl.BlockSpec((None, None, BRANCH_W, TN), functools.partial(
        lambda s, k: (l, k, 0, tile(s)), k=k)) for k in range(N_BRANCH)]
    return pl.pallas_call(
        _mixer_out_kernel,
        grid=(MG_STEPS,),
        in_specs=[pl.BlockSpec((MG_ROWS, D_MODEL), lambda s: (chunk(s), 0)),
                  pl.BlockSpec((MG_ROWS, D_MODEL), lambda s: (mix_chunk(s), 0)),
                  pl.BlockSpec((None, 1, D_MODEL), lambda s: (l, 0, 0)),
                  mod_spec(0, mix_chunk), mod_spec(1, mix_chunk), mod_spec(2, chunk),
                  rows_in(2 * ATT_W), rows_in(CONV_W), rows_in(POOL_W)] + gate_specs + br_specs
                 + [pl.BlockSpec((None, TN, D_MODEL), lambda s: (l, tile(s), 0))],
        out_specs=pl.BlockSpec((MG_ROWS, D_MODEL), lambda s: (chunk(s), 0)),
        out_shape=jax.ShapeDtypeStruct((ROWS, D_MODEL), F32),
        scratch_shapes=[pltpu.VMEM((MG_TILES * N_BRANCH, D_MODEL, TN), BF16),
                        pltpu.VMEM((MG_TILES * N_BRANCH, BRANCH_W, TN), BF16),
                        pltpu.VMEM((D_MODEL, D_MODEL), BF16),
                        pltpu.VMEM((MG_ROWS, D_MODEL), BF16),
                        pltpu.VMEM((MG_ROWS, D_MODEL), BF16)],
        compiler_params=_params(),
        name="mixer_out",
    )(x, x, g_mix.reshape(DEPTH, 1, D_MODEL), modc, modc, modc, att, conv, pool,
      *([w_in] * N_BRANCH), *([w_branch] * N_BRANCH), w_out)


FF_ROWS = 512
FF_CHUNKS = ROWS // FF_ROWS
FF_P_CHUNKS = P_ROWS // FF_ROWS
FF_TILES = D_FF // TN
FF_STEPS = FF_TILES + FF_CHUNKS
FF_FIRST_LATENT_STEP = FF_TILES + FF_P_CHUNKS
assert MOD_TILES <= FF_CHUNKS - 1


def _ffn_kernel(x_ref, xn_ref, g_ref, sh_ref, sc_ref, gate_ref, wa_ref, wb_ref, wd_ref, *rest, last):
    if last:
        op_ref, os_ref, wa_s, wb_s, wd_s, h_s, act_s = rest
    else:
        cond_ref, wm_ref, bm_ref, op_ref, mod_ref, wa_s, wb_s, wd_s, h_s, act_s = rest
        os_ref = op_ref
    s = pl.program_id(0)

    def norm():
        h_s[...] = _norm_modulate(xn_ref[...], g_ref[...], sc_ref[...], sh_ref[...])

    def up(wa, wb):
        h = h_s[...]
        a = _dot(h, wa)
        return (a * _sigmoid(a) * _dot(h, wb)).astype(BF16)

    def down():
        return x_ref[...] + gate_ref[...] * _dot(act_s[...], wd_s[...])

    def store(out):
        if last:
            @pl.when(s < FF_FIRST_LATENT_STEP)
            def _store_prompt():
                op_ref[...] = out

            @pl.when(s >= FF_FIRST_LATENT_STEP)
            def _store_latent():
                os_ref[...] = out
        else:
            op_ref[...] = out

    @pl.when(s == 0)
    def _first_norm():
        norm()

    @pl.when(s < FF_TILES)
    def _stream():
        wa = wa_ref[...].astype(BF16)
        wb = wb_ref[...].astype(BF16)
        wa_s[s] = wa
        wb_s[s] = wb
        wd_s[pl.ds(pl.multiple_of(s * TN, TN), TN), :] = wd_ref[...].astype(BF16)
        v = up(wa, wb)
        for j in range(FF_TILES):
            @pl.when(s == j)
            def _store(j=j):
                act_s[:, j * TN:(j + 1) * TN] = v

    @pl.when((s >= FF_TILES) & (s < FF_STEPS - 1))
    def _chunk():
        out = down()
        norm()
        store(out)
        for j in range(FF_TILES):
            act_s[:, j * TN:(j + 1) * TN] = up(wa_s[j], wb_s[j])
        if not last:
            _mod_tile(cond_ref, wm_ref, bm_ref, mod_ref)

    @pl.when(s == FF_STEPS - 1)
    def _last_chunk_down():
        store(down())


def _ffn(x, g_ffn, modc, w_gate_up, w_down, l, cond8=None, w_mod=None, b_mod=None):
    last = cond8 is None

    def chunk(s):
        return jnp.maximum(s - FF_TILES, 0)

    def up_chunk(s):
        return jnp.clip(s - (FF_TILES - 1), 0, FF_CHUNKS - 1)

    def tile(s):
        return jnp.minimum(s, FF_TILES - 1)

    def mod_tile(s):
        return jnp.clip(s - FF_TILES, 0, MOD_TILES - 1)

    def mod_spec(k, which):
        return pl.BlockSpec((None, 1, D_MODEL), lambda s: (which(s) * FF_ROWS // CHUNK, 0, k))

    in_specs = [pl.BlockSpec((FF_ROWS, D_MODEL), lambda s: (chunk(s), 0)),
                pl.BlockSpec((FF_ROWS, D_MODEL), lambda s: (up_chunk(s), 0)),
                pl.BlockSpec((None, 1, D_MODEL), lambda s: (l, 0, 0)),
                mod_spec(3, up_chunk), mod_spec(4, up_chunk), mod_spec(5, chunk),
                pl.BlockSpec((None, D_MODEL, TN), lambda s: (l, 0, tile(s))),
                pl.BlockSpec((None, D_MODEL, TN), lambda s: (l, 0, FF_TILES + tile(s))),
                pl.BlockSpec((None, TN, D_MODEL), lambda s: (l, tile(s), 0))]
    args = [x, x, g_ffn.reshape(DEPTH, 1, D_MODEL), modc, modc, modc, w_gate_up, w_gate_up, w_down]
    if last:
        out_specs = [pl.BlockSpec((FF_ROWS, D_MODEL), lambda s: (jnp.minimum(chunk(s), FF_P_CHUNKS - 1), 0)),
                     pl.BlockSpec((FF_ROWS, D_MODEL), lambda s: (jnp.maximum(chunk(s) - FF_P_CHUNKS, 0), 0))]
        out_shape = [jax.ShapeDtypeStruct((P_ROWS, D_MODEL), F32), jax.ShapeDtypeStruct((S_ROWS, D_MODEL), F32)]
    else:
        in_specs += [pl.BlockSpec((8, D_MODEL), lambda s: (0, 0)),
                     pl.BlockSpec((None, D_MODEL, MOD_TN), lambda s: (l + 1, 0, mod_tile(s))),
                     pl.BlockSpec((None, 1, MOD_TN), lambda s: (l + 1, 0, mod_tile(s)))]
        args += [cond8, w_mod, b_mod.reshape(DEPTH, 1, N_MOD)]
        out_specs = [pl.BlockSpec((FF_ROWS, D_MODEL), lambda s: (chunk(s), 0)),
                     pl.BlockSpec((8, MOD_TN), lambda s: (0, mod_tile(s)))]
        out_shape = [jax.ShapeDtypeStruct((ROWS, D_MODEL), F32), jax.ShapeDtypeStruct((8, N_MOD), F32)]
    return pl.pallas_call(
        functools.partial(_ffn_kernel, last=last),
        grid=(FF_STEPS,),
        in_specs=in_specs,
        out_specs=out_specs,
        out_shape=out_shape,
        scratch_shapes=[pltpu.VMEM((FF_TILES, D_MODEL, TN), BF16),
                        pltpu.VMEM((FF_TILES, D_MODEL, TN), BF16),
                        pltpu.VMEM((D_FF, D_MODEL), BF16),
                        pltpu.VMEM((FF_ROWS, D_MODEL), BF16),
                        pltpu.VMEM((FF_ROWS, D_FF), BF16)],
        compiler_params=_params(),
        name="ffn",
    )(*args)


def _rope_tables():
    rows = DEC_SEQ // GRID_W
    row = jnp.repeat(jnp.arange(rows, dtype=F32), GRID_W)
    col = jnp.tile(jnp.arange(GRID_W, dtype=F32), rows)
    inv = 1.0 / (ROPE_THETA ** (jnp.arange(N_FREQ, dtype=F32) / N_FREQ))
    cr, sr = jnp.cos(row[:, None] * inv), jnp.sin(row[:, None] * inv)
    cc, sc = jnp.cos(col[:, None] * inv), jnp.sin(col[:, None] * inv)
    cos = jnp.concatenate([cr, cr, cc, cc], axis=1)
    sin = jnp.concatenate([-sr, sr, -sc, sc], axis=1)
    reps = LANES // HEAD_DIM
    return jnp.tile(cos, (1, reps)), jnp.tile(sin, (1, reps))


def kernel(x_prompt, x_sample, cache_k_attn, cache_v_attn, cache_k_win, cache_v_win, c, c_ctx,
           w_mod, b_mod, g_mix, g_ffn, w_in, gq_attn, gk_attn, gq_win, gk_win, sink_win,
           conv_w, pool_w, pool_scale, w_branch, w_out, w_gate_up, w_down):
    cond8 = jnp.zeros((8, D_MODEL), F32).at[0].set(c_ctx).at[1:1 + DEC_BATCH].set(c)
    chunk_cond = np.array([0] * N_P_CHUNKS + [1 + b for b in range(DEC_BATCH)])

    def per_chunk(mod):
        return mod[chunk_cond].reshape(N_CHUNKS, 1, N_MOD)

    modc = per_chunk(_modulation(cond8, w_mod, b_mod, 0))

    cos, sin = _rope_tables()
    gq2 = jnp.stack([jnp.tile(gq_attn, (1, TN // HEAD_DIM)), jnp.tile(gq_win, (1, TN // HEAD_DIM))],
                    axis=1).reshape(DEPTH, 2, 1, TN)
    gk2 = jnp.stack([jnp.tile(gk_attn, (1, KV_W // HEAD_DIM)), jnp.tile(gk_win, (1, KV_W // HEAD_DIM))],
                    axis=1).reshape(DEPTH, 2, 1, KV_W)
    caches = [a.transpose(0, 1, 3, 4, 2).reshape(DEC_BATCH, DEPTH, KV_W, PAST_LEN)
              for a in (cache_k_attn, cache_v_attn, cache_k_win, cache_v_win)]

    xs = (x_prompt.reshape(P_ROWS, D_MODEL), x_sample.reshape(S_ROWS, D_MODEL))
    new_kv = []
    for l in range(DEPTH):
        q, kv, conv, pool, new, *stacked = _proj(xs, g_mix, modc, w_in, gq2, gk2, cos, sin,
                                                 conv_w, pool_w, pool_scale, l)
        x = stacked[0] if stacked else xs[0]
        new_kv.append(new)
        if l + 1 < DEPTH:
            att = _attention(sink_win, q, kv, *caches, l)
        else:
            att, *by_kind = _attention(sink_win, q, kv, *caches, l, new_kv)
        x = _mixer_out(x, g_mix, modc, att, conv, pool, w_in, w_branch, w_out, l)
        if l + 1 < DEPTH:
            x, mod_next = _ffn(x, g_ffn, modc, w_gate_up, w_down, l, cond8, w_mod, b_mod)
            modc = per_chunk(mod_next)
            xs = (x,)
        else:
            y_prompt, y_sample = _ffn(x, g_ffn, modc, w_gate_up, w_down, l)

    outs = [a.reshape(BATCH, DEPTH, N_KV_HEADS, HEAD_DIM, SEQ).transpose(0, 1, 4, 2, 3) for a in by_kind]
    return (y_prompt.reshape(BATCH, SEQ, D_MODEL), y_sample.reshape(DEC_BATCH, DEC_SEQ, D_MODEL), *outs)
```

```python
import functools

import numpy as np
import jax
import jax.numpy as jnp
from jax import lax
from jax.experimental import pallas as pl
from jax.experimental.pallas import tpu as pltpu

D_MODEL = 1024
BATCH = 16
SEQ = 256
DEPTH = 4
DEC_BATCH = 2
DEC_SEQ = 1024
PAST_LEN = 512
GRID_W = 64
HEAD_DIM = 64
N_Q_HEADS = 8
N_KV_HEADS = 2
ATT_W = N_Q_HEADS * HEAD_DIM
KV_W = N_KV_HEADS * HEAD_DIM
N_FREQ = HEAD_DIM // 4
ROPE_THETA = 10000.0
CONV_W = 512
POOL_W = 512
POOL_SIZES = (2, 4, 8, 16)
N_BRANCH = 4
BRANCH_W = 512
D_FF = 2816
WINDOW = 128
EPS = 1e-6
NEG = -1e30

P_ROWS = BATCH * SEQ
S_ROWS = DEC_BATCH * DEC_SEQ
ROWS = P_ROWS + S_ROWS
CHUNK = 1024
N_CHUNKS = ROWS // CHUNK
N_P_CHUNKS = P_ROWS // CHUNK
SEQ_PER_CHUNK = CHUNK // SEQ
TN = 256
LANES = 128
VMEM_LIMIT = 58 * 1024 * 1024

COL_QA, COL_KVA, COL_QC, COL_KVC = 0, 2, 3, 5
COL_U, COL_GB, COL_GC, COL_PV, COL_GATE = 6, 8, 10, 12, 14

F32 = jnp.float32
BF16 = jnp.bfloat16


def _params():
    return pltpu.CompilerParams(dimension_semantics=("arbitrary",), vmem_limit_bytes=VMEM_LIMIT)


def _dot(a, b):
    return jnp.dot(a, b, preferred_element_type=F32)


def _dot_nt(a, b):
    return lax.dot_general(a, b, (((1,), (1,)), ((), ())), preferred_element_type=F32)


def _sigmoid(x):
    return 1.0 / (1.0 + jnp.exp(-x))


def _norm_modulate(x, g, scale, shift):
    y = x * lax.rsqrt(jnp.mean(x * x, axis=-1, keepdims=True) + EPS)
    return ((y * g) * (1.0 + scale) + shift).astype(BF16)


def _head_segments(width):
    r = lax.broadcasted_iota(jnp.int32, (width, width), 0) // HEAD_DIM
    c = lax.broadcasted_iota(jnp.int32, (width, width), 1) // HEAD_DIM
    return jnp.where(r == c, 1.0 / HEAD_DIM, 0.0).astype(BF16)


def _head_mean_square(y, seg):
    return _dot((y * y).astype(BF16), seg)


def _swap16(x):
    lane = lax.broadcasted_iota(jnp.int32, x.shape, 1)
    fwd = pltpu.roll(x, LANES - 16, axis=1)
    bwd = pltpu.roll(x, 16, axis=1)
    return jnp.where((lane & 16) == 0, fwd, bwd)


def _rope128(x, cos, sin):
    return x * cos + _swap16(x) * sin


N_MOD = 6 * D_MODEL
MOD_TN = 768
MOD_TILES = N_MOD // MOD_TN


def _mod_tile(cond_ref, w_ref, b_ref, o_ref):
    c = cond_ref[...]
    s = (c * _sigmoid(c)).astype(BF16)
    o_ref[...] = _dot(s, w_ref[...].astype(BF16)) + b_ref[...]


def _modulation(cond8, w_mod, b_mod, l):
    tn = 2 * MOD_TN
    return pl.pallas_call(
        _mod_tile,
        grid=(N_MOD // tn,),
        in_specs=[pl.BlockSpec((8, D_MODEL), lambda j: (0, 0)),
                  pl.BlockSpec((None, D_MODEL, tn), lambda j: (l, 0, j)),
                  pl.BlockSpec((None, 1, tn), lambda j: (l, 0, j))],
        out_specs=pl.BlockSpec((8, tn), lambda j: (0, j)),
        out_shape=jax.ShapeDtypeStruct((8, N_MOD), F32),
        compiler_params=_params(),
        name="modulation",
    )(cond8, w_mod, b_mod.reshape(DEPTH, 1, N_MOD))


PJ_TILES = COL_GATE


def _proj_loads(first):
    per_load = 1 if first else 2
    return per_load, PJ_TILES // per_load


def _seq_pos(latent, width):
    row = lax.broadcasted_iota(jnp.int32, (CHUNK, width), 0)
    return row if latent else row & (SEQ - 1)


def _shift_rows(x, k, pos, seq_len):
    rolled = pltpu.roll(x, (-k) % CHUNK, axis=0)
    ok = (pos + k >= 0) & (pos + k < seq_len)
    return jnp.where(ok, rolled, 0.0)


def _window_mean_minus_token(p, half, pos, seq_len):
    fwd = p
    bwd = _shift_rows(p, -1, pos, seq_len)
    m = 1
    while m < half:
        fwd = fwd + _shift_rows(fwd, m, pos, seq_len)
        bwd = bwd + _shift_rows(bwd, -m, pos, seq_len)
        m *= 2
    count = jnp.minimum(pos + half, seq_len) - jnp.maximum(pos - half, 0)
    return (fwd + bwd) / count.astype(F32) - p


def _proj_chunk(latent, x_ref, g_ref, sh_ref, sc_ref, gq_ref, gk_ref, cos_ref, sin_ref, cw_ref, pw_ref,
                ps_ref, q_ref, kv_ref, conv_ref, pool_ref, new_ref, w_s, h_s):
    h_s[...] = _norm_modulate(x_ref[...], g_ref[...], sc_ref[...], sh_ref[...])
    seq_len = DEC_SEQ if latent else SEQ
    seg_q = _head_segments(TN)
    seg_k = _head_segments(KV_W)
    pos_wide = _seq_pos(latent, TN)
    pos = _seq_pos(latent, LANES)

    def q_tile(mixer, j, y):
        y = y * lax.rsqrt(_head_mean_square(y, seg_q) + EPS) * (gq_ref[mixer] * (HEAD_DIM ** -0.5))
        for s in range(TN // LANES):
            part = y[:, s * LANES:(s + 1) * LANES]
            if latent:
                part = _rope128(part, cos_ref[...], sin_ref[...])
            out_col = mixer * ATT_W + j * TN + s * LANES
            q_ref[:, out_col:out_col + LANES] = part.astype(BF16)

    def kv_tile(mixer, y):
        k = y[:, :KV_W]
        v = y[:, KV_W:]
        k = k * lax.rsqrt(_head_mean_square(k, seg_k) + EPS) * gk_ref[mixer]
        if latent:
            k = _rope128(k, cos_ref[...], sin_ref[...])
        else:
            for b in range(SEQ_PER_CHUNK):
                new_ref[b, 2 * mixer] = k[b * SEQ:(b + 1) * SEQ, :].T
                new_ref[b, 2 * mixer + 1] = v[b * SEQ:(b + 1) * SEQ, :].T
        kv_ref[:, 2 * mixer * KV_W:(2 * mixer + 1) * KV_W] = k.astype(BF16)
        kv_ref[:, (2 * mixer + 1) * KV_W:(2 * mixer + 2) * KV_W] = v.astype(BF16)

    def conv_tile(j, gc, u, gb):
        cols = slice(j * TN, (j + 1) * TN)
        z = gc * u
        y = (_shift_rows(z, -1, pos_wide, seq_len) * cw_ref[0:1, cols] + z * cw_ref[1:2, cols]
             + _shift_rows(z, 1, pos_wide, seq_len) * cw_ref[2:3, cols])
        conv_ref[:, cols] = (gb * y).astype(BF16)

    def pool_tile(j, p):
        for s in range(TN // LANES):
            grp = j * (TN // LANES) + s
            cols = slice(grp * LANES, (grp + 1) * LANES)
            y = _window_mean_minus_token(p[:, s * LANES:(s + 1) * LANES], POOL_SIZES[grp] // 2, pos, seq_len)
            pool_ref[:, cols] = (_dot(y.astype(BF16), pw_ref[grp].astype(BF16)) * ps_ref[:, cols]).astype(BF16)

    tasks = []
    for j in range(POOL_W // TN):
        tasks.append(((COL_PV + j,), functools.partial(pool_tile, j)))
        tasks.append(((COL_GC + j, COL_U + j, COL_GB + j), functools.partial(conv_tile, j)))
    for mixer, col in enumerate((COL_QA, COL_QC)):
        for j in range(ATT_W // TN):
            tasks.append(((col + j,), functools.partial(q_tile, mixer, j)))
    for mixer, col in enumerate((COL_KVA, COL_KVC)):
        tasks.append(((col,), functools.partial(kv_tile, mixer)))

    def matmuls(tiles):
        return [_dot(h_s[...], w_s[t]) for t in tiles]

    ys = matmuls(tasks[0][0])
    for i, (_, epilogue) in enumerate(tasks):
        nxt = matmuls(tasks[i + 1][0]) if i + 1 < len(tasks) else None
        epilogue(*ys)
        ys = nxt


def _proj_kernel(*refs, first):
    if first:
        xp_ref, xs_ref, *refs = refs
    else:
        xp_ref = xs_ref = refs[0]
        refs = refs[1:]
    (g_ref, sh_ref, sc_ref, w_ref, gq_ref, gk_ref, cos_ref, sin_ref, cw_ref, pw_ref, ps_ref,
     q_ref, kv_ref, conv_ref, pool_ref, new_ref, *refs) = refs
    xcat_ref = refs[0] if first else None
    w_s, h_s = refs[-2:]
    per_load, n_loads = _proj_loads(first)
    s = pl.program_id(0)
    rest = (g_ref, sh_ref, sc_ref, gq_ref, gk_ref, cos_ref, sin_ref, cw_ref, pw_ref, ps_ref,
            q_ref, kv_ref, conv_ref, pool_ref, new_ref, w_s, h_s)

    @pl.when(s < n_loads)
    def _stream():
        for j in range(per_load):
            w_s[per_load * s + j] = w_ref[:, j * TN:(j + 1) * TN].astype(BF16)

    @pl.when((s >= n_loads) & (s < n_loads + N_P_CHUNKS))
    def _prompt():
        if first:
            xcat_ref[...] = xp_ref[...]
        _proj_chunk(False, xp_ref, *rest)

    @pl.when(s >= n_loads + N_P_CHUNKS)
    def _latent():
        if first:
            xcat_ref[...] = xs_ref[...]
        _proj_chunk(True, xs_ref, *rest)


def _proj(xs, g_mix, modc, w_in, gq2, gk2, cos, sin, conv_w, pool_w, pool_scale, l):
    first = len(xs) == 2
    per_load, n_loads = _proj_loads(first)

    def chunk(s):
        return jnp.clip(s - n_loads, 0, N_CHUNKS - 1)

    def p_chunk(s):
        return jnp.clip(s - n_loads, 0, N_P_CHUNKS - 1)

    def s_chunk(s):
        return jnp.clip(s - n_loads - N_P_CHUNKS, 0, N_CHUNKS - N_P_CHUNKS - 1)

    def mod_spec(k):
        return pl.BlockSpec((None, 1, D_MODEL), lambda s: (chunk(s), 0, k))

    def const(shape):
        return pl.BlockSpec(shape, lambda s: (l,) + (0,) * (len(shape) - 1))

    def rows_out(width):
        return pl.BlockSpec((CHUNK, width), lambda s: (chunk(s), 0))

    if first:
        x_specs = [pl.BlockSpec((CHUNK, D_MODEL), lambda s: (p_chunk(s), 0)),
                   pl.BlockSpec((CHUNK, D_MODEL), lambda s: (s_chunk(s), 0))]
    else:
        x_specs = [rows_out(D_MODEL)]
    out_specs = [rows_out(2 * ATT_W), rows_out(4 * KV_W), rows_out(CONV_W), rows_out(POOL_W),
                 pl.BlockSpec((SEQ_PER_CHUNK, 4, KV_W, SEQ), lambda s: (p_chunk(s), 0, 0, 0))]
    out_shape = [jax.ShapeDtypeStruct((ROWS, 2 * ATT_W), BF16),
                 jax.ShapeDtypeStruct((ROWS, 4 * KV_W), BF16),
                 jax.ShapeDtypeStruct((ROWS, CONV_W), BF16),
                 jax.ShapeDtypeStruct((ROWS, POOL_W), BF16),
                 jax.ShapeDtypeStruct((BATCH, 4, KV_W, SEQ), F32)]
    if first:
        out_specs.append(rows_out(D_MODEL))
        out_shape.append(jax.ShapeDtypeStruct((ROWS, D_MODEL), F32))
    return pl.pallas_call(
        functools.partial(_proj_kernel, first=first),
        grid=(n_loads + N_CHUNKS,),
        in_specs=x_specs + [
            const((None, 1, D_MODEL)), mod_spec(0), mod_spec(1),
            pl.BlockSpec((None, D_MODEL, per_load * TN), lambda s: (l, 0, jnp.minimum(s, n_loads - 1))),
            const((None, 2, 1, TN)), const((None, 2, 1, KV_W)),
            pl.BlockSpec((DEC_SEQ, LANES), lambda s: (0, 0)),
            pl.BlockSpec((DEC_SEQ, LANES), lambda s: (0, 0)),
            const((None, 3, CONV_W)), const((None, len(POOL_SIZES), LANES, LANES)),
            const((None, 1, POOL_W))],
        out_specs=out_specs,
        out_shape=out_shape,
        scratch_shapes=[pltpu.VMEM((PJ_TILES, D_MODEL, TN), BF16),
                        pltpu.VMEM((CHUNK, D_MODEL), BF16)],
        compiler_params=_params(),
        name="proj",
    )(*xs, g_mix.reshape(DEPTH, 1, D_MODEL), modc, modc, w_in, gq2, gk2, cos, sin,
      conv_w, pool_w, pool_scale.reshape(DEPTH, 1, POOL_W))


TQ = SEQ
N_QT = DEC_SEQ // TQ
WIN_KEYS = TQ + 2 * WINDOW
PAD_SEQ = WINDOW + DEC_SEQ + WINDOW


def _split_lanes(x):
    xr = pltpu.roll(x, HEAD_DIM, axis=1)
    low = lax.broadcasted_iota(jnp.int32, x.shape, 1) < HEAD_DIM
    return (jnp.where(low, x, 0.0).astype(BF16), jnp.where(low, 0.0, xr).astype(BF16),
            jnp.where(low, xr, 0.0).astype(BF16), jnp.where(low, 0.0, x).astype(BF16))


def _split_rows(xt):
    z = jnp.zeros((HEAD_DIM, xt.shape[1]), F32)
    h0, h1 = xt[:HEAD_DIM], xt[HEAD_DIM:]
    return tuple(jnp.concatenate(p, axis=0).astype(BF16) for p in ((h0, z), (z, h0), (h1, z), (z, h1)))


def _attend_pairs(tasks):
    units = [(t, parity) for t in range(len(tasks)) for parity in range(2)]

    def scores_of(unit):
        t, parity = unit
        q, pieces, _, _ = tasks[t]
        out = []
        for k_lo, k_hi, k_t, _, _, _, mask in pieces:
            k = k_hi if parity else k_lo
            s = _dot(q, k) if k_t else _dot_nt(q, k)
            if mask is not None:
                s = jnp.where(mask, s, NEG)
            out.append(s)
        return out

    scores = scores_of(units[0])
    even = None
    for i, (t, parity) in enumerate(units):
        nxt = scores_of(units[i + 1]) if i + 1 < len(units) else None
        _, pieces, sinks, write = tasks[t]
        m = functools.reduce(jnp.maximum, [jnp.max(s, axis=1, keepdims=True) for s in scores])
        if sinks is not None:
            m = jnp.maximum(m, sinks[parity])
        denom = None
        acc = None
        for s, (_, _, _, v_lo, v_hi, v_t, _) in zip(scores, pieces):
            p = jnp.exp(s - m)
            d = jnp.sum(p, axis=1, keepdims=True)
            v = v_hi if parity else v_lo
            a = _dot_nt(p.astype(BF16), v) if v_t else _dot(p.astype(BF16), v)
            denom = d if denom is None else denom + d
            acc = a if acc is None else acc + a
        if sinks is not None:
            denom = denom + jnp.exp(sinks[parity] - m)
        acc = acc / denom
        if parity == 0:
            even = acc
        else:
            write(even + acc)
        scores = nxt


def _attn_prompt_seq(rows, sink_ref, q_ref, kv_ref, o_ref, layer):
    def writer(col):
        def write(out):
            o_ref[rows, col:col + LANES] = out.astype(BF16)
        return write

    tasks = []
    for mixer in range(2):
        k = _split_lanes(kv_ref[rows, 2 * mixer * KV_W:(2 * mixer + 1) * KV_W].astype(F32))
        v = _split_lanes(kv_ref[rows, (2 * mixer + 1) * KV_W:(2 * mixer + 2) * KV_W].astype(F32))
        for pair in range(N_Q_HEADS // 2):
            lo = 2 * (pair // 2)
            piece = (k[lo], k[lo + 1], False, v[lo], v[lo + 1], False, None)
            sinks = None
            if mixer == 1:
                sinks = (sink_ref[layer, 2 * pair], sink_ref[layer, 2 * pair + 1])
            col = mixer * ATT_W + pair * LANES
            tasks.append((q_ref[rows, col:col + LANES],[piece], sinks, writer(col)))
    _attend_pairs(tasks)


def _attn_latent_fill(kv_ref, cka_ref, cva_ref, ckc_ref, cvc_ref, ctx_s, ka_s, va_s, kc_s, vc_s):
    for i, ref in enumerate((cka_ref, cva_ref, ckc_ref, cvc_ref)):
        for j, part in enumerate(_split_rows(ref[...])):
            ctx_s[4 * i + j] = part
    for dst, col in ((ka_s, 0), (va_s, KV_W)):
        for j, part in enumerate(_split_lanes(kv_ref[:, col:col + KV_W].astype(F32))):
            dst[j] = part
    zeros = jnp.zeros((WINDOW, LANES), BF16)
    for dst, col in ((kc_s, 2 * KV_W), (vc_s, 3 * KV_W)):
        for j, part in enumerate(_split_lanes(kv_ref[:, col:col + KV_W].astype(F32))):
            dst[j, 0:WINDOW, :] = zeros
            dst[j, WINDOW:WINDOW + DEC_SEQ, :] = part
            dst[j, WINDOW + DEC_SEQ:PAD_SEQ, :] = zeros


def _attn_latent_tile(qt, sink_ref, q_ref, o_ref, ctx_s, ka_s, va_s, kc_s, vc_s, layer):
    q0 = qt * TQ
    rows = pl.ds(pl.multiple_of(q0, TQ), TQ)
    win = pl.ds(pl.multiple_of(q0, TQ), WIN_KEYS)
    r = lax.broadcasted_iota(jnp.int32, (TQ, WIN_KEYS), 0)
    jk = lax.broadcasted_iota(jnp.int32, (TQ, WIN_KEYS), 1)
    kpos = q0 - WINDOW + jk
    band = jnp.where((jk - r >= 0) & (jk - r <= 2 * WINDOW), kpos, -1)
    mask = (band >= 0) & (band < DEC_SEQ)

    def writer(col):
        def write(out):
            o_ref[rows, col:col + LANES] = out.astype(BF16)
        return write

    tasks = []
    for pair in range(N_Q_HEADS // 2):
        lo = 2 * (pair // 2)
        hi = lo + 1
        col = pair * LANES
        ctx = (ctx_s[lo], ctx_s[hi], True, ctx_s[4 + lo], ctx_s[4 + hi], True, None)
        cur = (ka_s[lo], ka_s[hi], False, va_s[lo], va_s[hi], False, None)
        tasks.append((q_ref[rows, col:col + LANES],[ctx, cur], None, writer(col)))

        col = ATT_W + pair * LANES
        ctx = (ctx_s[8 + lo], ctx_s[8 + hi], True, ctx_s[12 + lo], ctx_s[12 + hi], True, None)
        near = (kc_s[lo, win, :], kc_s[hi, win, :], False, vc_s[lo, win, :], vc_s[hi, win, :], False, mask)
        sinks = (sink_ref[layer, 2 * pair], sink_ref[layer, 2 * pair + 1])
        tasks.append((q_ref[rows, col:col + LANES],[ctx, near], sinks, writer(col)))
    _attend_pairs(tasks)


def _attn_kernel(sink_ref, q_ref, kv_ref, cka_ref, cva_ref, ckc_ref, cvc_ref, o_ref,
                 ctx_s, ka_s, va_s, kc_s, vc_s, *, layer):
    s = pl.program_id(0)

    @pl.when(s < N_P_CHUNKS)
    def _prompt():
        def seq(b, carry):
            _attn_prompt_seq(pl.ds(pl.multiple_of(b * SEQ, SEQ), SEQ), sink_ref, q_ref, kv_ref, o_ref, layer)
            return carry
        lax.fori_loop(0, SEQ_PER_CHUNK, seq, 0)

    @pl.when(s >= N_P_CHUNKS)
    def _latent():
        _attn_latent_fill(kv_ref, cka_ref, cva_ref, ckc_ref, cvc_ref, ctx_s, ka_s, va_s, kc_s, vc_s)

        def tile(qt, carry):
            _attn_latent_tile(qt, sink_ref, q_ref, o_ref, ctx_s, ka_s, va_s, kc_s, vc_s, layer)
            return carry
        lax.fori_loop(0, N_QT, tile, 0)


def _attention(sink, q, kv, cka, cva, ckc, cvc, l):
    cache_spec = pl.BlockSpec((None, None, KV_W, PAST_LEN),
                              lambda s: (jnp.maximum(s - N_P_CHUNKS, 0), l, 0, 0))
    return pl.pallas_call(
        functools.partial(_attn_kernel, layer=l),
        grid=(N_CHUNKS,),
        in_specs=[pl.BlockSpec(memory_space=pltpu.SMEM),
                  pl.BlockSpec((CHUNK, 2 * ATT_W), lambda s: (s, 0)),
                  pl.BlockSpec((CHUNK, 4 * KV_W), lambda s: (s, 0)),
                  cache_spec, cache_spec, cache_spec, cache_spec],
        out_specs=pl.BlockSpec((CHUNK, 2 * ATT_W), lambda s: (s, 0)),
        out_shape=jax.ShapeDtypeStruct((ROWS, 2 * ATT_W), BF16),
        scratch_shapes=[pltpu.VMEM((16, KV_W, PAST_LEN), BF16),
                        pltpu.VMEM((4, DEC_SEQ, LANES), BF16), pltpu.VMEM((4, DEC_SEQ, LANES), BF16),
                        pltpu.VMEM((4, PAD_SEQ, LANES), BF16), pltpu.VMEM((4, PAD_SEQ, LANES), BF16)],
        compiler_params=_params(),
        name="attention",
    )(sink, q, kv, cka, cva, ckc, cvc)


MG_ROWS = 512
MG_CHUNKS = ROWS // MG_ROWS
MG_TILES = D_MODEL // TN
MG_STEPS = MG_TILES + MG_CHUNKS


def _mixer_out_kernel(x_ref, xn_ref, g_ref, sh_ref, sc_ref, gate_ref, att_ref, conv_ref, pool_ref,
                      wg0, wg1, wg2, wg3, wb0, wb1, wb2, wb3, wo_ref, o_ref,
                      wg_s, wb_s, wo_s, h_s, mix_s):
    s = pl.program_id(0)

    def norm():
        h_s[...] = _norm_modulate(xn_ref[...], g_ref[...], sc_ref[...], sh_ref[...])

    @pl.when(s == 0)
    def _first_norm():
        norm()

    def mix_tile(wgs, wbs):
        h = h_s[...]
        branches = (att_ref[:, :ATT_W], conv_ref[...], att_ref[:, ATT_W:], pool_ref[...])
        acc = None
        for br, wg, wb in zip(branches, wgs, wbs):
            term = _sigmoid(_dot(h, wg)) * _dot(br, wb)
            acc = term if acc is None else acc + term
        return acc.astype(BF16)

    def out():
        o_ref[...] = x_ref[...] + gate_ref[...] * _dot(mix_s[...], wo_s[...])

    @pl.when(s < MG_TILES)
    def _stream():
        wgs = [r[...].astype(BF16) for r in (wg0, wg1, wg2, wg3)]
        wbs = [r[...].astype(BF16) for r in (wb0, wb1, wb2, wb3)]
        for k in range(N_BRANCH):
            wg_s[s * N_BRANCH + k] = wgs[k]
            wb_s[s * N_BRANCH + k] = wbs[k]
        wo_s[pl.ds(pl.multiple_of(s * TN, TN), TN), :] = wo_ref[...].astype(BF16)
        v = mix_tile(wgs, wbs)
        for n in range(MG_TILES):
            @pl.when(s == n)
            def _store(n=n):
                mix_s[:, n * TN:(n + 1) * TN] = v

    @pl.when((s >= MG_TILES) & (s < MG_STEPS - 1))
    def _chunk():
        out()
        norm()
        for n in range(MG_TILES):
            mix_s[:, n * TN:(n + 1) * TN] = mix_tile(
                [wg_s[n * N_BRANCH + k] for k in range(N_BRANCH)],
                [wb_s[n * N_BRANCH + k] for k in range(N_BRANCH)])

    @pl.when(s == MG_STEPS - 1)
    def _last_chunk_out():
        out()


def _mixer_out(x, g_mix, modc, att, conv, pool, w_in, w_branch, w_out, l):
    def chunk(s):
        return jnp.maximum(s - MG_TILES, 0)

    def mix_chunk(s):
        return jnp.clip(s - (MG_TILES - 1), 0, MG_CHUNKS - 1)

    def tile(s):
        return jnp.minimum(s, MG_TILES - 1)

    def mod_spec(k, which):
        return pl.BlockSpec((None, 1, D_MODEL), lambda s: (which(s) * MG_ROWS // CHUNK, 0, k))

    def rows_in(width):
        return pl.BlockSpec((MG_ROWS, width), lambda s: (mix_chunk(s), 0))

    gate_specs = [pl.BlockSpec((None, D_MODEL, TN), functools.partial(
        lambda s, k: (l, 0, COL_GATE + MG_TILES * k + tile(s)), k=k)) for k in range(N_BRANCH)]
    br_specs = [pl.BlockSpec((None, None, BRANCH_W, TN), functools.partial(
        lambda s, k: (l, k, 0, tile(s)), k=k)) for k in range(N_BRANCH)]
    return pl.pallas_call(
        _mixer_out_kernel,
        grid=(MG_STEPS,),
        in_specs=[pl.BlockSpec((MG_ROWS, D_MODEL), lambda s: (chunk(s), 0)),
                  pl.BlockSpec((MG_ROWS, D_MODEL), lambda s: (mix_chunk(s), 0)),
                  pl.BlockSpec((None, 1, D_MODEL), lambda s: (l, 0, 0)),
                  mod_spec(0, mix_chunk), mod_spec(1, mix_chunk), mod_spec(2, chunk),
                  rows_in(2 * ATT_W), rows_in(CONV_W), rows_in(POOL_W)] + gate_specs + br_specs
                 + [pl.BlockSpec((None, TN, D_MODEL), lambda s: (l, tile(s), 0))],
        out_specs=pl.BlockSpec((MG_ROWS, D_MODEL), lambda s: (chunk(s), 0)),
        out_shape=jax.ShapeDtypeStruct((ROWS, D_MODEL), F32),
        scratch_shapes=[pltpu.VMEM((MG_TILES * N_BRANCH, D_MODEL, TN), BF16),
                        pltpu.VMEM((MG_TILES * N_BRANCH, BRANCH_W, TN), BF16),
                        pltpu.VMEM((D_MODEL, D_MODEL), BF16),
                        pltpu.VMEM((MG_ROWS, D_MODEL), BF16),
                        pltpu.VMEM((MG_ROWS, D_MODEL), BF16)],
        compiler_params=_params(),
        name="mixer_out",
    )(x, x, g_mix.reshape(DEPTH, 1, D_MODEL), modc, modc, modc, att, conv, pool,
      *([w_in] * N_BRANCH), *([w_branch] * N_BRANCH), w_out)


FF_ROWS = 512
FF_CHUNKS = ROWS // FF_ROWS
FF_P_CHUNKS = P_ROWS // FF_ROWS
FF_TILES = D_FF // TN
FF_STEPS = FF_TILES + FF_CHUNKS
FF_FIRST_LATENT_STEP = FF_TILES + FF_P_CHUNKS
assert MOD_TILES <= FF_CHUNKS - 1
FF_REGROUP_EARLY = BATCH - FF_CHUNKS
assert 0 <= FF_REGROUP_EARLY <= FF_TILES


def _ffn_kernel(x_ref, xn_ref, g_ref, sh_ref, sc_ref, gate_ref, wa_ref, wb_ref, wd_ref, *rest, last):
    if last:
        new_refs = rest[:DEPTH]
        op_ref, os_ref = rest[DEPTH:DEPTH + 2]
        kind_refs = rest[DEPTH + 2:DEPTH + 6]
        wa_s, wb_s, wd_s, h_s, act_s = rest[DEPTH + 6:]
    else:
        cond_ref, wm_ref, bm_ref, op_ref, mod_ref, wa_s, wb_s, wd_s, h_s, act_s = rest
        os_ref = op_ref
    s = pl.program_id(0)

    def norm():
        h_s[...] = _norm_modulate(xn_ref[...], g_ref[...], sc_ref[...], sh_ref[...])

    def up(wa, wb):
        h = h_s[...]
        a = _dot(h, wa)
        return (a * _sigmoid(a) * _dot(h, wb)).astype(BF16)

    def down():
        return x_ref[...] + gate_ref[...] * _dot(act_s[...], wd_s[...])

    def store(out):
        if last:
            @pl.when(s < FF_FIRST_LATENT_STEP)
            def _store_prompt():
                op_ref[...] = out

            @pl.when(s >= FF_FIRST_LATENT_STEP)
            def _store_latent():
                os_ref[...] = out
        else:
            op_ref[...] = out

    @pl.when(s == 0)
    def _first_norm():
        norm()

    @pl.when(s < FF_TILES)
    def _stream():
        wa = wa_ref[...].astype(BF16)
        wb = wb_ref[...].astype(BF16)
        wa_s[s] = wa
        wb_s[s] = wb
        wd_s[pl.ds(pl.multiple_of(s * TN, TN), TN), :] = wd_ref[...].astype(BF16)
        v = up(wa, wb)
        for j in range(FF_TILES):
            @pl.when(s == j)
            def _store(j=j):
                act_s[:, j * TN:(j + 1) * TN] = v

    @pl.when((s >= FF_TILES) & (s < FF_STEPS - 1))
    def _chunk():
        out = down()
        norm()
        store(out)
        for j in range(FF_TILES):
            act_s[:, j * TN:(j + 1) * TN] = up(wa_s[j], wb_s[j])
        if not last:
            _mod_tile(cond_ref, wm_ref, bm_ref, mod_ref)

    @pl.when(s == FF_STEPS - 1)
    def _last_chunk_down():
        store(down())

    if last:
        @pl.when((s < FF_REGROUP_EARLY) | (s >= FF_TILES))
        def _regroup_new_cache():
            for kind, dst in enumerate(kind_refs):
                for depth, src in enumerate(new_refs):
                    dst[0, depth] = src[0, kind]


def _ffn(x, g_ffn, modc, w_gate_up, w_down, l, cond8=None, w_mod=None, b_mod=None, new_kv=None):
    last = cond8 is None

    def chunk(s):
        return jnp.maximum(s - FF_TILES, 0)

    def up_chunk(s):
        return jnp.clip(s - (FF_TILES - 1), 0, FF_CHUNKS - 1)

    def tile(s):
        return jnp.minimum(s, FF_TILES - 1)

    def mod_tile(s):
        return jnp.clip(s - FF_TILES, 0, MOD_TILES - 1)

    def mod_spec(k, which):
        return pl.BlockSpec((None, 1, D_MODEL), lambda s: (which(s) * FF_ROWS // CHUNK, 0, k))

    in_specs = [pl.BlockSpec((FF_ROWS, D_MODEL), lambda s: (chunk(s), 0)),
                pl.BlockSpec((FF_ROWS, D_MODEL), lambda s: (up_chunk(s), 0)),
                pl.BlockSpec((None, 1, D_MODEL), lambda s: (l, 0, 0)),
                mod_spec(3, up_chunk), mod_spec(4, up_chunk), mod_spec(5, chunk),
                pl.BlockSpec((None, D_MODEL, TN), lambda s: (l, 0, tile(s))),
                pl.BlockSpec((None, D_MODEL, TN), lambda s: (l, 0, FF_TILES + tile(s))),
                pl.BlockSpec((None, TN, D_MODEL), lambda s: (l, tile(s), 0))]
    args = [x, x, g_ffn.reshape(DEPTH, 1, D_MODEL), modc, modc, modc, w_gate_up, w_gate_up, w_down]
    if last:
        def seq(s):
            return jnp.minimum(s, FF_REGROUP_EARLY - 1) + jnp.maximum(s - (FF_TILES - 1), 0)

        in_specs += [pl.BlockSpec((1, 4, KV_W, SEQ), lambda s: (seq(s), 0, 0, 0))] * DEPTH
        args += list(new_kv)
        out_specs = [pl.BlockSpec((FF_ROWS, D_MODEL), lambda s: (jnp.minimum(chunk(s), FF_P_CHUNKS - 1), 0)),
                     pl.BlockSpec((FF_ROWS, D_MODEL), lambda s: (jnp.maximum(chunk(s) - FF_P_CHUNKS, 0), 0))]
        out_shape = [jax.ShapeDtypeStruct((P_ROWS, D_MODEL), F32), jax.ShapeDtypeStruct((S_ROWS, D_MODEL), F32)]
        out_specs += [pl.BlockSpec((1, DEPTH, KV_W, SEQ), lambda s: (seq(s), 0, 0, 0))] * 4
        out_shape += [jax.ShapeDtypeStruct((BATCH, DEPTH, KV_W, SEQ), F32)] * 4
    else:
        in_specs += [pl.BlockSpec((8, D_MODEL), lambda s: (0, 0)),
                     pl.BlockSpec((None, D_MODEL, MOD_TN), lambda s: (l + 1, 0, mod_tile(s))),
                     pl.BlockSpec((None, 1, MOD_TN), lambda s: (l + 1, 0, mod_tile(s)))]
        args += [cond8, w_mod, b_mod.reshape(DEPTH, 1, N_MOD)]
        out_specs = [pl.BlockSpec((FF_ROWS, D_MODEL), lambda s: (chunk(s), 0)),
                     pl.BlockSpec((8, MOD_TN), lambda s: (0, mod_tile(s)))]
        out_shape = [jax.ShapeDtypeStruct((ROWS, D_MODEL), F32), jax.ShapeDtypeStruct((8, N_MOD), F32)]
    return pl.pallas_call(
        functools.partial(_ffn_kernel, last=last),
        grid=(FF_STEPS,),
        in_specs=in_specs,
        out_specs=out_specs,
        out_shape=out_shape,
        scratch_shapes=[pltpu.VMEM((FF_TILES, D_MODEL, TN), BF16),
                        pltpu.VMEM((FF_TILES, D_MODEL, TN), BF16),
                        pltpu.VMEM((D_FF, D_MODEL), BF16),
                        pltpu.VMEM((FF_ROWS, D_MODEL), BF16),
                        pltpu.VMEM((FF_ROWS, D_FF), BF16)],
        compiler_params=_params(),
        name="ffn",
    )(*args)


def _rope_tables():
    rows = DEC_SEQ // GRID_W
    row = jnp.repeat(jnp.arange(rows, dtype=F32), GRID_W)
    col = jnp.tile(jnp.arange(GRID_W, dtype=F32), rows)
    inv = 1.0 / (ROPE_THETA ** (jnp.arange(N_FREQ, dtype=F32) / N_FREQ))
    cr, sr = jnp.cos(row[:, None] * inv), jnp.sin(row[:, None] * inv)
    cc, sc = jnp.cos(col[:, None] * inv), jnp.sin(col[:, None] * inv)
    cos = jnp.concatenate([cr, cr, cc, cc], axis=1)
    sin = jnp.concatenate([-sr, sr, -sc, sc], axis=1)
    reps = LANES // HEAD_DIM
    return jnp.tile(cos, (1, reps)), jnp.tile(sin, (1, reps))


def kernel(x_prompt, x_sample, cache_k_attn, cache_v_attn, cache_k_win, cache_v_win, c, c_ctx,
           w_mod, b_mod, g_mix, g_ffn, w_in, gq_attn, gk_attn, gq_win, gk_win, sink_win,
           conv_w, pool_w, pool_scale, w_branch, w_out, w_gate_up, w_down):
    cond8 = jnp.zeros((8, D_MODEL), F32).at[0].set(c_ctx).at[1:1 + DEC_BATCH].set(c)
    chunk_cond = np.array([0] * N_P_CHUNKS + [1 + b for b in range(DEC_BATCH)])

    def per_chunk(mod):
        return mod[chunk_cond].reshape(N_CHUNKS, 1, N_MOD)

    modc = per_chunk(_modulation(cond8, w_mod, b_mod, 0))

    cos, sin = _rope_tables()
    gq2 = jnp.stack([jnp.tile(gq_attn, (1, TN // HEAD_DIM)), jnp.tile(gq_win, (1, TN // HEAD_DIM))],
                    axis=1).reshape(DEPTH, 2, 1, TN)
    gk2 = jnp.stack([jnp.tile(gk_attn, (1, KV_W // HEAD_DIM)), jnp.tile(gk_win, (1, KV_W // HEAD_DIM))],
                    axis=1).reshape(DEPTH, 2, 1, KV_W)
    caches = [a.transpose(0, 1, 3, 4, 2).reshape(DEC_BATCH, DEPTH, KV_W, PAST_LEN)
              for a in (cache_k_attn, cache_v_attn, cache_k_win, cache_v_win)]

    xs = (x_prompt.reshape(P_ROWS, D_MODEL), x_sample.reshape(S_ROWS, D_MODEL))
    new_kv = []
    for l in range(DEPTH):
        q, kv, conv, pool, new, *stacked = _proj(xs, g_mix, modc, w_in, gq2, gk2, cos, sin,
                                                 conv_w, pool_w, pool_scale, l)
        x = stacked[0] if stacked else xs[0]
        new_kv.append(new)
        att = _attention(sink_win, q, kv, *caches, l)
        x = _mixer_out(x, g_mix, modc, att, conv, pool, w_in, w_branch, w_out, l)
        if l + 1 < DEPTH:
            x, mod_next = _ffn(x, g_ffn, modc, w_gate_up, w_down, l, cond8, w_mod, b_mod)
            modc = per_chunk(mod_next)
            xs = (x,)
        else:
            y_prompt, y_sample, *by_kind = _ffn(x, g_ffn, modc, w_gate_up, w_down, l, new_kv=new_kv)

    outs = [a.reshape(BATCH, DEPTH, N_KV_HEADS, HEAD_DIM, SEQ).transpose(0, 1, 4, 2, 3) for a in by_kind]
    return (y_prompt.reshape(BATCH, SEQ, D_MODEL), y_sample.reshape(DEC_BATCH, DEC_SEQ, D_MODEL), *outs)
```

```python
import functools

import numpy as np
import jax
import jax.numpy as jnp
from jax import lax
from jax.experimental import pallas as pl
from jax.experimental.pallas import tpu as pltpu

D_MODEL = 1024
BATCH = 16
SEQ = 256
DEPTH = 4
DEC_BATCH = 2
DEC_SEQ = 1024
PAST_LEN = 512
GRID_W = 64
HEAD_DIM = 64
N_Q_HEADS = 8
N_KV_HEADS = 2
ATT_W = N_Q_HEADS * HEAD_DIM
KV_W = N_KV_HEADS * HEAD_DIM
N_FREQ = HEAD_DIM // 4
ROPE_THETA = 10000.0
CONV_W = 512
POOL_W = 512
POOL_SIZES = (2, 4, 8, 16)
N_BRANCH = 4
BRANCH_W = 512
D_FF = 2816
WINDOW = 128
EPS = 1e-6
NEG = -1e30

P_ROWS = BATCH * SEQ
S_ROWS = DEC_BATCH * DEC_SEQ
ROWS = P_ROWS + S_ROWS
CHUNK = 1024
N_CHUNKS = ROWS // CHUNK
N_P_CHUNKS = P_ROWS // CHUNK
SEQ_PER_CHUNK = CHUNK // SEQ
TN = 256
LANES = 128
VMEM_LIMIT = 58 * 1024 * 1024

COL_QA, COL_KVA, COL_QC, COL_KVC = 0, 2, 3, 5
COL_U, COL_GB, COL_GC, COL_PV, COL_GATE = 6, 8, 10, 12, 14

F32 = jnp.float32
BF16 = jnp.bfloat16


def _params():
    return pltpu.CompilerParams(dimension_semantics=("arbitrary",), vmem_limit_bytes=VMEM_LIMIT)


def _dot(a, b):
    return jnp.dot(a, b, preferred_element_type=F32)


def _dot_nt(a, b):
    return lax.dot_general(a, b, (((1,), (1,)), ((), ())), preferred_element_type=F32)


def _sigmoid(x):
    return 1.0 / (1.0 + jnp.exp(-x))


def _norm_modulate(x, g, scale, shift):
    y = x * lax.rsqrt(jnp.mean(x * x, axis=-1, keepdims=True) + EPS)
    return ((y * g) * (1.0 + scale) + shift).astype(BF16)


def _head_segments(width):
    r = lax.broadcasted_iota(jnp.int32, (width, width), 0) // HEAD_DIM
    c = lax.broadcasted_iota(jnp.int32, (width, width), 1) // HEAD_DIM
    return jnp.where(r == c, 1.0 / HEAD_DIM, 0.0).astype(BF16)


def _head_mean_square(y, seg):
    return _dot((y * y).astype(BF16), seg)


def _swap16(x):
    lane = lax.broadcasted_iota(jnp.int32, x.shape, 1)
    fwd = pltpu.roll(x, LANES - 16, axis=1)
    bwd = pltpu.roll(x, 16, axis=1)
    return jnp.where((lane & 16) == 0, fwd, bwd)


def _rope128(x, cos, sin):
    return x * cos + _swap16(x) * sin


N_MOD = 6 * D_MODEL
COND_ROWS = 8
MOD_TN = 768
MOD_TILES = N_MOD // MOD_TN


def _mod_tile(cond_ref, w_ref, b_ref, o_ref):
    c = cond_ref[...]
    s = (c * _sigmoid(c)).astype(BF16)
    o_ref[...] = _dot(s, w_ref[...].astype(BF16)) + b_ref[...]


def _modulation(cond8, w_mod, b_mod, l):
    tn = 2 * MOD_TN
    return pl.pallas_call(
        _mod_tile,
        grid=(N_MOD // tn,),
        in_specs=[pl.BlockSpec((COND_ROWS, D_MODEL), lambda j: (0, 0)),
                  pl.BlockSpec((None, D_MODEL, tn), lambda j: (l, 0, j)),
                  pl.BlockSpec((None, 1, tn), lambda j: (l, 0, j))],
        out_specs=pl.BlockSpec((COND_ROWS, tn), lambda j: (0, j)),
        out_shape=jax.ShapeDtypeStruct((COND_ROWS, N_MOD), F32),
        compiler_params=_params(),
        name="modulation",
    )(cond8, w_mod, b_mod.reshape(DEPTH, 1, N_MOD))


PJ_TILES = COL_GATE


def _proj_loads(first):
    per_load = 1 if first else 2
    return per_load, PJ_TILES // per_load


def _seq_pos(latent, width):
    row = lax.broadcasted_iota(jnp.int32, (CHUNK, width), 0)
    return row if latent else row & (SEQ - 1)


def _shift_rows(x, k, pos, seq_len):
    rolled = pltpu.roll(x, (-k) % CHUNK, axis=0)
    ok = (pos + k >= 0) & (pos + k < seq_len)
    return jnp.where(ok, rolled, 0.0)


def _window_mean_minus_token(p, half, pos, seq_len):
    fwd = p
    bwd = _shift_rows(p, -1, pos, seq_len)
    m = 1
    while m < half:
        fwd = fwd + _shift_rows(fwd, m, pos, seq_len)
        bwd = bwd + _shift_rows(bwd, -m, pos, seq_len)
        m *= 2
    count = jnp.minimum(pos + half, seq_len) - jnp.maximum(pos - half, 0)
    return (fwd + bwd) / count.astype(F32) - p


def _proj_chunk(latent, x_ref, g_ref, sh_ref, sc_ref, gq_ref, gk_ref, cos_ref, sin_ref, cw_ref, pw_ref,
                ps_ref, q_ref, kv_ref, conv_ref, pool_ref, new_ref, w_s, h_s):
    h_s[...] = _norm_modulate(x_ref[...], g_ref[...], sc_ref[...], sh_ref[...])
    seq_len = DEC_SEQ if latent else SEQ
    seg_q = _head_segments(TN)
    seg_k = _head_segments(KV_W)
    pos_wide = _seq_pos(latent, TN)
    pos = _seq_pos(latent, LANES)

    def q_tile(mixer, j, y):
        y = y * lax.rsqrt(_head_mean_square(y, seg_q) + EPS) * (gq_ref[mixer] * (HEAD_DIM ** -0.5))
        for s in range(TN // LANES):
            part = y[:, s * LANES:(s + 1) * LANES]
            if latent:
                part = _rope128(part, cos_ref[...], sin_ref[...])
            out_col = mixer * ATT_W + j * TN + s * LANES
            q_ref[:, out_col:out_col + LANES] = part.astype(BF16)

    def kv_tile(mixer, y):
        k = y[:, :KV_W]
        v = y[:, KV_W:]
        k = k * lax.rsqrt(_head_mean_square(k, seg_k) + EPS) * gk_ref[mixer]
        if latent:
            k = _rope128(k, cos_ref[...], sin_ref[...])
        else:
            for b in range(SEQ_PER_CHUNK):
                new_ref[b, 2 * mixer] = k[b * SEQ:(b + 1) * SEQ, :].T
                new_ref[b, 2 * mixer + 1] = v[b * SEQ:(b + 1) * SEQ, :].T
        kv_ref[:, 2 * mixer * KV_W:(2 * mixer + 1) * KV_W] = k.astype(BF16)
        kv_ref[:, (2 * mixer + 1) * KV_W:(2 * mixer + 2) * KV_W] = v.astype(BF16)

    def conv_tile(j, gc, u, gb):
        cols = slice(j * TN, (j + 1) * TN)
        z = gc * u
        y = (_shift_rows(z, -1, pos_wide, seq_len) * cw_ref[0:1, cols] + z * cw_ref[1:2, cols]
             + _shift_rows(z, 1, pos_wide, seq_len) * cw_ref[2:3, cols])
        conv_ref[:, cols] = (gb * y).astype(BF16)

    def pool_tile(j, p):
        for s in range(TN // LANES):
            grp = j * (TN // LANES) + s
            cols = slice(grp * LANES, (grp + 1) * LANES)
            y = _window_mean_minus_token(p[:, s * LANES:(s + 1) * LANES], POOL_SIZES[grp] // 2, pos, seq_len)
            pool_ref[:, cols] = (_dot(y.astype(BF16), pw_ref[grp].astype(BF16)) * ps_ref[:, cols]).astype(BF16)

    tasks = []
    for j in range(POOL_W // TN):
        tasks.append(((COL_PV + j,), functools.partial(pool_tile, j)))
        tasks.append(((COL_GC + j, COL_U + j, COL_GB + j), functools.partial(conv_tile, j)))
    for mixer, col in enumerate((COL_QA, COL_QC)):
        for j in range(ATT_W // TN):
            tasks.append(((col + j,), functools.partial(q_tile, mixer, j)))
    for mixer, col in enumerate((COL_KVA, COL_KVC)):
        tasks.append(((col,), functools.partial(kv_tile, mixer)))

    def matmuls(tiles):
        return [_dot(h_s[...], w_s[t]) for t in tiles]

    ys = matmuls(tasks[0][0])
    for i, (_, epilogue) in enumerate(tasks):
        nxt = matmuls(tasks[i + 1][0]) if i + 1 < len(tasks) else None
        epilogue(*ys)
        ys = nxt


def _proj_kernel(*refs, first):
    if first:
        xp_ref, xs_ref, *refs = refs
    else:
        xp_ref = xs_ref = refs[0]
        refs = refs[1:]
    (g_ref, sh_ref, sc_ref, w_ref, gq_ref, gk_ref, cos_ref, sin_ref, cw_ref, pw_ref, ps_ref,
     q_ref, kv_ref, conv_ref, pool_ref, new_ref, *refs) = refs
    xcat_ref = refs[0] if first else None
    w_s, h_s = refs[-2:]
    per_load, n_loads = _proj_loads(first)
    s = pl.program_id(0)
    rest = (g_ref, sh_ref, sc_ref, gq_ref, gk_ref, cos_ref, sin_ref, cw_ref, pw_ref, ps_ref,
            q_ref, kv_ref, conv_ref, pool_ref, new_ref, w_s, h_s)

    @pl.when(s < n_loads)
    def _stream():
        for j in range(per_load):
            w_s[per_load * s + j] = w_ref[:, j * TN:(j + 1) * TN].astype(BF16)

    @pl.when((s >= n_loads) & (s < n_loads + N_P_CHUNKS))
    def _prompt():
        if first:
            xcat_ref[...] = xp_ref[...]
        _proj_chunk(False, xp_ref, *rest)

    @pl.when(s >= n_loads + N_P_CHUNKS)
    def _latent():
        if first:
            xcat_ref[...] = xs_ref[...]
        _proj_chunk(True, xs_ref, *rest)


def _proj(xs, g_mix, modc, w_in, gq2, gk2, cos, sin, conv_w, pool_w, pool_scale, l):
    first = len(xs) == 2
    per_load, n_loads = _proj_loads(first)

    def chunk(s):
        return jnp.clip(s - n_loads, 0, N_CHUNKS - 1)

    def p_chunk(s):
        return jnp.clip(s - n_loads, 0, N_P_CHUNKS - 1)

    def s_chunk(s):
        return jnp.clip(s - n_loads - N_P_CHUNKS, 0, N_CHUNKS - N_P_CHUNKS - 1)

    def mod_spec(k):
        return pl.BlockSpec((None, 1, D_MODEL), lambda s: (chunk(s), 0, k))

    def const(shape):
        return pl.BlockSpec(shape, lambda s: (l,) + (0,) * (len(shape) - 1))

    def rows_out(width):
        return pl.BlockSpec((CHUNK, width), lambda s: (chunk(s), 0))

    if first:
        x_specs = [pl.BlockSpec((CHUNK, D_MODEL), lambda s: (p_chunk(s), 0)),
                   pl.BlockSpec((CHUNK, D_MODEL), lambda s: (s_chunk(s), 0))]
    else:
        x_specs = [rows_out(D_MODEL)]
    out_specs = [rows_out(2 * ATT_W), rows_out(4 * KV_W), rows_out(CONV_W), rows_out(POOL_W),
                 pl.BlockSpec((SEQ_PER_CHUNK, 4, KV_W, SEQ), lambda s: (p_chunk(s), 0, 0, 0))]
    out_shape = [jax.ShapeDtypeStruct((ROWS, 2 * ATT_W), BF16),
                 jax.ShapeDtypeStruct((ROWS, 4 * KV_W), BF16),
                 jax.ShapeDtypeStruct((ROWS, CONV_W), BF16),
                 jax.ShapeDtypeStruct((ROWS, POOL_W), BF16),
                 jax.ShapeDtypeStruct((BATCH, 4, KV_W, SEQ), F32)]
    if first:
        out_specs.append(rows_out(D_MODEL))
        out_shape.append(jax.ShapeDtypeStruct((ROWS, D_MODEL), F32))
    return pl.pallas_call(
        functools.partial(_proj_kernel, first=first),
        grid=(n_loads + N_CHUNKS,),
        in_specs=x_specs + [
            const((None, 1, D_MODEL)), mod_spec(0), mod_spec(1),
            pl.BlockSpec((None, D_MODEL, per_load * TN), lambda s: (l, 0, jnp.minimum(s, n_loads - 1))),
            const((None, 2, 1, TN)), const((None, 2, 1, KV_W)),
            pl.BlockSpec((DEC_SEQ, LANES), lambda s: (0, 0)),
            pl.BlockSpec((DEC_SEQ, LANES), lambda s: (0, 0)),
            const((None, 3, CONV_W)), const((None, len(POOL_SIZES), LANES, LANES)),
            const((None, 1, POOL_W))],
        out_specs=out_specs,
        out_shape=out_shape,
        scratch_shapes=[pltpu.VMEM((PJ_TILES, D_MODEL, TN), BF16),
                        pltpu.VMEM((CHUNK, D_MODEL), BF16)],
        compiler_params=_params(),
        name="proj",
    )(*xs, g_mix.reshape(DEPTH, 1, D_MODEL), modc, modc, w_in, gq2, gk2, cos, sin,
      conv_w, pool_w, pool_scale.reshape(DEPTH, 1, POOL_W))


TQ = 256
N_QT = DEC_SEQ // TQ
WIN_KEYS = TQ + 2 * WINDOW
PAD_SEQ = WINDOW + DEC_SEQ + WINDOW


def _split_lanes(x):
    xr = pltpu.roll(x, HEAD_DIM, axis=1)
    low = lax.broadcasted_iota(jnp.int32, x.shape, 1) < HEAD_DIM
    return (jnp.where(low, x, 0.0).astype(BF16), jnp.where(low, 0.0, xr).astype(BF16),
            jnp.where(low, xr, 0.0).astype(BF16), jnp.where(low, 0.0, x).astype(BF16))


def _split_rows(xt):
    z = jnp.zeros((HEAD_DIM, xt.shape[1]), F32)
    h0, h1 = xt[:HEAD_DIM], xt[HEAD_DIM:]
    return tuple(jnp.concatenate(p, axis=0).astype(BF16) for p in ((h0, z), (z, h0), (h1, z), (z, h1)))


def _attend_pairs(tasks):
    units = [(t, parity) for t in range(len(tasks)) for parity in range(2)]

    def scores_of(unit):
        t, parity = unit
        q, pieces, _, _ = tasks[t]
        out = []
        for k_lo, k_hi, k_t, _, _, _, mask in pieces:
            k = k_hi if parity else k_lo
            s = _dot(q, k) if k_t else _dot_nt(q, k)
            if mask is not None:
                s = jnp.where(mask, s, NEG)
            out.append(s)
        return out

    scores = scores_of(units[0])
    even = None
    for i, (t, parity) in enumerate(units):
        nxt = scores_of(units[i + 1]) if i + 1 < len(units) else None
        _, pieces, sinks, write = tasks[t]
        m = functools.reduce(jnp.maximum, [jnp.max(s, axis=1, keepdims=True) for s in scores])
        if sinks is not None:
            m = jnp.maximum(m, sinks[parity])
        denom = None
        acc = None
        for s, (_, _, _, v_lo, v_hi, v_t, _) in zip(scores, pieces):
            p = jnp.exp(s - m)
            d = jnp.sum(p, axis=1, keepdims=True)
            v = v_hi if parity else v_lo
            a = _dot_nt(p.astype(BF16), v) if v_t else _dot(p.astype(BF16), v)
            denom = d if denom is None else denom + d
            acc = a if acc is None else acc + a
        if sinks is not None:
            denom = denom + jnp.exp(sinks[parity] - m)
        acc = acc / denom
        if parity == 0:
            even = acc
        else:
            write(even + acc)
        scores = nxt


def _attn_prompt_seq(rows, sink_ref, q_ref, kv_ref, o_ref, layer):
    def writer(col):
        def write(out):
            o_ref[rows, col:col + LANES] = out.astype(BF16)
        return write

    tasks = []
    for mixer in range(2):
        k = _split_lanes(kv_ref[rows, 2 * mixer * KV_W:(2 * mixer + 1) * KV_W].astype(F32))
        v = _split_lanes(kv_ref[rows, (2 * mixer + 1) * KV_W:(2 * mixer + 2) * KV_W].astype(F32))
        for pair in range(N_Q_HEADS // 2):
            lo = 2 * (pair // 2)
            piece = (k[lo], k[lo + 1], False, v[lo], v[lo + 1], False, None)
            sinks = None
            if mixer == 1:
                sinks = (sink_ref[layer, 2 * pair], sink_ref[layer, 2 * pair + 1])
            col = mixer * ATT_W + pair * LANES
            tasks.append((q_ref[rows, col:col + LANES], [piece], sinks, writer(col)))
    _attend_pairs(tasks)


def _attn_latent_fill(kv_ref, cka_ref, cva_ref, ckc_ref, cvc_ref, ctx_s, ka_s, va_s, kc_s, vc_s):
    for i, ref in enumerate((cka_ref, cva_ref, ckc_ref, cvc_ref)):
        for j, part in enumerate(_split_rows(ref[...])):
            ctx_s[4 * i + j] = part
    for dst, col in ((ka_s, 0), (va_s, KV_W)):
        for j, part in enumerate(_split_lanes(kv_ref[:, col:col + KV_W].astype(F32))):
            dst[j] = part
    zeros = jnp.zeros((WINDOW, LANES), BF16)
    for dst, col in ((kc_s, 2 * KV_W), (vc_s, 3 * KV_W)):
        for j, part in enumerate(_split_lanes(kv_ref[:, col:col + KV_W].astype(F32))):
            dst[j, 0:WINDOW, :] = zeros
            dst[j, WINDOW:WINDOW + DEC_SEQ, :] = part
            dst[j, WINDOW + DEC_SEQ:PAD_SEQ, :] = zeros


def _attn_latent_tile(qt, sink_ref, q_ref, o_ref, ctx_s, ka_s, va_s, kc_s, vc_s, layer):
    q0 = qt * TQ
    rows = pl.ds(pl.multiple_of(q0, TQ), TQ)
    win = pl.ds(pl.multiple_of(q0, TQ), WIN_KEYS)
    r = lax.broadcasted_iota(jnp.int32, (TQ, WIN_KEYS), 0)
    jk = lax.broadcasted_iota(jnp.int32, (TQ, WIN_KEYS), 1)
    kpos = q0 - WINDOW + jk
    band = jnp.where((jk - r >= 0) & (jk - r <= 2 * WINDOW), kpos, -1)
    mask = (band >= 0) & (band < DEC_SEQ)

    def writer(col):
        def write(out):
            o_ref[rows, col:col + LANES] = out.astype(BF16)
        return write

    tasks = []
    for pair in range(N_Q_HEADS // 2):
        lo = 2 * (pair // 2)
        hi = lo + 1
        col = pair * LANES
        ctx = (ctx_s[lo], ctx_s[hi], True, ctx_s[4 + lo], ctx_s[4 + hi], True, None)
        cur = (ka_s[lo], ka_s[hi], False, va_s[lo], va_s[hi], False, None)
        tasks.append((q_ref[rows, col:col + LANES], [ctx, cur], None, writer(col)))

        col = ATT_W + pair * LANES
        ctx = (ctx_s[8 + lo], ctx_s[8 + hi], True, ctx_s[12 + lo], ctx_s[12 + hi], True, None)
        near = (kc_s[lo, win, :], kc_s[hi, win, :], False, vc_s[lo, win, :], vc_s[hi, win, :], False, mask)
        sinks = (sink_ref[layer, 2 * pair], sink_ref[layer, 2 * pair + 1])
        tasks.append((q_ref[rows, col:col + LANES], [ctx, near], sinks, writer(col)))
    _attend_pairs(tasks)


def _attn_kernel(sink_ref, q_ref, kv_ref, cka_ref, cva_ref, ckc_ref, cvc_ref, o_ref,
                 ctx_s, ka_s, va_s, kc_s, vc_s, *, layer):
    s = pl.program_id(0)

    @pl.when(s < N_P_CHUNKS)
    def _prompt():
        def seq(b, carry):
            _attn_prompt_seq(pl.ds(pl.multiple_of(b * SEQ, SEQ), SEQ), sink_ref, q_ref, kv_ref, o_ref, layer)
            return carry
        lax.fori_loop(0, SEQ_PER_CHUNK, seq, 0)

    @pl.when(s >= N_P_CHUNKS)
    def _latent():
        _attn_latent_fill(kv_ref, cka_ref, cva_ref, ckc_ref, cvc_ref, ctx_s, ka_s, va_s, kc_s, vc_s)

        def tile(qt, carry):
            _attn_latent_tile(qt, sink_ref, q_ref, o_ref, ctx_s, ka_s, va_s, kc_s, vc_s, layer)
            return carry
        lax.fori_loop(0, N_QT, tile, 0)


def _attention(sink, q, kv, cka, cva, ckc, cvc, l):
    cache_spec = pl.BlockSpec((None, None, KV_W, PAST_LEN),
                              lambda s: (jnp.maximum(s - N_P_CHUNKS, 0), l, 0, 0))
    return pl.pallas_call(
        functools.partial(_attn_kernel, layer=l),
        grid=(N_CHUNKS,),
        in_specs=[pl.BlockSpec(memory_space=pltpu.SMEM),
                  pl.BlockSpec((CHUNK, 2 * ATT_W), lambda s: (s, 0)),
                  pl.BlockSpec((CHUNK, 4 * KV_W), lambda s: (s, 0)),
                  cache_spec, cache_spec, cache_spec, cache_spec],
        out_specs=pl.BlockSpec((CHUNK, 2 * ATT_W), lambda s: (s, 0)),
        out_shape=jax.ShapeDtypeStruct((ROWS, 2 * ATT_W), BF16),
        scratch_shapes=[pltpu.VMEM((16, KV_W, PAST_LEN), BF16),
                        pltpu.VMEM((4, DEC_SEQ, LANES), BF16), pltpu.VMEM((4, DEC_SEQ, LANES), BF16),
                        pltpu.VMEM((4, PAD_SEQ, LANES), BF16), pltpu.VMEM((4, PAD_SEQ, LANES), BF16)],
        compiler_params=_params(),
        name="attention",
    )(sink, q, kv, cka, cva, ckc, cvc)


MG_ROWS = 512
MG_CHUNKS = ROWS // MG_ROWS
MG_TILES = D_MODEL // TN
MG_STEPS = MG_TILES + MG_CHUNKS


def _mixer_out_kernel(x_ref, xn_ref, g_ref, sh_ref, sc_ref, gate_ref, att_ref, conv_ref, pool_ref,
                      wg0, wg1, wg2, wg3, wb0, wb1, wb2, wb3, wo_ref, o_ref,
                      wg_s, wb_s, wo_s, h_s, mix_s):
    s = pl.program_id(0)

    def norm():
        h_s[...] = _norm_modulate(xn_ref[...], g_ref[...], sc_ref[...], sh_ref[...])

    @pl.when(s == 0)
    def _first_norm():
        norm()

    def mix_tile(wgs, wbs):
        h = h_s[...]
        branches = (att_ref[:, :ATT_W], conv_ref[...], att_ref[:, ATT_W:], pool_ref[...])
        acc = None
        for br, wg, wb in zip(branches, wgs, wbs):
            term = _sigmoid(_dot(h, wg)) * _dot(br, wb)
            acc = term if acc is None else acc + term
        return acc.astype(BF16)

    def out():
        o_ref[...] = x_ref[...] + gate_ref[...] * _dot(mix_s[...], wo_s[...])

    @pl.when(s < MG_TILES)
    def _stream():
        wgs = [r[...].astype(BF16) for r in (wg0, wg1, wg2, wg3)]
        wbs = [r[...].astype(BF16) for r in (wb0, wb1, wb2, wb3)]
        for k in range(N_BRANCH):
            wg_s[s * N_BRANCH + k] = wgs[k]
            wb_s[s * N_BRANCH + k] = wbs[k]
        wo_s[pl.ds(pl.multiple_of(s * TN, TN), TN), :] = wo_ref[...].astype(BF16)
        v = mix_tile(wgs, wbs)
        for n in range(MG_TILES):
            @pl.when(s == n)
            def _store(n=n):
                mix_s[:, n * TN:(n + 1) * TN] = v

    @pl.when((s >= MG_TILES) & (s < MG_STEPS - 1))
    def _chunk():
        out()
        norm()
        for n in range(MG_TILES):
            mix_s[:, n * TN:(n + 1) * TN] = mix_tile(
                [wg_s[n * N_BRANCH + k] for k in range(N_BRANCH)],
                [wb_s[n * N_BRANCH + k] for k in range(N_BRANCH)])

    @pl.when(s == MG_STEPS - 1)
    def _last_chunk_out():
        out()


def _mixer_out(x, g_mix, modc, att, conv, pool, w_in, w_branch, w_out, l):
    def chunk(s):
        return jnp.maximum(s - MG_TILES, 0)

    def mix_chunk(s):
        return jnp.clip(s - (MG_TILES - 1), 0, MG_CHUNKS - 1)

    def tile(s):
        return jnp.minimum(s, MG_TILES - 1)

    def mod_spec(k, which):
        return pl.BlockSpec((None, 1, D_MODEL), lambda s: (which(s) * MG_ROWS // CHUNK, 0, k))

    def rows_in(width):
        return pl.BlockSpec((MG_ROWS, width), lambda s: (mix_chunk(s), 0))

    gate_specs = [pl.BlockSpec((None, D_MODEL, TN), functools.partial(
        lambda s, k: (l, 0, COL_GATE + MG_TILES * k + tile(s)), k=k)) for k in range(N_BRANCH)]
    br_specs = [pl.BlockSpec((None, None, BRANCH_W, TN), functools.partial(
        lambda s, k: (l, k, 0, tile(s)), k=k)) for k in range(N_BRANCH)]
    return pl.pallas_call(
        _mixer_out_kernel,
        grid=(MG_STEPS,),
        in_specs=[pl.BlockSpec((MG_ROWS, D_MODEL), lambda s: (chunk(s), 0)),
                  pl.BlockSpec((MG_ROWS, D_MODEL), lambda s: (mix_chunk(s), 0)),
                  pl.BlockSpec((None, 1, D_MODEL), lambda s: (l, 0, 0)),
                  mod_spec(0, mix_chunk), mod_spec(1, mix_chunk), mod_spec(2, chunk),
                  rows_in(2 * ATT_W), rows_in(CONV_W), rows_in(POOL_W)] + gate_specs + br_specs
                 + [pl.BlockSpec((None, TN, D_MODEL), lambda s: (l, tile(s), 0))],
        out_specs=pl.BlockSpec((MG_ROWS, D_MODEL), lambda s: (chunk(s), 0)),
        out_shape=jax.ShapeDtypeStruct((ROWS, D_MODEL), F32),
        scratch_shapes=[pltpu.VMEM((MG_TILES * N_BRANCH, D_MODEL, TN), BF16),
                        pltpu.VMEM((MG_TILES * N_BRANCH, BRANCH_W, TN), BF16),
                        pltpu.VMEM((D_MODEL, D_MODEL), BF16),
                        pltpu.VMEM((MG_ROWS, D_MODEL), BF16),
                        pltpu.VMEM((MG_ROWS, D_MODEL), BF16)],
        compiler_params=_params(),
        name="mixer_out",
    )(x, x, g_mix.reshape(DEPTH, 1, D_MODEL), modc, modc, modc, att, conv, pool,
      *([w_in] * N_BRANCH), *([w_branch] * N_BRANCH), w_out)


FF_ROWS = 512
FF_CHUNKS = ROWS // FF_ROWS
FF_P_CHUNKS = P_ROWS // FF_ROWS
FF_TILES = D_FF // TN
FF_STEPS = FF_TILES + FF_CHUNKS
FF_FIRST_LATENT_STEP = FF_TILES + FF_P_CHUNKS
assert MOD_TILES <= FF_CHUNKS - 1
FF_REGROUP_EARLY = BATCH - FF_CHUNKS
assert 0 <= FF_REGROUP_EARLY <= FF_TILES


def _ffn_kernel(x_ref, xn_ref, g_ref, sh_ref, sc_ref, gate_ref, wa_ref, wb_ref, wd_ref, *rest, last):
    if last:
        new_refs = rest[:DEPTH]
        op_ref, os_ref = rest[DEPTH:DEPTH + 2]
        kind_refs = rest[DEPTH + 2:DEPTH + 6]
        wa_s, wb_s, wd_s, h_s, act_s = rest[DEPTH + 6:]
    else:
        cond_ref, wm_ref, bm_ref, op_ref, mod_ref, wa_s, wb_s, wd_s, h_s, act_s = rest
        os_ref = op_ref
    s = pl.program_id(0)

    def norm():
        h_s[...] = _norm_modulate(xn_ref[...], g_ref[...], sc_ref[...], sh_ref[...])

    def up(wa, wb):
        h = h_s[...]
        a = _dot(h, wa)
        return (a * _sigmoid(a) * _dot(h, wb)).astype(BF16)

    def down():
        return x_ref[...] + gate_ref[...] * _dot(act_s[...], wd_s[...])

    def regroup_new_cache():
        for kind, dst in enumerate(kind_refs):
            for depth, src in enumerate(new_refs):
                dst[0, depth] = src[0, kind]

    def store(out):
        if last:
            @pl.when(s < FF_FIRST_LATENT_STEP)
            def _store_prompt():
                op_ref[...] = out

            @pl.when(s >= FF_FIRST_LATENT_STEP)
            def _store_latent():
                os_ref[...] = out
        else:
            op_ref[...] = out

    @pl.when(s == 0)
    def _first_norm():
        norm()

    @pl.when(s < FF_TILES)
    def _stream():
        wa = wa_ref[...].astype(BF16)
        wb = wb_ref[...].astype(BF16)
        wa_s[s] = wa
        wb_s[s] = wb
        wd_s[pl.ds(pl.multiple_of(s * TN, TN), TN), :] = wd_ref[...].astype(BF16)
        v = up(wa, wb)
        for j in range(FF_TILES):
            @pl.when(s == j)
            def _store(j=j):
                act_s[:, j * TN:(j + 1) * TN] = v

    @pl.when((s >= FF_TILES) & (s < FF_STEPS - 1))
    def _chunk():
        out = down()
        norm()
        store(out)
        for j in range(FF_TILES):
            act_s[:, j * TN:(j + 1) * TN] = up(wa_s[j], wb_s[j])
        if last:
            regroup_new_cache()
        else:
            _mod_tile(cond_ref, wm_ref, bm_ref, mod_ref)

    @pl.when(s == FF_STEPS - 1)
    def _last_chunk_down():
        store(down())
        if last:
            regroup_new_cache()

    if last:
        @pl.when(s < FF_REGROUP_EARLY)
        def _regroup_early():
            regroup_new_cache()


def _ffn(x, g_ffn, modc, w_gate_up, w_down, l, cond8=None, w_mod=None, b_mod=None, new_kv=None):
    last = cond8 is None

    def chunk(s):
        return jnp.maximum(s - FF_TILES, 0)

    def up_chunk(s):
        return jnp.clip(s - (FF_TILES - 1), 0, FF_CHUNKS - 1)

    def tile(s):
        return jnp.minimum(s, FF_TILES - 1)

    def mod_tile(s):
        return jnp.clip(s - FF_TILES, 0, MOD_TILES - 1)

    def mod_spec(k, which):
        return pl.BlockSpec((None, 1, D_MODEL), lambda s: (which(s) * FF_ROWS // CHUNK, 0, k))

    in_specs = [pl.BlockSpec((FF_ROWS, D_MODEL), lambda s: (chunk(s), 0)),
                pl.BlockSpec((FF_ROWS, D_MODEL), lambda s: (up_chunk(s), 0)),
                pl.BlockSpec((None, 1, D_MODEL), lambda s: (l, 0, 0)),
                mod_spec(3, up_chunk), mod_spec(4, up_chunk), mod_spec(5, chunk),
                pl.BlockSpec((None, D_MODEL, TN), lambda s: (l, 0, tile(s))),
                pl.BlockSpec((None, D_MODEL, TN), lambda s: (l, 0, FF_TILES + tile(s))),
                pl.BlockSpec((None, TN, D_MODEL), lambda s: (l, tile(s), 0))]
    args = [x, x, g_ffn.reshape(DEPTH, 1, D_MODEL), modc, modc, modc, w_gate_up, w_gate_up, w_down]
    if last:
        def seq(s):
            return jnp.minimum(s, FF_REGROUP_EARLY - 1) + jnp.maximum(s - (FF_TILES - 1), 0)

        in_specs += [pl.BlockSpec((1, 4, KV_W, SEQ), lambda s: (seq(s), 0, 0, 0))] * DEPTH
        args += list(new_kv)
        out_specs = [pl.BlockSpec((FF_ROWS, D_MODEL), lambda s: (jnp.minimum(chunk(s), FF_P_CHUNKS - 1), 0)),
                     pl.BlockSpec((FF_ROWS, D_MODEL), lambda s: (jnp.maximum(chunk(s) - FF_P_CHUNKS, 0), 0))]
        out_shape = [jax.ShapeDtypeStruct((P_ROWS, D_MODEL), F32), jax.ShapeDtypeStruct((S_ROWS, D_MODEL), F32)]
        out_specs += [pl.BlockSpec((1, DEPTH, KV_W, SEQ), lambda s: (seq(s), 0, 0, 0))] * 4
        out_shape += [jax.ShapeDtypeStruct((BATCH, DEPTH, KV_W, SEQ), F32)] * 4
    else:
        in_specs += [pl.BlockSpec((COND_ROWS, D_MODEL), lambda s: (0, 0)),
                     pl.BlockSpec((None, D_MODEL, MOD_TN), lambda s: (l + 1, 0, mod_tile(s))),
                     pl.BlockSpec((None, 1, MOD_TN), lambda s: (l + 1, 0, mod_tile(s)))]
        args += [cond8, w_mod, b_mod.reshape(DEPTH, 1, N_MOD)]
        out_specs = [pl.BlockSpec((FF_ROWS, D_MODEL), lambda s: (chunk(s), 0)),
                     pl.BlockSpec((COND_ROWS, MOD_TN), lambda s: (0, mod_tile(s)))]
        out_shape = [jax.ShapeDtypeStruct((ROWS, D_MODEL), F32), jax.ShapeDtypeStruct((COND_ROWS, N_MOD), F32)]
    return pl.pallas_call(
        functools.partial(_ffn_kernel, last=last),
        grid=(FF_STEPS,),
        in_specs=in_specs,
        out_specs=out_specs,
        out_shape=out_shape,
        scratch_shapes=[pltpu.VMEM((FF_TILES, D_MODEL, TN), BF16),
                        pltpu.VMEM((FF_TILES, D_MODEL, TN), BF16),
                        pltpu.VMEM((D_FF, D_MODEL), BF16),
                        pltpu.VMEM((FF_ROWS, D_MODEL), BF16),
                        pltpu.VMEM((FF_ROWS, D_FF), BF16)],
        compiler_params=_params(),
        name="ffn",
    )(*args)


def _rope_tables():
    rows = DEC_SEQ // GRID_W
    row = np.repeat(np.arange(rows, dtype=np.float32), GRID_W)
    col = np.tile(np.arange(GRID_W, dtype=np.float32), rows)
    inv = (1.0 / (np.float32(ROPE_THETA) ** (np.arange(N_FREQ, dtype=np.float32) / N_FREQ))).astype(np.float32)
    ang_r, ang_c = row[:, None] * inv, col[:, None] * inv
    cr, sr, cc, sc = np.cos(ang_r), np.sin(ang_r), np.cos(ang_c), np.sin(ang_c)
    cos = np.concatenate([cr, cr, cc, cc], axis=1)
    sin = np.concatenate([-sr, sr, -sc, sc], axis=1)
    reps = LANES // HEAD_DIM
    return (jnp.asarray(np.tile(cos, (1, reps)), F32), jnp.asarray(np.tile(sin, (1, reps)), F32))


def kernel(x_prompt, x_sample, cache_k_attn, cache_v_attn, cache_k_win, cache_v_win, c, c_ctx,
           w_mod, b_mod, g_mix, g_ffn, w_in, gq_attn, gk_attn, gq_win, gk_win, sink_win,
           conv_w, pool_w, pool_scale, w_branch, w_out, w_gate_up, w_down):
    cond8 = jnp.zeros((COND_ROWS, D_MODEL), F32).at[0].set(c_ctx).at[1:1 + DEC_BATCH].set(c)
    chunk_cond = np.array([0] * N_P_CHUNKS + [1 + b for b in range(DEC_BATCH)])

    def per_chunk(mod):
        return mod[chunk_cond].reshape(N_CHUNKS, 1, N_MOD)

    modc = per_chunk(_modulation(cond8, w_mod, b_mod, 0))

    cos, sin = _rope_tables()
    gq2 = jnp.stack([jnp.tile(gq_attn, (1, TN // HEAD_DIM)), jnp.tile(gq_win, (1, TN // HEAD_DIM))],
                    axis=1).reshape(DEPTH, 2, 1, TN)
    gk2 = jnp.stack([jnp.tile(gk_attn, (1, KV_W // HEAD_DIM)), jnp.tile(gk_win, (1, KV_W // HEAD_DIM))],
                    axis=1).reshape(DEPTH, 2, 1, KV_W)
    caches = [a.transpose(0, 1, 3, 4, 2).reshape(DEC_BATCH, DEPTH, KV_W, PAST_LEN)
              for a in (cache_k_attn, cache_v_attn, cache_k_win, cache_v_win)]

    xs = (x_prompt.reshape(P_ROWS, D_MODEL), x_sample.reshape(S_ROWS, D_MODEL))
    new_kv = []
    for l in range(DEPTH):
        q, kv, conv, pool, new, *stacked = _proj(xs, g_mix, modc, w_in, gq2, gk2, cos, sin,
                                                 conv_w, pool_w, pool_scale, l)
        x = stacked[0] if stacked else xs[0]
        new_kv.append(new)
        att = _attention(sink_win, q, kv, *caches, l)
        x = _mixer_out(x, g_mix, modc, att, conv, pool, w_in, w_branch, w_out, l)
        if l + 1 < DEPTH:
            x, mod_next = _ffn(x, g_ffn, modc, w_gate_up, w_down, l, cond8, w_mod, b_mod)
            modc = per_chunk(mod_next)
            xs = (x,)
        else:
            y_prompt, y_sample, *by_kind = _ffn(x, g_ffn, modc, w_gate_up, w_down, l, new_kv=new_kv)

    outs = [a.reshape(BATCH, DEPTH, N_KV_HEADS, HEAD_DIM, SEQ).transpose(0, 1, 4, 2, 3) for a in by_kind]
    return (y_prompt.reshape(BATCH, SEQ, D_MODEL), y_sample.reshape(DEC_BATCH, DEC_SEQ, D_MODEL), *outs)
```

```python
import functools

import numpy as np
import jax
import jax.numpy as jnp
from jax import lax
from jax.experimental import pallas as pl
from jax.experimental.pallas import tpu as pltpu

D_MODEL = 1024
BATCH = 16
SEQ = 256
DEPTH = 4
DEC_BATCH = 2
DEC_SEQ = 1024
PAST_LEN = 512
GRID_W = 64
HEAD_DIM = 64
N_Q_HEADS = 8
N_KV_HEADS = 2
ATT_W = N_Q_HEADS * HEAD_DIM
KV_W = N_KV_HEADS * HEAD_DIM
N_FREQ = HEAD_DIM // 4
ROPE_THETA = 10000.0
CONV_W = 512
POOL_W = 512
POOL_SIZES = (2, 4, 8, 16)
N_BRANCH = 4
BRANCH_W = 512
D_FF = 2816
WINDOW = 128
EPS = 1e-6
NEG = -1e30

P_ROWS = BATCH * SEQ
S_ROWS = DEC_BATCH * DEC_SEQ
ROWS = P_ROWS + S_ROWS
CHUNK = 1024
N_CHUNKS = ROWS // CHUNK
N_P_CHUNKS = P_ROWS // CHUNK
SEQ_PER_CHUNK = CHUNK // SEQ
TN = 256
LANES = 128
VMEM_LIMIT = 58 * 1024 * 1024

COL_QA, COL_KVA, COL_QC, COL_KVC = 0, 2, 3, 5
COL_U, COL_GB, COL_GC, COL_PV, COL_GATE = 6, 8, 10, 12, 14

F32 = jnp.float32
BF16 = jnp.bfloat16


def _params():
    return pltpu.CompilerParams(dimension_semantics=("arbitrary",), vmem_limit_bytes=VMEM_LIMIT)


def _dot(a, b):
    return jnp.dot(a, b, preferred_element_type=F32)


def _dot_nt(a, b):
    return lax.dot_general(a, b, (((1,), (1,)), ((), ())), preferred_element_type=F32)


def _sigmoid(x):
    return 1.0 / (1.0 + jnp.exp(-x))


def _norm_modulate(x, g, scale, shift):
    y = x * lax.rsqrt(jnp.mean(x * x, axis=-1, keepdims=True) + EPS)
    return ((y * g) * (1.0 + scale) + shift).astype(BF16)


def _head_segments(width):
    r = lax.broadcasted_iota(jnp.int32, (width, width), 0) // HEAD_DIM
    c = lax.broadcasted_iota(jnp.int32, (width, width), 1) // HEAD_DIM
    return jnp.where(r == c, 1.0 / HEAD_DIM, 0.0).astype(BF16)


def _head_mean_square(y, seg):
    return _dot((y * y).astype(BF16), seg)


def _swap16(x):
    lane = lax.broadcasted_iota(jnp.int32, x.shape, 1)
    fwd = pltpu.roll(x, LANES - 16, axis=1)
    bwd = pltpu.roll(x, 16, axis=1)
    return jnp.where((lane & 16) == 0, fwd, bwd)


def _rope128(x, cos, sin):
    return x * cos + _swap16(x) * sin


N_MOD = 6 * D_MODEL
COND_ROWS = 8
MOD_TN = 768
MOD_TILES = N_MOD // MOD_TN


def _cond_row(chunk):
    return jnp.maximum(chunk - (N_P_CHUNKS - 1), 0)


def _mod_tile(cond_ref, w_ref, b_ref, o_ref):
    c = cond_ref[...]
    s = (c * _sigmoid(c)).astype(BF16)
    res = _dot(s, w_ref[...].astype(BF16)) + b_ref[...]
    for r in range(COND_ROWS):
        o_ref[r] = res[r:r + 1, :]


def _modulation(cond8, w_mod, b_mod, l):
    tn = 2 * MOD_TN
    return pl.pallas_call(
        _mod_tile,
        grid=(N_MOD // tn,),
        in_specs=[pl.BlockSpec((COND_ROWS, D_MODEL), lambda j: (0, 0)),
                  pl.BlockSpec((None, D_MODEL, tn), lambda j: (l, 0, j)),
                  pl.BlockSpec((None, 1, tn), lambda j: (l, 0, j))],
        out_specs=pl.BlockSpec((COND_ROWS, 1, tn), lambda j: (0, 0, j)),
        out_shape=jax.ShapeDtypeStruct((COND_ROWS, 1, N_MOD), F32),
        compiler_params=_params(),
        name="modulation",
    )(cond8, w_mod, b_mod.reshape(DEPTH, 1, N_MOD))


PJ_TILES = COL_GATE


def _proj_loads(first):
    per_load = 1 if first else 2
    return per_load, PJ_TILES // per_load


def _seq_pos(latent, width):
    row = lax.broadcasted_iota(jnp.int32, (CHUNK, width), 0)
    return row if latent else row & (SEQ - 1)


def _shift_rows(x, k, pos, seq_len):
    rolled = pltpu.roll(x, (-k) % CHUNK, axis=0)
    ok = (pos + k >= 0) & (pos + k < seq_len)
    return jnp.where(ok, rolled, 0.0)


def _window_mean_minus_token(p, half, pos, seq_len):
    fwd = p
    bwd = _shift_rows(p, -1, pos, seq_len)
    m = 1
    while m < half:
        fwd = fwd + _shift_rows(fwd, m, pos, seq_len)
        bwd = bwd + _shift_rows(bwd, -m, pos, seq_len)
        m *= 2
    count = jnp.minimum(pos + half, seq_len) - jnp.maximum(pos - half, 0)
    return (fwd + bwd) / count.astype(F32) - p


def _proj_chunk(latent, x_ref, g_ref, sh_ref, sc_ref, gain_ref, cos_ref, sin_ref, cw_ref, pw_ref,
                ps_ref, q_ref, kv_ref, conv_ref, pool_ref, new_ref, w_s, h_s):
    h_s[...] = _norm_modulate(x_ref[...], g_ref[...], sc_ref[...], sh_ref[...])
    seq_len = DEC_SEQ if latent else SEQ
    seg_q = _head_segments(TN)
    seg_k = _head_segments(KV_W)
    pos_wide = _seq_pos(latent, TN)
    pos = _seq_pos(latent, LANES)

    def q_tile(mixer, j, y):
        y = y * lax.rsqrt(_head_mean_square(y, seg_q) + EPS) * (gain_ref[mixer] * (HEAD_DIM ** -0.5))
        for s in range(TN // LANES):
            part = y[:, s * LANES:(s + 1) * LANES]
            if latent:
                part = _rope128(part, cos_ref[...], sin_ref[...])
            out_col = mixer * ATT_W + j * TN + s * LANES
            q_ref[:, out_col:out_col + LANES] = part.astype(BF16)

    def kv_tile(mixer, y):
        k = y[:, :KV_W]
        v = y[:, KV_W:]
        k = k * lax.rsqrt(_head_mean_square(k, seg_k) + EPS) * gain_ref[2 + mixer][:, :KV_W]
        if latent:
            k = _rope128(k, cos_ref[...], sin_ref[...])
        else:
            for b in range(SEQ_PER_CHUNK):
                new_ref[b, 2 * mixer] = k[b * SEQ:(b + 1) * SEQ, :].T
                new_ref[b, 2 * mixer + 1] = v[b * SEQ:(b + 1) * SEQ, :].T
        kv_ref[:, 2 * mixer * KV_W:(2 * mixer + 1) * KV_W] = k.astype(BF16)
        kv_ref[:, (2 * mixer + 1) * KV_W:(2 * mixer + 2) * KV_W] = v.astype(BF16)

    def conv_tile(j, gc, u, gb):
        cols = slice(j * TN, (j + 1) * TN)
        z = gc * u
        y = (_shift_rows(z, -1, pos_wide, seq_len) * cw_ref[0:1, cols] + z * cw_ref[1:2, cols]
             + _shift_rows(z, 1, pos_wide, seq_len) * cw_ref[2:3, cols])
        conv_ref[:, cols] = (gb * y).astype(BF16)

    def pool_tile(j, p):
        for s in range(TN // LANES):
            grp = j * (TN // LANES) + s
            cols = slice(grp * LANES, (grp + 1) * LANES)
            y = _window_mean_minus_token(p[:, s * LANES:(s + 1) * LANES], POOL_SIZES[grp] // 2, pos, seq_len)
            pool_ref[:, cols] = (_dot(y.astype(BF16), pw_ref[grp].astype(BF16)) * ps_ref[:, cols]).astype(BF16)

    tasks = []
    for j in range(POOL_W // TN):
        tasks.append(((COL_PV + j,), functools.partial(pool_tile, j)))
        tasks.append(((COL_GC + j, COL_U + j, COL_GB + j), functools.partial(conv_tile, j)))
    for mixer, col in enumerate((COL_QA, COL_QC)):
        for j in range(ATT_W // TN):
            tasks.append(((col + j,), functools.partial(q_tile, mixer, j)))
    for mixer, col in enumerate((COL_KVA, COL_KVC)):
        tasks.append(((col,), functools.partial(kv_tile, mixer)))

    def matmuls(tiles):
        return [_dot(h_s[...], w_s[t]) for t in tiles]

    ys = matmuls(tasks[0][0])
    for i, (_, epilogue) in enumerate(tasks):
        nxt = matmuls(tasks[i + 1][0]) if i + 1 < len(tasks) else None
        epilogue(*ys)
        ys = nxt


def _proj_kernel(*refs, first):
    if first:
        xp_ref, xs_ref, *refs = refs
    else:
        xp_ref = xs_ref = refs[0]
        refs = refs[1:]
    (g_ref, sh_ref, sc_ref, w_ref, gain_ref, cos_ref, sin_ref, cw_ref, pw_ref, ps_ref,
     q_ref, kv_ref, conv_ref, pool_ref, new_ref, *refs) = refs
    xcat_ref = refs[0] if first else None
    w_s, h_s = refs[-2:]
    per_load, n_loads = _proj_loads(first)
    s = pl.program_id(0)
    rest = (g_ref, sh_ref, sc_ref, gain_ref, cos_ref, sin_ref, cw_ref, pw_ref, ps_ref,
            q_ref, kv_ref, conv_ref, pool_ref, new_ref, w_s, h_s)

    @pl.when(s < n_loads)
    def _stream():
        for j in range(per_load):
            w_s[per_load * s + j] = w_ref[:, j * TN:(j + 1) * TN].astype(BF16)

    @pl.when((s >= n_loads) & (s < n_loads + N_P_CHUNKS))
    def _prompt():
        if first:
            xcat_ref[...] = xp_ref[...]
        _proj_chunk(False, xp_ref, *rest)

    @pl.when(s >= n_loads + N_P_CHUNKS)
    def _latent():
        if first:
            xcat_ref[...] = xs_ref[...]
        _proj_chunk(True, xs_ref, *rest)


def _proj(xs, g_mix, modc, w_in, gains, cos, sin, conv_w, pool_w, pool_scale, l):
    first = len(xs) == 2
    per_load, n_loads = _proj_loads(first)

    def chunk(s):
        return jnp.clip(s - n_loads, 0, N_CHUNKS - 1)

    def p_chunk(s):
        return jnp.clip(s - n_loads, 0, N_P_CHUNKS - 1)

    def s_chunk(s):
        return jnp.clip(s - n_loads - N_P_CHUNKS, 0, N_CHUNKS - N_P_CHUNKS - 1)

    def mod_spec(k):
        return pl.BlockSpec((None, 1, D_MODEL), lambda s: (_cond_row(chunk(s)), 0, k))

    def const(shape):
        return pl.BlockSpec(shape, lambda s: (l,) + (0,) * (len(shape) - 1))

    def rows_out(width):
        return pl.BlockSpec((CHUNK, width), lambda s: (chunk(s), 0))

    if first:
        x_specs = [pl.BlockSpec((CHUNK, D_MODEL), lambda s: (p_chunk(s), 0)),
                   pl.BlockSpec((CHUNK, D_MODEL), lambda s: (s_chunk(s), 0))]
    else:
        x_specs = [rows_out(D_MODEL)]
    out_specs = [rows_out(2 * ATT_W), rows_out(4 * KV_W), rows_out(CONV_W), rows_out(POOL_W),
                 pl.BlockSpec((SEQ_PER_CHUNK, 4, KV_W, SEQ), lambda s: (p_chunk(s), 0, 0, 0))]
    out_shape = [jax.ShapeDtypeStruct((ROWS, 2 * ATT_W), BF16),
                 jax.ShapeDtypeStruct((ROWS, 4 * KV_W), BF16),
                 jax.ShapeDtypeStruct((ROWS, CONV_W), BF16),
                 jax.ShapeDtypeStruct((ROWS, POOL_W), BF16),
                 jax.ShapeDtypeStruct((BATCH, 4, KV_W, SEQ), F32)]
    if first:
        out_specs.append(rows_out(D_MODEL))
        out_shape.append(jax.ShapeDtypeStruct((ROWS, D_MODEL), F32))
    return pl.pallas_call(
        functools.partial(_proj_kernel, first=first),
        grid=(n_loads + N_CHUNKS,),
        in_specs=x_specs + [
            const((None, 1, D_MODEL)), mod_spec(0), mod_spec(1),
            pl.BlockSpec((None, D_MODEL, per_load * TN), lambda s: (l, 0, jnp.minimum(s, n_loads - 1))),
            const((None, 4, 1, TN)),
            pl.BlockSpec((DEC_SEQ, LANES), lambda s: (0, 0)),
            pl.BlockSpec((DEC_SEQ, LANES), lambda s: (0, 0)),
            const((None, 3, CONV_W)), const((None, len(POOL_SIZES), LANES, LANES)),
            const((None, 1, POOL_W))],
        out_specs=out_specs,
        out_shape=out_shape,
        scratch_shapes=[pltpu.VMEM((PJ_TILES, D_MODEL, TN), BF16),
                        pltpu.VMEM((CHUNK, D_MODEL), BF16)],
        compiler_params=_params(),
        name="proj",
    )(*xs, g_mix.reshape(DEPTH, 1, D_MODEL), modc, modc, w_in, gains, cos, sin,
      conv_w, pool_w, pool_scale.reshape(DEPTH, 1, POOL_W))


TQ = 256
N_QT = DEC_SEQ // TQ
WIN_KEYS = TQ + 2 * WINDOW
PAD_SEQ = WINDOW + DEC_SEQ + WINDOW


def _split_lanes(x):
    xr = pltpu.roll(x, HEAD_DIM, axis=1)
    low = lax.broadcasted_iota(jnp.int32, x.shape, 1) < HEAD_DIM
    return (jnp.where(low, x, 0.0).astype(BF16), jnp.where(low, 0.0, xr).astype(BF16),
            jnp.where(low, xr, 0.0).astype(BF16), jnp.where(low, 0.0, x).astype(BF16))


def _split_rows(xt):
    z = jnp.zeros((HEAD_DIM, xt.shape[1]), F32)
    h0, h1 = xt[:HEAD_DIM], xt[HEAD_DIM:]
    return tuple(jnp.concatenate(p, axis=0).astype(BF16) for p in ((h0, z), (z, h0), (h1, z), (z, h1)))


def _attend_pairs(tasks):
    units = [(t, parity) for t in range(len(tasks)) for parity in range(2)]

    def scores_of(unit):
        t, parity = unit
        q, pieces, _, _ = tasks[t]
        out = []
        for k_lo, k_hi, k_t, _, _, _, mask in pieces:
            k = k_hi if parity else k_lo
            s = _dot(q, k) if k_t else _dot_nt(q, k)
            if mask is not None:
                s = jnp.where(mask, s, NEG)
            out.append(s)
        return out

    scores = scores_of(units[0])
    even = None
    for i, (t, parity) in enumerate(units):
        nxt = scores_of(units[i + 1]) if i + 1 < len(units) else None
        _, pieces, sinks, write = tasks[t]
        m = functools.reduce(jnp.maximum, [jnp.max(s, axis=1, keepdims=True) for s in scores])
        if sinks is not None:
            m = jnp.maximum(m, sinks[parity])
        denom = None
        acc = None
        for s, (_, _, _, v_lo, v_hi, v_t, _) in zip(scores, pieces):
            p = jnp.exp(s - m)
            d = jnp.sum(p, axis=1, keepdims=True)
            v = v_hi if parity else v_lo
            a = _dot_nt(p.astype(BF16), v) if v_t else _dot(p.astype(BF16), v)
            denom = d if denom is None else denom + d
            acc = a if acc is None else acc + a
        if sinks is not None:
            denom = denom + jnp.exp(sinks[parity] - m)
        acc = acc / denom
        if parity == 0:
            even = acc
        else:
            write(even + acc)
        scores = nxt


def _attn_prompt_seq(rows, sink_ref, q_ref, kv_ref, o_ref, layer):
    def writer(col):
        def write(out):
            o_ref[rows, col:col + LANES] = out.astype(BF16)
        return write

    tasks = []
    for mixer in range(2):
        k = _split_lanes(kv_ref[rows, 2 * mixer * KV_W:(2 * mixer + 1) * KV_W].astype(F32))
        v = _split_lanes(kv_ref[rows, (2 * mixer + 1) * KV_W:(2 * mixer + 2) * KV_W].astype(F32))
        for pair in range(N_Q_HEADS // 2):
            lo = 2 * (pair // 2)
            piece = (k[lo], k[lo + 1], False, v[lo], v[lo + 1], False, None)
            sinks = None
            if mixer == 1:
                sinks = (sink_ref[layer, 2 * pair], sink_ref[layer, 2 * pair + 1])
            col = mixer * ATT_W + pair * LANES
            tasks.append((q_ref[rows, col:col + LANES], [piece], sinks, writer(col)))
    _attend_pairs(tasks)


def _attn_latent_fill(kv_ref, cka_ref, cva_ref, ckc_ref, cvc_ref, ctx_s, ka_s, va_s, kc_s, vc_s):
    for i, ref in enumerate((cka_ref, cva_ref, ckc_ref, cvc_ref)):
        for j, part in enumerate(_split_rows(ref[...])):
            ctx_s[4 * i + j] = part
    for dst, col in ((ka_s, 0), (va_s, KV_W)):
        for j, part in enumerate(_split_lanes(kv_ref[:, col:col + KV_W].astype(F32))):
            dst[j] = part
    zeros = jnp.zeros((WINDOW, LANES), BF16)
    for dst, col in ((kc_s, 2 * KV_W), (vc_s, 3 * KV_W)):
        for j, part in enumerate(_split_lanes(kv_ref[:, col:col + KV_W].astype(F32))):
            dst[j, 0:WINDOW, :] = zeros
            dst[j, WINDOW:WINDOW + DEC_SEQ, :] = part
            dst[j, WINDOW + DEC_SEQ:PAD_SEQ, :] = zeros


def _attn_latent_tile(qt, sink_ref, q_ref, o_ref, ctx_s, ka_s, va_s, kc_s, vc_s, layer):
    q0 = qt * TQ
    rows = pl.ds(pl.multiple_of(q0, TQ), TQ)
    win = pl.ds(pl.multiple_of(q0, TQ), WIN_KEYS)
    r = lax.broadcasted_iota(jnp.int32, (TQ, WIN_KEYS), 0)
    jk = lax.broadcasted_iota(jnp.int32, (TQ, WIN_KEYS), 1)
    kpos = q0 - WINDOW + jk
    band = jnp.where((jk - r >= 0) & (jk - r <= 2 * WINDOW), kpos, -1)
    mask = (band >= 0) & (band < DEC_SEQ)

    def writer(col):
        def write(out):
            o_ref[rows, col:col + LANES] = out.astype(BF16)
        return write

    tasks = []
    for pair in range(N_Q_HEADS // 2):
        lo = 2 * (pair // 2)
        hi = lo + 1
        col = pair * LANES
        ctx = (ctx_s[lo], ctx_s[hi], True, ctx_s[4 + lo], ctx_s[4 + hi], True, None)
        cur = (ka_s[lo], ka_s[hi], False, va_s[lo], va_s[hi], False, None)
        tasks.append((q_ref[rows, col:col + LANES], [ctx, cur], None, writer(col)))

        col = ATT_W + pair * LANES
        ctx = (ctx_s[8 + lo], ctx_s[8 + hi], True, ctx_s[12 + lo], ctx_s[12 + hi], True, None)
        near = (kc_s[lo, win, :], kc_s[hi, win, :], False, vc_s[lo, win, :], vc_s[hi, win, :], False, mask)
        sinks = (sink_ref[layer, 2 * pair], sink_ref[layer, 2 * pair + 1])
        tasks.append((q_ref[rows, col:col + LANES], [ctx, near], sinks, writer(col)))
    _attend_pairs(tasks)


def _attn_kernel(sink_ref, q_ref, kv_ref, cka_ref, cva_ref, ckc_ref, cvc_ref, o_ref,
                 ctx_s, ka_s, va_s, kc_s, vc_s, *, layer):
    s = pl.program_id(0)

    @pl.when(s < N_P_CHUNKS)
    def _prompt():
        def seq(b, carry):
            _attn_prompt_seq(pl.ds(pl.multiple_of(b * SEQ, SEQ), SEQ), sink_ref, q_ref, kv_ref, o_ref, layer)
            return carry
        lax.fori_loop(0, SEQ_PER_CHUNK, seq, 0)

    @pl.when(s >= N_P_CHUNKS)
    def _latent():
        _attn_latent_fill(kv_ref, cka_ref, cva_ref, ckc_ref, cvc_ref, ctx_s, ka_s, va_s, kc_s, vc_s)

        def tile(qt, carry):
            _attn_latent_tile(qt, sink_ref, q_ref, o_ref, ctx_s, ka_s, va_s, kc_s, vc_s, layer)
            return carry
        lax.fori_loop(0, N_QT, tile, 0)


def _attention(sink, q, kv, cka, cva, ckc, cvc, l):
    cache_spec = pl.BlockSpec((None, None, KV_W, PAST_LEN),
                              lambda s: (jnp.maximum(s - N_P_CHUNKS, 0), l, 0, 0))
    return pl.pallas_call(
        functools.partial(_attn_kernel, layer=l),
        grid=(N_CHUNKS,),
        in_specs=[pl.BlockSpec(memory_space=pltpu.SMEM),
                  pl.BlockSpec((CHUNK, 2 * ATT_W), lambda s: (s, 0)),
                  pl.BlockSpec((CHUNK, 4 * KV_W), lambda s: (s, 0)),
                  cache_spec, cache_spec, cache_spec, cache_spec],
        out_specs=pl.BlockSpec((CHUNK, 2 * ATT_W), lambda s: (s, 0)),
        out_shape=jax.ShapeDtypeStruct((ROWS, 2 * ATT_W), BF16),
        scratch_shapes=[pltpu.VMEM((16, KV_W, PAST_LEN), BF16),
                        pltpu.VMEM((4, DEC_SEQ, LANES), BF16), pltpu.VMEM((4, DEC_SEQ, LANES), BF16),
                        pltpu.VMEM((4, PAD_SEQ, LANES), BF16), pltpu.VMEM((4, PAD_SEQ, LANES), BF16)],
        compiler_params=_params(),
        name="attention",
    )(sink, q, kv, cka, cva, ckc, cvc)


MG_ROWS = 512
MG_CHUNKS = ROWS // MG_ROWS
MG_TILES = D_MODEL // TN
MG_STEPS = MG_TILES + MG_CHUNKS


def _mixer_out_kernel(x_ref, xn_ref, g_ref, sh_ref, sc_ref, gate_ref, att_ref, conv_ref, pool_ref,
                      wg0, wg1, wg2, wg3, wb0, wb1, wb2, wb3, wo_ref, o_ref,
                      wg_s, wb_s, wo_s, h_s, mix_s):
    s = pl.program_id(0)

    def norm():
        h_s[...] = _norm_modulate(xn_ref[...], g_ref[...], sc_ref[...], sh_ref[...])

    @pl.when(s == 0)
    def _first_norm():
        norm()

    def mix_tile(wgs, wbs):
        h = h_s[...]
        branches = (att_ref[:, :ATT_W], conv_ref[...], att_ref[:, ATT_W:], pool_ref[...])
        acc = None
        for br, wg, wb in zip(branches, wgs, wbs):
            term = _sigmoid(_dot(h, wg)) * _dot(br, wb)
            acc = term if acc is None else acc + term
        return acc.astype(BF16)

    def out():
        o_ref[...] = x_ref[...] + gate_ref[...] * _dot(mix_s[...], wo_s[...])

    @pl.when(s < MG_TILES)
    def _stream():
        wgs = [r[...].astype(BF16) for r in (wg0, wg1, wg2, wg3)]
        wbs = [r[...].astype(BF16) for r in (wb0, wb1, wb2, wb3)]
        for k in range(N_BRANCH):
            wg_s[s * N_BRANCH + k] = wgs[k]
            wb_s[s * N_BRANCH + k] = wbs[k]
        wo_s[pl.ds(pl.multiple_of(s * TN, TN), TN), :] = wo_ref[...].astype(BF16)
        v = mix_tile(wgs, wbs)
        for n in range(MG_TILES):
            @pl.when(s == n)
            def _store(n=n):
                mix_s[:, n * TN:(n + 1) * TN] = v

    @pl.when((s >= MG_TILES) & (s < MG_STEPS - 1))
    def _chunk():
        out()
        norm()
        for n in range(MG_TILES):
            mix_s[:, n * TN:(n + 1) * TN] = mix_tile(
                [wg_s[n * N_BRANCH + k] for k in range(N_BRANCH)],
                [wb_s[n * N_BRANCH + k] for k in range(N_BRANCH)])

    @pl.when(s == MG_STEPS - 1)
    def _last_chunk_out():
        out()


def _mixer_out(x, g_mix, modc, att, conv, pool, w_in, w_branch, w_out, l):
    def chunk(s):
        return jnp.maximum(s - MG_TILES, 0)

    def mix_chunk(s):
        return jnp.clip(s - (MG_TILES - 1), 0, MG_CHUNKS - 1)

    def tile(s):
        return jnp.minimum(s, MG_TILES - 1)

    def mod_spec(k, which):
        return pl.BlockSpec((None, 1, D_MODEL), lambda s: (_cond_row(which(s) * MG_ROWS // CHUNK), 0, k))

    def rows_in(width):
        return pl.BlockSpec((MG_ROWS, width), lambda s: (mix_chunk(s), 0))

    gate_specs = [pl.BlockSpec((None, D_MODEL, TN), functools.partial(
        lambda s, k: (l, 0, COL_GATE + MG_TILES * k + tile(s)), k=k)) for k in range(N_BRANCH)]
    br_specs = [pl.BlockSpec((None, None, BRANCH_W, TN), functools.partial(
        lambda s, k: (l, k, 0, tile(s)), k=k)) for k in range(N_BRANCH)]
    return pl.pallas_call(
        _mixer_out_kernel,
        grid=(MG_STEPS,),
        in_specs=[pl.BlockSpec((MG_ROWS, D_MODEL), lambda s: (chunk(s), 0)),
                  pl.BlockSpec((MG_ROWS, D_MODEL), lambda s: (mix_chunk(s), 0)),
                  pl.BlockSpec((None, 1, D_MODEL), lambda s: (l, 0, 0)),
                  mod_spec(0, mix_chunk), mod_spec(1, mix_chunk), mod_spec(2, chunk),
                  rows_in(2 * ATT_W), rows_in(CONV_W), rows_in(POOL_W)] + gate_specs + br_specs
                 + [pl.BlockSpec((None, TN, D_MODEL), lambda s: (l, tile(s), 0))],
        out_specs=pl.BlockSpec((MG_ROWS, D_MODEL), lambda s: (chunk(s), 0)),
        out_shape=jax.ShapeDtypeStruct((ROWS, D_MODEL), F32),
        scratch_shapes=[pltpu.VMEM((MG_TILES * N_BRANCH, D_MODEL, TN), BF16),
                        pltpu.VMEM((MG_TILES * N_BRANCH, BRANCH_W, TN), BF16),
                        pltpu.VMEM((D_MODEL, D_MODEL), BF16),
                        pltpu.VMEM((MG_ROWS, D_MODEL), BF16),
                        pltpu.VMEM((MG_ROWS, D_MODEL), BF16)],
        compiler_params=_params(),
        name="mixer_out",
    )(x, x, g_mix.reshape(DEPTH, 1, D_MODEL), modc, modc, modc, att, conv, pool,
      *([w_in] * N_BRANCH), *([w_branch] * N_BRANCH), w_out)


FF_ROWS = 512
FF_CHUNKS = ROWS // FF_ROWS
FF_P_CHUNKS = P_ROWS // FF_ROWS
FF_TILES = D_FF // TN
FF_STEPS = FF_TILES + FF_CHUNKS
FF_FIRST_LATENT_STEP = FF_TILES + FF_P_CHUNKS
assert MOD_TILES <= FF_CHUNKS - 1
FF_REGROUP_EARLY = BATCH - FF_CHUNKS
assert 0 <= FF_REGROUP_EARLY <= FF_TILES


def _ffn_kernel(x_ref, xn_ref, g_ref, sh_ref, sc_ref, gate_ref, wa_ref, wb_ref, wd_ref, *rest, last):
    if last:
        new_refs = rest[:DEPTH]
        op_ref, os_ref = rest[DEPTH:DEPTH + 2]
        kind_refs = rest[DEPTH + 2:DEPTH + 6]
        wa_s, wb_s, wd_s, h_s, act_s = rest[DEPTH + 6:]
    else:
        cond_ref, wm_ref, bm_ref, op_ref, mod_ref, wa_s, wb_s, wd_s, h_s, act_s = rest
        os_ref = op_ref
    s = pl.program_id(0)

    def norm():
        h_s[...] = _norm_modulate(xn_ref[...], g_ref[...], sc_ref[...], sh_ref[...])

    def up(wa, wb):
        h = h_s[...]
        a = _dot(h, wa)
        return (a * _sigmoid(a) * _dot(h, wb)).astype(BF16)

    def down():
        return x_ref[...] + gate_ref[...] * _dot(act_s[...], wd_s[...])

    def regroup_new_cache():
        for kind, dst in enumerate(kind_refs):
            for depth, src in enumerate(new_refs):
                dst[0, depth] = src[0, kind]

    def store(out):
        if last:
            @pl.when(s < FF_FIRST_LATENT_STEP)
            def _store_prompt():
                op_ref[...] = out

            @pl.when(s >= FF_FIRST_LATENT_STEP)
            def _store_latent():
                os_ref[...] = out
        else:
            op_ref[...] = out

    @pl.when(s == 0)
    def _first_norm():
        norm()

    @pl.when(s < FF_TILES)
    def _stream():
        wa = wa_ref[...].astype(BF16)
        wb = wb_ref[...].astype(BF16)
        wa_s[s] = wa
        wb_s[s] = wb
        wd_s[pl.ds(pl.multiple_of(s * TN, TN), TN), :] = wd_ref[...].astype(BF16)
        v = up(wa, wb)
        for j in range(FF_TILES):
            @pl.when(s == j)
            def _store(j=j):
                act_s[:, j * TN:(j + 1) * TN] = v

    @pl.when((s >= FF_TILES) & (s < FF_STEPS - 1))
    def _chunk():
        out = down()
        norm()
        store(out)
        for j in range(FF_TILES):
            act_s[:, j * TN:(j + 1) * TN] = up(wa_s[j], wb_s[j])
        if last:
            regroup_new_cache()
        else:
            _mod_tile(cond_ref, wm_ref, bm_ref, mod_ref)

    @pl.when(s == FF_STEPS - 1)
    def _last_chunk_down():
        store(down())
        if last:
            regroup_new_cache()

    if last:
        @pl.when(s < FF_REGROUP_EARLY)
        def _regroup_early():
            regroup_new_cache()


def _ffn(x, g_ffn, modc, w_gate_up, w_down, l, cond8=None, w_mod=None, b_mod=None, new_kv=None):
    last = cond8 is None

    def chunk(s):
        return jnp.maximum(s - FF_TILES, 0)

    def up_chunk(s):
        return jnp.clip(s - (FF_TILES - 1), 0, FF_CHUNKS - 1)

    def tile(s):
        return jnp.minimum(s, FF_TILES - 1)

    def mod_tile(s):
        return jnp.clip(s - FF_TILES, 0, MOD_TILES - 1)

    def mod_spec(k, which):
        return pl.BlockSpec((None, 1, D_MODEL), lambda s: (_cond_row(which(s) * FF_ROWS // CHUNK), 0, k))

    in_specs = [pl.BlockSpec((FF_ROWS, D_MODEL), lambda s: (chunk(s), 0)),
                pl.BlockSpec((FF_ROWS, D_MODEL), lambda s: (up_chunk(s), 0)),
                pl.BlockSpec((None, 1, D_MODEL), lambda s: (l, 0, 0)),
                mod_spec(3, up_chunk), mod_spec(4, up_chunk), mod_spec(5, chunk),
                pl.BlockSpec((None, D_MODEL, TN), lambda s: (l, 0, tile(s))),
                pl.BlockSpec((None, D_MODEL, TN), lambda s: (l, 0, FF_TILES + tile(s))),
                pl.BlockSpec((None, TN, D_MODEL), lambda s: (l, tile(s), 0))]
    args = [x, x, g_ffn.reshape(DEPTH, 1, D_MODEL), modc, modc, modc, w_gate_up, w_gate_up, w_down]
    if last:
        def seq(s):
            return jnp.minimum(s, FF_REGROUP_EARLY - 1) + jnp.maximum(s - (FF_TILES - 1), 0)

        in_specs += [pl.BlockSpec((1, 4, KV_W, SEQ), lambda s: (seq(s), 0, 0, 0))] * DEPTH
        args += list(new_kv)
        out_specs = [pl.BlockSpec((FF_ROWS, D_MODEL), lambda s: (jnp.minimum(chunk(s), FF_P_CHUNKS - 1), 0)),
                     pl.BlockSpec((FF_ROWS, D_MODEL), lambda s: (jnp.maximum(chunk(s) - FF_P_CHUNKS, 0), 0))]
        out_shape = [jax.ShapeDtypeStruct((P_ROWS, D_MODEL), F32), jax.ShapeDtypeStruct((S_ROWS, D_MODEL), F32)]
        out_specs += [pl.BlockSpec((1, DEPTH, KV_W, SEQ), lambda s: (seq(s), 0, 0, 0))] * 4
        out_shape += [jax.ShapeDtypeStruct((BATCH, DEPTH, KV_W, SEQ), F32)] * 4
    else:
        in_specs += [pl.BlockSpec((COND_ROWS, D_MODEL), lambda s: (0, 0)),
                     pl.BlockSpec((None, D_MODEL, MOD_TN), lambda s: (l + 1, 0, mod_tile(s))),
                     pl.BlockSpec((None, 1, MOD_TN), lambda s: (l + 1, 0, mod_tile(s)))]
        args += [cond8, w_mod, b_mod.reshape(DEPTH, 1, N_MOD)]
        out_specs = [pl.BlockSpec((FF_ROWS, D_MODEL), lambda s: (chunk(s), 0)),
                     pl.BlockSpec((COND_ROWS, 1, MOD_TN), lambda s: (0, 0, mod_tile(s)))]
        out_shape = [jax.ShapeDtypeStruct((ROWS, D_MODEL), F32),
                     jax.ShapeDtypeStruct((COND_ROWS, 1, N_MOD), F32)]
    return pl.pallas_call(
        functools.partial(_ffn_kernel, last=last),
        grid=(FF_STEPS,),
        in_specs=in_specs,
        out_specs=out_specs,
        out_shape=out_shape,
        scratch_shapes=[pltpu.VMEM((FF_TILES, D_MODEL, TN), BF16),
                        pltpu.VMEM((FF_TILES, D_MODEL, TN), BF16),
                        pltpu.VMEM((D_FF, D_MODEL), BF16),
                        pltpu.VMEM((FF_ROWS, D_MODEL), BF16),
                        pltpu.VMEM((FF_ROWS, D_FF), BF16)],
        compiler_params=_params(),
        name="ffn",
    )(*args)


def _rope_tables():
    rows = DEC_SEQ // GRID_W
    row = np.repeat(np.arange(rows, dtype=np.float32), GRID_W)
    col = np.tile(np.arange(GRID_W, dtype=np.float32), rows)
    inv = (1.0 / (np.float32(ROPE_THETA) ** (np.arange(N_FREQ, dtype=np.float32) / N_FREQ))).astype(np.float32)
    ang_r, ang_c = row[:, None] * inv, col[:, None] * inv
    cr, sr, cc, sc = np.cos(ang_r), np.sin(ang_r), np.cos(ang_c), np.sin(ang_c)
    cos = np.concatenate([cr, cr, cc, cc], axis=1)
    sin = np.concatenate([-sr, sr, -sc, sc], axis=1)
    reps = LANES // HEAD_DIM
    return (jnp.asarray(np.tile(cos, (1, reps)), F32), jnp.asarray(np.tile(sin, (1, reps)), F32))


def kernel(x_prompt, x_sample, cache_k_attn, cache_v_attn, cache_k_win, cache_v_win, c, c_ctx,
           w_mod, b_mod, g_mix, g_ffn, w_in, gq_attn, gk_attn, gq_win, gk_win, sink_win,
           conv_w, pool_w, pool_scale, w_branch, w_out, w_gate_up, w_down):
    cond8 = jnp.zeros((COND_ROWS, D_MODEL), F32).at[0].set(c_ctx).at[1:1 + DEC_BATCH].set(c)
    modc = _modulation(cond8, w_mod, b_mod, 0)

    cos, sin = _rope_tables()
    gains = jnp.tile(jnp.stack([gq_attn, gq_win, gk_attn, gk_win], axis=1), (1, 1, TN // HEAD_DIM))
    gains = gains.reshape(DEPTH, 4, 1, TN)
    caches = [a.transpose(0, 1, 3, 4, 2).reshape(DEC_BATCH, DEPTH, KV_W, PAST_LEN)
              for a in (cache_k_attn, cache_v_attn, cache_k_win, cache_v_win)]

    xs = (x_prompt.reshape(P_ROWS, D_MODEL), x_sample.reshape(S_ROWS, D_MODEL))
    new_kv = []
    for l in range(DEPTH):
        q, kv, conv, pool, new, *stacked = _proj(xs, g_mix, modc, w_in, gains, cos, sin,
                                                 conv_w, pool_w, pool_scale, l)
        x = stacked[0] if stacked else xs[0]
        new_kv.append(new)
        att = _attention(sink_win, q, kv, *caches, l)
        x = _mixer_out(x, g_mix, modc, att, conv, pool, w_in, w_branch, w_out, l)
        if l + 1 < DEPTH:
            x, modc_next = _ffn(x, g_ffn, modc, w_gate_up, w_down, l, cond8, w_mod, b_mod)
            modc = modc_next
            xs = (x,)
        else:
            y_prompt, y_sample, *by_kind = _ffn(x, g_ffn, modc, w_gate_up, w_down, l, new_kv=new_kv)

    outs = [a.reshape(BATCH, DEPTH, N_KV_HEADS, HEAD_DIM, SEQ).transpose(0, 1, 4, 2, 3) for a in by_kind]
    return (y_prompt.reshape(BATCH, SEQ, D_MODEL), y_sample.reshape(DEC_BATCH, DEC_SEQ, D_MODEL), *outs)
```

```python
import functools

import numpy as np
import jax
import jax.numpy as jnp
from jax import lax
from jax.experimental import pallas as pl
from jax.experimental.pallas import tpu as pltpu

D_MODEL = 1024
BATCH = 16
SEQ = 256
DEPTH = 4
DEC_BATCH = 2
DEC_SEQ = 1024
PAST_LEN = 512
GRID_W = 64
HEAD_DIM = 64
N_Q_HEADS = 8
N_KV_HEADS = 2
ATT_W = N_Q_HEADS * HEAD_DIM
KV_W = N_KV_HEADS * HEAD_DIM
N_FREQ = HEAD_DIM // 4
ROPE_THETA = 10000.0
CONV_W = 512
POOL_W = 512
POOL_SIZES = (2, 4, 8, 16)
N_BRANCH = 4
BRANCH_W = 512
D_FF = 2816
WINDOW = 128
EPS = 1e-6
NEG = -1e30

P_ROWS = BATCH * SEQ
S_ROWS = DEC_BATCH * DEC_SEQ
ROWS = P_ROWS + S_ROWS
CHUNK = 1024
N_CHUNKS = ROWS // CHUNK
N_P_CHUNKS = P_ROWS // CHUNK
SEQ_PER_CHUNK = CHUNK // SEQ
TN = 256
LANES = 128
VMEM_LIMIT = 58 * 1024 * 1024

COL_QA, COL_KVA, COL_QC, COL_KVC = 0, 2, 3, 5
COL_U, COL_GB, COL_GC, COL_PV, COL_GATE = 6, 8, 10, 12, 14

F32 = jnp.float32
BF16 = jnp.bfloat16

assert SEQ & (SEQ - 1) == 0 and CHUNK % SEQ == 0 and DEC_SEQ == CHUNK and P_ROWS % CHUNK == 0
assert HEAD_DIM * 2 == LANES and KV_W == LANES and ATT_W % TN == 0 and POOL_W == LANES * len(POOL_SIZES)
assert N_Q_HEADS // N_KV_HEADS == 4 and COL_GATE * TN == 2 * ATT_W + 4 * KV_W + 3 * CONV_W + POOL_W


def _params():
    return pltpu.CompilerParams(dimension_semantics=("arbitrary",), vmem_limit_bytes=VMEM_LIMIT)


def _dot(a, b):
    return jnp.dot(a, b, preferred_element_type=F32)


def _dot_nt(a, b):
    return lax.dot_general(a, b, (((1,), (1,)), ((), ())), preferred_element_type=F32)


def _sigmoid(x):
    return 1.0 / (1.0 + jnp.exp(-x))


def _norm_modulate(x, g, scale, shift):
    y = x * lax.rsqrt(jnp.mean(x * x, axis=-1, keepdims=True) + EPS)
    return ((y * g) * (1.0 + scale) + shift).astype(BF16)


def _head_segments(width):
    r = lax.broadcasted_iota(jnp.int32, (width, width), 0) // HEAD_DIM
    c = lax.broadcasted_iota(jnp.int32, (width, width), 1) // HEAD_DIM
    return jnp.where(r == c, 1.0 / HEAD_DIM, 0.0).astype(BF16)


def _head_mean_square(y, seg):
    return _dot((y * y).astype(BF16), seg)


def _swap16(x):
    lane = lax.broadcasted_iota(jnp.int32, x.shape, 1)
    fwd = pltpu.roll(x, LANES - 16, axis=1)
    bwd = pltpu.roll(x, 16, axis=1)
    return jnp.where((lane & 16) == 0, fwd, bwd)


def _rope128(x, cos, sin):
    return x * cos + _swap16(x) * sin


N_MOD = 6 * D_MODEL
COND_ROWS = 8
MOD_TN = 768
MOD_TILES = N_MOD // MOD_TN


def _cond_row(chunk):
    return jnp.maximum(chunk - (N_P_CHUNKS - 1), 0)


def _mod_tile(cond_ref, w_ref, b_ref, o_ref):
    c = cond_ref[...]
    s = (c * _sigmoid(c)).astype(BF16)
    res = _dot(s, w_ref[...].astype(BF16)) + b_ref[...]
    for r in range(COND_ROWS):
        o_ref[r] = res[r:r + 1, :]


def _modulation(cond8, w_mod, b_mod, l):
    tn = 2 * MOD_TN
    return pl.pallas_call(
        _mod_tile,
        grid=(N_MOD // tn,),
        in_specs=[pl.BlockSpec((COND_ROWS, D_MODEL), lambda j: (0, 0)),
                  pl.BlockSpec((None, D_MODEL, tn), lambda j: (l, 0, j)),
                  pl.BlockSpec((None, 1, tn), lambda j: (l, 0, j))],
        out_specs=pl.BlockSpec((COND_ROWS, 1, tn), lambda j: (0, 0, j)),
        out_shape=jax.ShapeDtypeStruct((COND_ROWS, 1, N_MOD), F32),
        compiler_params=_params(),
        name="modulation",
    )(cond8, w_mod, b_mod.reshape(DEPTH, 1, N_MOD))


PJ_TILES = COL_GATE


def _proj_loads(first):
    per_load = 1 if first else 7
    return per_load, PJ_TILES // per_load


def _seq_pos(latent, width):
    row = lax.broadcasted_iota(jnp.int32, (CHUNK, width), 0)
    return row if latent else row & (SEQ - 1)


def _shift_rows(x, k, pos, seq_len):
    rolled = pltpu.roll(x, (-k) % CHUNK, axis=0)
    ok = (pos + k >= 0) & (pos + k < seq_len)
    return jnp.where(ok, rolled, 0.0)


def _window_mean_minus_token(p, half, pos, seq_len):
    fwd = p
    bwd = _shift_rows(p, -1, pos, seq_len)
    m = 1
    while m < half:
        fwd = fwd + _shift_rows(fwd, m, pos, seq_len)
        bwd = bwd + _shift_rows(bwd, -m, pos, seq_len)
        m *= 2
    count = jnp.minimum(pos + half, seq_len) - jnp.maximum(pos - half, 0)
    return (fwd + bwd) / count.astype(F32) - p


def _proj_chunk(latent, x_ref, g_ref, sh_ref, sc_ref, gain_ref, cos_ref, sin_ref, cw_ref, pw_ref,
                ps_ref, q_ref, kv_ref, conv_ref, pool_ref, new_ref, w_s, h_s):
    h_s[...] = _norm_modulate(x_ref[...], g_ref[...], sc_ref[...], sh_ref[...])
    seq_len = DEC_SEQ if latent else SEQ
    seg_q = _head_segments(TN)
    seg_k = _head_segments(KV_W)
    pos_wide = _seq_pos(latent, TN)
    pos = _seq_pos(latent, LANES)

    def q_tile(mixer, j, y):
        y = y * lax.rsqrt(_head_mean_square(y, seg_q) + EPS) * (gain_ref[mixer] * (HEAD_DIM ** -0.5))
        for s in range(TN // LANES):
            part = y[:, s * LANES:(s + 1) * LANES]
            if latent:
                part = _rope128(part, cos_ref[...], sin_ref[...])
            out_col = mixer * ATT_W + j * TN + s * LANES
            q_ref[:, out_col:out_col + LANES] = part.astype(BF16)

    def kv_tile(mixer, y):
        k = y[:, :KV_W]
        v = y[:, KV_W:]
        k = k * lax.rsqrt(_head_mean_square(k, seg_k) + EPS) * gain_ref[2 + mixer][:, :KV_W]
        if latent:
            k = _rope128(k, cos_ref[...], sin_ref[...])
        else:
            for b in range(SEQ_PER_CHUNK):
                new_ref[b, 2 * mixer] = k[b * SEQ:(b + 1) * SEQ, :].T
                new_ref[b, 2 * mixer + 1] = v[b * SEQ:(b + 1) * SEQ, :].T
        kv_ref[:, 2 * mixer * KV_W:(2 * mixer + 1) * KV_W] = k.astype(BF16)
        kv_ref[:, (2 * mixer + 1) * KV_W:(2 * mixer + 2) * KV_W] = v.astype(BF16)

    def conv_tile(j, gc, u, gb):
        cols = slice(j * TN, (j + 1) * TN)
        z = gc * u
        y = (_shift_rows(z, -1, pos_wide, seq_len) * cw_ref[0:1, cols] + z * cw_ref[1:2, cols]
             + _shift_rows(z, 1, pos_wide, seq_len) * cw_ref[2:3, cols])
        conv_ref[:, cols] = (gb * y).astype(BF16)

    def pool_tile(j, p):
        for s in range(TN // LANES):
            grp = j * (TN // LANES) + s
            cols = slice(grp * LANES, (grp + 1) * LANES)
            y = _window_mean_minus_token(p[:, s * LANES:(s + 1) * LANES], POOL_SIZES[grp] // 2, pos, seq_len)
            pool_ref[:, cols] = (_dot(y.astype(BF16), pw_ref[grp].astype(BF16)) * ps_ref[:, cols]).astype(BF16)

    tasks = []
    for j in range(POOL_W // TN):
        tasks.append(((COL_PV + j,), functools.partial(pool_tile, j)))
        tasks.append(((COL_GC + j, COL_U + j, COL_GB + j), functools.partial(conv_tile, j)))
    for mixer, col in enumerate((COL_QA, COL_QC)):
        for j in range(ATT_W // TN):
            tasks.append(((col + j,), functools.partial(q_tile, mixer, j)))
    for mixer, col in enumerate((COL_KVA, COL_KVC)):
        tasks.append(((col,), functools.partial(kv_tile, mixer)))

    def matmuls(tiles):
        return [_dot(h_s[...], w_s[t]) for t in tiles]

    ys = matmuls(tasks[0][0])
    for i, (_, epilogue) in enumerate(tasks):
        nxt = matmuls(tasks[i + 1][0]) if i + 1 < len(tasks) else None
        epilogue(*ys)
        ys = nxt


def _proj_kernel(*refs, first):
    if first:
        xp_ref, xs_ref, *refs = refs
    else:
        xp_ref = xs_ref = refs[0]
        refs = refs[1:]
    (g_ref, sh_ref, sc_ref, w_ref, gain_ref, cos_ref, sin_ref, cw_ref, pw_ref, ps_ref,
     q_ref, kv_ref, conv_ref, pool_ref, new_ref, *refs) = refs
    xcat_ref = refs[0] if first else None
    w_s, h_s = refs[-2:]
    per_load, n_loads = _proj_loads(first)
    s = pl.program_id(0)
    rest = (g_ref, sh_ref, sc_ref, gain_ref, cos_ref, sin_ref, cw_ref, pw_ref, ps_ref,
            q_ref, kv_ref, conv_ref, pool_ref, new_ref, w_s, h_s)

    @pl.when(s < n_loads)
    def _stream():
        for j in range(per_load):
            w_s[per_load * s + j] = w_ref[:, j * TN:(j + 1) * TN].astype(BF16)

    @pl.when((s >= n_loads) & (s < n_loads + N_P_CHUNKS))
    def _prompt():
        if first:
            xcat_ref[...] = xp_ref[...]
        _proj_chunk(False, xp_ref, *rest)

    @pl.when(s >= n_loads + N_P_CHUNKS)
    def _latent():
        if first:
            xcat_ref[...] = xs_ref[...]
        _proj_chunk(True, xs_ref, *rest)


def _proj(xs, g_mix, modc, w_in, gains, cos, sin, conv_w, pool_w, pool_scale, l):
    first = len(xs) == 2
    per_load, n_loads = _proj_loads(first)

    def chunk(s):
        return jnp.clip(s - n_loads, 0, N_CHUNKS - 1)

    def p_chunk(s):
        return jnp.clip(s - n_loads, 0, N_P_CHUNKS - 1)

    def s_chunk(s):
        return jnp.clip(s - n_loads - N_P_CHUNKS, 0, N_CHUNKS - N_P_CHUNKS - 1)

    def mod_spec(k):
        return pl.BlockSpec((None, 1, D_MODEL), lambda s: (_cond_row(chunk(s)), 0, k))

    def const(shape):
        return pl.BlockSpec(shape, lambda s: (l,) + (0,) * (len(shape) - 1))

    def rows_out(width):
        return pl.BlockSpec((CHUNK, width), lambda s: (chunk(s), 0))

    if first:
        x_specs = [pl.BlockSpec((CHUNK, D_MODEL), lambda s: (p_chunk(s), 0)),
                   pl.BlockSpec((CHUNK, D_MODEL), lambda s: (s_chunk(s), 0))]
    else:
        x_specs = [rows_out(D_MODEL)]
    out_specs = [rows_out(2 * ATT_W), rows_out(4 * KV_W), rows_out(CONV_W), rows_out(POOL_W),
                 pl.BlockSpec((SEQ_PER_CHUNK, 4, KV_W, SEQ), lambda s: (p_chunk(s), 0, 0, 0))]
    out_shape = [jax.ShapeDtypeStruct((ROWS, 2 * ATT_W), BF16),
                 jax.ShapeDtypeStruct((ROWS, 4 * KV_W), BF16),
                 jax.ShapeDtypeStruct((ROWS, CONV_W), BF16),
                 jax.ShapeDtypeStruct((ROWS, POOL_W), BF16),
                 jax.ShapeDtypeStruct((BATCH, 4, KV_W, SEQ), F32)]
    if first:
        out_specs.append(rows_out(D_MODEL))
        out_shape.append(jax.ShapeDtypeStruct((ROWS, D_MODEL), F32))
    return pl.pallas_call(
        functools.partial(_proj_kernel, first=first),
        grid=(n_loads + N_CHUNKS,),
        in_specs=x_specs + [
            const((None, 1, D_MODEL)), mod_spec(0), mod_spec(1),
            pl.BlockSpec((None, D_MODEL, per_load * TN), lambda s: (l, 0, jnp.minimum(s, n_loads - 1))),
            const((None, 4, 1, TN)),
            pl.BlockSpec((DEC_SEQ, LANES), lambda s: (0, 0)),
            pl.BlockSpec((DEC_SEQ, LANES), lambda s: (0, 0)),
            const((None, 3, CONV_W)), const((None, len(POOL_SIZES), LANES, LANES)),
            const((None, 1, POOL_W))],
        out_specs=out_specs,
        out_shape=out_shape,
        scratch_shapes=[pltpu.VMEM((PJ_TILES, D_MODEL, TN), BF16),
                        pltpu.VMEM((CHUNK, D_MODEL), BF16)],
        compiler_params=_params(),
        name="proj",
    )(*xs, g_mix.reshape(DEPTH, 1, D_MODEL), modc, modc, w_in, gains, cos, sin,
      conv_w, pool_w, pool_scale.reshape(DEPTH, 1, POOL_W))


TQ = 256
N_QT = DEC_SEQ // TQ
WIN_KEYS = TQ + 2 * WINDOW
PAD_SEQ = WINDOW + DEC_SEQ + WINDOW


def _split_lanes(x, ones=False):
    xr = pltpu.roll(x, HEAD_DIM, axis=1)
    lane = lax.broadcasted_iota(jnp.int32, x.shape, 1)
    low = lane < HEAD_DIM
    fill_lo = jnp.where(lane == HEAD_DIM, 1.0, 0.0) if ones else 0.0
    fill_hi = jnp.where(lane == 0, 1.0, 0.0) if ones else 0.0
    return (jnp.where(low, x, fill_lo).astype(BF16), jnp.where(low, fill_hi, xr).astype(BF16),
            jnp.where(low, xr, fill_lo).astype(BF16), jnp.where(low, fill_hi, x).astype(BF16))


def _split_rows(xt, ones=False):
    z = jnp.zeros((HEAD_DIM, xt.shape[1]), F32)
    if ones:
        z = jnp.where(lax.broadcasted_iota(jnp.int32, z.shape, 0) == 0, 1.0, z)
    h0, h1 = xt[:HEAD_DIM], xt[HEAD_DIM:]
    return tuple(jnp.concatenate(p, axis=0).astype(BF16) for p in ((h0, z), (z, h0), (h1, z), (z, h1)))


def _attend_pairs(tasks, denom_from_values=False):
    units = [(t, parity) for t in range(len(tasks)) for parity in range(2)]

    def scores_of(unit):
        t, parity = unit
        q, pieces, _, _ = tasks[t]
        out = []
        for k_lo, k_hi, k_t, _, _, _, mask in pieces:
            k = k_hi if parity else k_lo
            s = _dot(q, k) if k_t else _dot_nt(q, k)
            if mask is not None:
                s = jnp.where(mask, s, NEG)
            out.append(s)
        return out

    scores = scores_of(units[0])
    even = None
    for i, (t, parity) in enumerate(units):
        nxt = scores_of(units[i + 1]) if i + 1 < len(units) else None
        _, pieces, sinks, write = tasks[t]
        m = functools.reduce(jnp.maximum, [jnp.max(s, axis=1, keepdims=True) for s in scores])
        if sinks is not None:
            m = jnp.maximum(m, sinks[parity])
        acc = None
        denom = None
        for s, (_, _, _, v_lo, v_hi, v_t, _) in zip(scores, pieces):
            p = jnp.exp(s - m)
            if not denom_from_values:
                d = jnp.sum(p, axis=1, keepdims=True)
                denom = d if denom is None else denom + d
            v = v_hi if parity else v_lo
            a = _dot_nt(p.astype(BF16), v) if v_t else _dot(p.astype(BF16), v)
            acc = a if acc is None else acc + a
        if denom_from_values:
            ones_lane = 0 if parity else HEAD_DIM
            denom = acc[:, ones_lane:ones_lane + 1]
        if sinks is not None:
            denom = denom + jnp.exp(sinks[parity] - m)
        acc = acc / denom
        if denom_from_values:
            lane = lax.broadcasted_iota(jnp.int32, acc.shape, 1)
            acc = jnp.where((lane >= HEAD_DIM) if parity else (lane < HEAD_DIM), acc, 0.0)
        if parity == 0:
            even = acc
        else:
            write(even + acc)
        scores = nxt


def _attn_prompt_seq(rows, sink_ref, q_ref, kv_ref, o_ref, layer):
    def writer(col):
        def write(out):
            o_ref[rows, col:col + LANES] = out.astype(BF16)
        return write

    tasks = []
    for mixer in range(2):
        k = _split_lanes(kv_ref[rows, 2 * mixer * KV_W:(2 * mixer + 1) * KV_W].astype(F32))
        v = _split_lanes(kv_ref[rows, (2 * mixer + 1) * KV_W:(2 * mixer + 2) * KV_W].astype(F32))
        for pair in range(N_Q_HEADS // 2):
            lo = 2 * (pair // 2)
            piece = (k[lo], k[lo + 1], False, v[lo], v[lo + 1], False, None)
            sinks = None
            if mixer == 1:
                sinks = (sink_ref[layer, 2 * pair], sink_ref[layer, 2 * pair + 1])
            col = mixer * ATT_W + pair * LANES
            tasks.append((q_ref[rows, col:col + LANES], [piece], sinks, writer(col)))
    _attend_pairs(tasks)


def _attn_latent_fill(kv_ref, cka_ref, cva_ref, ckc_ref, cvc_ref, ctx_s, ka_s, va_s, kc_s, vc_s):
    for i, ref in enumerate((cka_ref, cva_ref, ckc_ref, cvc_ref)):
        for j, part in enumerate(_split_rows(ref[...], ones=i % 2 == 1)):
            ctx_s[4 * i + j] = part
    for dst, col in ((ka_s, 0), (va_s, KV_W)):
        for j, part in enumerate(_split_lanes(kv_ref[:, col:col + KV_W].astype(F32), ones=dst is va_s)):
            dst[j] = part
    zeros = jnp.zeros((WINDOW, LANES), BF16)
    for dst, col in ((kc_s, 2 * KV_W), (vc_s, 3 * KV_W)):
        for j, part in enumerate(_split_lanes(kv_ref[:, col:col + KV_W].astype(F32), ones=dst is vc_s)):
            dst[j, 0:WINDOW, :] = zeros
            dst[j, WINDOW:WINDOW + DEC_SEQ, :] = part
            dst[j, WINDOW + DEC_SEQ:PAD_SEQ, :] = zeros


def _attn_latent_tile(qt, sink_ref, q_ref, o_ref, ctx_s, ka_s, va_s, kc_s, vc_s, layer):
    q0 = qt * TQ
    rows = pl.ds(pl.multiple_of(q0, TQ), TQ)
    win = pl.ds(pl.multiple_of(q0, TQ), WIN_KEYS)
    r = lax.broadcasted_iota(jnp.int32, (TQ, WIN_KEYS), 0)
    jk = lax.broadcasted_iota(jnp.int32, (TQ, WIN_KEYS), 1)
    kpos = q0 - WINDOW + jk
    band = jnp.where((jk - r >= 0) & (jk - r <= 2 * WINDOW), kpos, -1)
    mask = (band >= 0) & (band < DEC_SEQ)

    def writer(col):
        def write(out):
            o_ref[rows, col:col + LANES] = out.astype(BF16)
        return write

    tasks = []
    for pair in range(N_Q_HEADS // 2):
        lo = 2 * (pair // 2)
        hi = lo + 1
        col = pair * LANES
        ctx = (ctx_s[lo], ctx_s[hi], True, ctx_s[4 + lo], ctx_s[4 + hi], True, None)
        cur = (ka_s[lo], ka_s[hi], False, va_s[lo], va_s[hi], False, None)
        tasks.append((q_ref[rows, col:col + LANES], [ctx, cur], None, writer(col)))

        col = ATT_W + pair * LANES
        ctx = (ctx_s[8 + lo], ctx_s[8 + hi], True, ctx_s[12 + lo], ctx_s[12 + hi], True, None)
        near = (kc_s[lo, win, :], kc_s[hi, win, :], False, vc_s[lo, win, :], vc_s[hi, win, :], False, mask)
        sinks = (sink_ref[layer, 2 * pair], sink_ref[layer, 2 * pair + 1])
        tasks.append((q_ref[rows, col:col + LANES], [ctx, near], sinks, writer(col)))
    _attend_pairs(tasks, denom_from_values=True)


def _attn_kernel(sink_ref, q_ref, kv_ref, cka_ref, cva_ref, ckc_ref, cvc_ref, o_ref,
                 ctx_s, ka_s, va_s, kc_s, vc_s, *, layer):
    s = pl.program_id(0)

    @pl.when(s < N_P_CHUNKS)
    def _prompt():
        def seq(b, carry):
            _attn_prompt_seq(pl.ds(pl.multiple_of(b * SEQ, SEQ), SEQ), sink_ref, q_ref, kv_ref, o_ref, layer)
            return carry
        lax.fori_loop(0, SEQ_PER_CHUNK, seq, 0)

    @pl.when(s >= N_P_CHUNKS)
    def _latent():
        _attn_latent_fill(kv_ref, cka_ref, cva_ref, ckc_ref, cvc_ref, ctx_s, ka_s, va_s, kc_s, vc_s)

        def tile(qt, carry):
            _attn_latent_tile(qt, sink_ref, q_ref, o_ref, ctx_s, ka_s, va_s, kc_s, vc_s, layer)
            return carry
        lax.fori_loop(0, N_QT, tile, 0)


def _attention(sink, q, kv, cka, cva, ckc, cvc, l):
    cache_spec = pl.BlockSpec((None, None, KV_W, PAST_LEN),
                              lambda s: (jnp.maximum(s - N_P_CHUNKS, 0), l, 0, 0))
    return pl.pallas_call(
        functools.partial(_attn_kernel, layer=l),
        grid=(N_CHUNKS,),
        in_specs=[pl.BlockSpec(memory_space=pltpu.SMEM),
                  pl.BlockSpec((CHUNK, 2 * ATT_W), lambda s: (s, 0)),
                  pl.BlockSpec((CHUNK, 4 * KV_W), lambda s: (s, 0)),
                  cache_spec, cache_spec, cache_spec, cache_spec],
        out_specs=pl.BlockSpec((CHUNK, 2 * ATT_W), lambda s: (s, 0)),
        out_shape=jax.ShapeDtypeStruct((ROWS, 2 * ATT_W), BF16),
        scratch_shapes=[pltpu.VMEM((16, KV_W, PAST_LEN), BF16),
                        pltpu.VMEM((4, DEC_SEQ, LANES), BF16), pltpu.VMEM((4, DEC_SEQ, LANES), BF16),
                        pltpu.VMEM((4, PAD_SEQ, LANES), BF16), pltpu.VMEM((4, PAD_SEQ, LANES), BF16)],
        compiler_params=_params(),
        name="attention",
    )(sink, q, kv, cka, cva, ckc, cvc)


MG_ROWS = 512
MG_CHUNKS = ROWS // MG_ROWS
MG_TILES = D_MODEL // TN
MG_STEPS = MG_TILES + MG_CHUNKS


def _mixer_out_kernel(x_ref, xn_ref, g_ref, sh_ref, sc_ref, gate_ref, att_ref, conv_ref, pool_ref,
                      wg0, wg1, wg2, wg3, wb0, wb1, wb2, wb3, wo_ref, o_ref,
                      wg_s, wb_s, wo_s, h_s, mix_s):
    s = pl.program_id(0)

    def norm():
        h_s[...] = _norm_modulate(xn_ref[...], g_ref[...], sc_ref[...], sh_ref[...])

    @pl.when(s == 0)
    def _first_norm():
        norm()

    def mix_tile(wgs, wbs):
        h = h_s[...]
        branches = (att_ref[:, :ATT_W], conv_ref[...], att_ref[:, ATT_W:], pool_ref[...])
        acc = None
        for br, wg, wb in zip(branches, wgs, wbs):
            term = _sigmoid(_dot(h, wg)) * _dot(br, wb)
            acc = term if acc is None else acc + term
        return acc.astype(BF16)

    def out():
        o_ref[...] = x_ref[...] + gate_ref[...] * _dot(mix_s[...], wo_s[...])

    @pl.when(s < MG_TILES)
    def _stream():
        wgs = [r[...].astype(BF16) for r in (wg0, wg1, wg2, wg3)]
        wbs = [r[...].astype(BF16) for r in (wb0, wb1, wb2, wb3)]
        for k in range(N_BRANCH):
            wg_s[s * N_BRANCH + k] = wgs[k]
            wb_s[s * N_BRANCH + k] = wbs[k]
        wo_s[pl.ds(pl.multiple_of(s * TN, TN), TN), :] = wo_ref[...].astype(BF16)
        v = mix_tile(wgs, wbs)
        for n in range(MG_TILES):
            @pl.when(s == n)
            def _store(n=n):
                mix_s[:, n * TN:(n + 1) * TN] = v

    @pl.when((s >= MG_TILES) & (s < MG_STEPS - 1))
    def _chunk():
        out()
        norm()
        for n in range(MG_TILES):
            mix_s[:, n * TN:(n + 1) * TN] = mix_tile(
                [wg_s[n * N_BRANCH + k] for k in range(N_BRANCH)],
                [wb_s[n * N_BRANCH + k] for k in range(N_BRANCH)])

    @pl.when(s == MG_STEPS - 1)
    def _last_chunk_out():
        out()


def _mixer_out(x, g_mix, modc, att, conv, pool, w_in, w_branch, w_out, l):
    def chunk(s):
        return jnp.maximum(s - MG_TILES, 0)

    def mix_chunk(s):
        return jnp.clip(s - (MG_TILES - 1), 0, MG_CHUNKS - 1)

    def tile(s):
        return jnp.minimum(s, MG_TILES - 1)

    def mod_spec(k, which):
        return pl.BlockSpec((None, 1, D_MODEL), lambda s: (_cond_row(which(s) * MG_ROWS // CHUNK), 0, k))

    def rows_in(width):
        return pl.BlockSpec((MG_ROWS, width), lambda s: (mix_chunk(s), 0))

    gate_specs = [pl.BlockSpec((None, D_MODEL, TN), functools.partial(
        lambda s, k: (l, 0, COL_GATE + MG_TILES * k + tile(s)), k=k)) for k in range(N_BRANCH)]
    br_specs = [pl.BlockSpec((None, None, BRANCH_W, TN), functools.partial(
        lambda s, k: (l, k, 0, tile(s)), k=k)) for k in range(N_BRANCH)]
    return pl.pallas_call(
        _mixer_out_kernel,
        grid=(MG_STEPS,),
        in_specs=[pl.BlockSpec((MG_ROWS, D_MODEL), lambda s: (chunk(s), 0)),
                  pl.BlockSpec((MG_ROWS, D_MODEL), lambda s: (mix_chunk(s), 0)),
                  pl.BlockSpec((None, 1, D_MODEL), lambda s: (l, 0, 0)),
                  mod_spec(0, mix_chunk), mod_spec(1, mix_chunk), mod_spec(2, chunk),
                  rows_in(2 * ATT_W), rows_in(CONV_W), rows_in(POOL_W)] + gate_specs + br_specs
                 + [pl.BlockSpec((None, TN, D_MODEL), lambda s: (l, tile(s), 0))],
        out_specs=pl.BlockSpec((MG_ROWS, D_MODEL), lambda s: (chunk(s), 0)),
        out_shape=jax.ShapeDtypeStruct((ROWS, D_MODEL), F32),
        scratch_shapes=[pltpu.VMEM((MG_TILES * N_BRANCH, D_MODEL, TN), BF16),
                        pltpu.VMEM((MG_TILES * N_BRANCH, BRANCH_W, TN), BF16),
                        pltpu.VMEM((D_MODEL, D_MODEL), BF16),
                        pltpu.VMEM((MG_ROWS, D_MODEL), BF16),
                        pltpu.VMEM((MG_ROWS, D_MODEL), BF16)],
        compiler_params=_params(),
        name="mixer_out",
    )(x, x, g_mix.reshape(DEPTH, 1, D_MODEL), modc, modc, modc, att, conv, pool,
      *([w_in] * N_BRANCH), *([w_branch] * N_BRANCH), w_out)


FF_ROWS = 512
FF_CHUNKS = ROWS // FF_ROWS
FF_P_CHUNKS = P_ROWS // FF_ROWS
FF_TILES = D_FF // TN
FF_STEPS = FF_TILES + FF_CHUNKS
FF_FIRST_LATENT_STEP = FF_TILES + FF_P_CHUNKS
assert MOD_TILES <= FF_CHUNKS - 1
FF_REGROUP_EARLY = BATCH - FF_CHUNKS
assert 0 <= FF_REGROUP_EARLY <= FF_TILES


def _ffn_kernel(x_ref, xn_ref, g_ref, sh_ref, sc_ref, gate_ref, wa_ref, wb_ref, wd_ref, *rest, last):
    if last:
        new_refs = rest[:DEPTH]
        op_ref, os_ref = rest[DEPTH:DEPTH + 2]
        kind_refs = rest[DEPTH + 2:DEPTH + 6]
        wa_s, wb_s, wd_s, h_s, act_s = rest[DEPTH + 6:]
    else:
        cond_ref, wm_ref, bm_ref, op_ref, mod_ref, wa_s, wb_s, wd_s, h_s, act_s = rest
        os_ref = op_ref
    s = pl.program_id(0)

    def norm():
        h_s[...] = _norm_modulate(xn_ref[...], g_ref[...], sc_ref[...], sh_ref[...])

    def up(wa, wb):
        h = h_s[...]
        a = _dot(h, wa)
        return (a * _sigmoid(a) * _dot(h, wb)).astype(BF16)

    def down():
        return x_ref[...] + gate_ref[...] * _dot(act_s[...], wd_s[...])

    def regroup_new_cache():
        for kind, dst in enumerate(kind_refs):
            for depth, src in enumerate(new_refs):
                dst[0, depth] = src[0, kind]

    def store(out):
        if last:
            @pl.when(s < FF_FIRST_LATENT_STEP)
            def _store_prompt():
                op_ref[...] = out

            @pl.when(s >= FF_FIRST_LATENT_STEP)
            def _store_latent():
                os_ref[...] = out
        else:
            op_ref[...] = out

    @pl.when(s == 0)
    def _first_norm():
        norm()

    @pl.when(s < FF_TILES)
    def _stream():
        wa = wa_ref[...].astype(BF16)
        wb = wb_ref[...].astype(BF16)
        wa_s[s] = wa
        wb_s[s] = wb
        wd_s[pl.ds(pl.multiple_of(s * TN, TN), TN), :] = wd_ref[...].astype(BF16)
        v = up(wa, wb)
        for j in range(FF_TILES):
            @pl.when(s == j)
            def _store(j=j):
                act_s[:, j * TN:(j + 1) * TN] = v

    @pl.when((s >= FF_TILES) & (s < FF_STEPS - 1))
    def _chunk():
        out = down()
        norm()
        store(out)
        for j in range(FF_TILES):
            act_s[:, j * TN:(j + 1) * TN] = up(wa_s[j], wb_s[j])
        if last:
            regroup_new_cache()
        else:
            _mod_tile(cond_ref, wm_ref, bm_ref, mod_ref)

    @pl.when(s == FF_STEPS - 1)
    def _last_chunk_down():
        store(down())
        if last:
            regroup_new_cache()

    if last:
        @pl.when(s < FF_REGROUP_EARLY)
        def _regroup_early():
            regroup_new_cache()


def _ffn(x, g_ffn, modc, w_gate_up, w_down, l, cond8=None, w_mod=None, b_mod=None, new_kv=None):
    last = cond8 is None

    def chunk(s):
        return jnp.maximum(s - FF_TILES, 0)

    def up_chunk(s):
        return jnp.clip(s - (FF_TILES - 1), 0, FF_CHUNKS - 1)

    def tile(s):
        return jnp.minimum(s, FF_TILES - 1)

    def mod_tile(s):
        return jnp.clip(s - FF_TILES, 0, MOD_TILES - 1)

    def mod_spec(k, which):
        return pl.BlockSpec((None, 1, D_MODEL), lambda s: (_cond_row(which(s) * FF_ROWS // CHUNK), 0, k))

    in_specs = [pl.BlockSpec((FF_ROWS, D_MODEL), lambda s: (chunk(s), 0)),
                pl.BlockSpec((FF_ROWS, D_MODEL), lambda s: (up_chunk(s), 0)),
                pl.BlockSpec((None, 1, D_MODEL), lambda s: (l, 0, 0)),
                mod_spec(3, up_chunk), mod_spec(4, up_chunk), mod_spec(5, chunk),
                pl.BlockSpec((None, D_MODEL, TN), lambda s: (l, 0, tile(s))),
                pl.BlockSpec((None, D_MODEL, TN), lambda s: (l, 0, FF_TILES + tile(s))),
                pl.BlockSpec((None, TN, D_MODEL), lambda s: (l, tile(s), 0))]
    args = [x, x, g_ffn.reshape(DEPTH, 1, D_MODEL), modc, modc, modc, w_gate_up, w_gate_up, w_down]
    if last:
        def seq(s):
            return jnp.minimum(s, FF_REGROUP_EARLY - 1) + jnp.maximum(s - (FF_TILES - 1), 0)

        in_specs += [pl.BlockSpec((1, 4, KV_W, SEQ), lambda s: (seq(s), 0, 0, 0))] * DEPTH
        args += list(new_kv)
        out_specs = [pl.BlockSpec((FF_ROWS, D_MODEL), lambda s: (jnp.minimum(chunk(s), FF_P_CHUNKS - 1), 0)),
                     pl.BlockSpec((FF_ROWS, D_MODEL), lambda s: (jnp.maximum(chunk(s) - FF_P_CHUNKS, 0), 0))]
        out_shape = [jax.ShapeDtypeStruct((P_ROWS, D_MODEL), F32), jax.ShapeDtypeStruct((S_ROWS, D_MODEL), F32)]
        out_specs += [pl.BlockSpec((1, DEPTH, KV_W, SEQ), lambda s: (seq(s), 0, 0, 0))] * 4
        out_shape += [jax.ShapeDtypeStruct((BATCH, DEPTH, KV_W, SEQ), F32)] * 4
    else:
        in_specs += [pl.BlockSpec((COND_ROWS, D_MODEL), lambda s: (0, 0)),
                     pl.BlockSpec((None, D_MODEL, MOD_TN), lambda s: (l + 1, 0, mod_tile(s))),
                     pl.BlockSpec((None, 1, MOD_TN), lambda s: (l + 1, 0, mod_tile(s)))]
        args += [cond8, w_mod, b_mod.reshape(DEPTH, 1, N_MOD)]
        out_specs = [pl.BlockSpec((FF_ROWS, D_MODEL), lambda s: (chunk(s), 0)),
                     pl.BlockSpec((COND_ROWS, 1, MOD_TN), lambda s: (0, 0, mod_tile(s)))]
        out_shape = [jax.ShapeDtypeStruct((ROWS, D_MODEL), F32),
                     jax.ShapeDtypeStruct((COND_ROWS, 1, N_MOD), F32)]
    return pl.pallas_call(
        functools.partial(_ffn_kernel, last=last),
        grid=(FF_STEPS,),
        in_specs=in_specs,
        out_specs=out_specs,
        out_shape=out_shape,
        scratch_shapes=[pltpu.VMEM((FF_TILES, D_MODEL, TN), BF16),
                        pltpu.VMEM((FF_TILES, D_MODEL, TN), BF16),
                        pltpu.VMEM((D_FF, D_MODEL), BF16),
                        pltpu.VMEM((FF_ROWS, D_MODEL), BF16),
                        pltpu.VMEM((FF_ROWS, D_FF), BF16)],
        compiler_params=_params(),
        name="ffn",
    )(*args)


def _rope_tables():
    rows = DEC_SEQ // GRID_W
    row = np.repeat(np.arange(rows, dtype=np.float32), GRID_W)
    col = np.tile(np.arange(GRID_W, dtype=np.float32), rows)
    inv = (1.0 / (np.float32(ROPE_THETA) ** (np.arange(N_FREQ, dtype=np.float32) / N_FREQ))).astype(np.float32)
    ang_r, ang_c = row[:, None] * inv, col[:, None] * inv
    cr, sr, cc, sc = np.cos(ang_r), np.sin(ang_r), np.cos(ang_c), np.sin(ang_c)
    cos = np.concatenate([cr, cr, cc, cc], axis=1)
    sin = np.concatenate([-sr, sr, -sc, sc], axis=1)
    reps = LANES // HEAD_DIM
    return (jnp.asarray(np.tile(cos, (1, reps)), F32), jnp.asarray(np.tile(sin, (1, reps)), F32))


def kernel(x_prompt, x_sample, cache_k_attn, cache_v_attn, cache_k_win, cache_v_win, c, c_ctx,
           w_mod, b_mod, g_mix, g_ffn, w_in, gq_attn, gk_attn, gq_win, gk_win, sink_win,
           conv_w, pool_w, pool_scale, w_branch, w_out, w_gate_up, w_down):
    assert x_prompt.shape == (BATCH, SEQ, D_MODEL) and x_sample.shape == (DEC_BATCH, DEC_SEQ, D_MODEL)
    assert cache_k_attn.shape == (DEC_BATCH, DEPTH, PAST_LEN, N_KV_HEADS, HEAD_DIM)
    assert w_in.shape == (DEPTH, D_MODEL, (COL_GATE + N_BRANCH * D_MODEL // TN) * TN)
    assert w_gate_up.shape == (DEPTH, D_MODEL, 2 * D_FF) and w_down.shape == (DEPTH, D_FF, D_MODEL)
    assert all(a.dtype == F32 for a in (x_prompt, x_sample, cache_k_attn, w_in, w_mod, w_gate_up, w_down))
    cond8 = jnp.zeros((COND_ROWS, D_MODEL), F32).at[0].set(c_ctx).at[1:1 + DEC_BATCH].set(c)
    modc = _modulation(cond8, w_mod, b_mod, 0)

    cos, sin = _rope_tables()
    gains = jnp.tile(jnp.stack([gq_attn, gq_win, gk_attn, gk_win], axis=1), (1, 1, TN // HEAD_DIM))
    gains = gains.reshape(DEPTH, 4, 1, TN)
    caches = [a.transpose(0, 1, 3, 4, 2).reshape(DEC_BATCH, DEPTH, KV_W, PAST_LEN)
              for a in (cache_k_attn, cache_v_attn, cache_k_win, cache_v_win)]

    xs = (x_prompt.reshape(P_ROWS, D_MODEL), x_sample.reshape(S_ROWS, D_MODEL))
    new_kv = []
    for l in range(DEPTH):
        q, kv, conv, pool, new, *stacked = _proj(xs, g_mix, modc, w_in, gains, cos, sin,
                                                 conv_w, pool_w, pool_scale, l)
        x = stacked[0] if stacked else xs[0]
        new_kv.append(new)
        att = _attention(sink_win, q, kv, *caches, l)
        x = _mixer_out(x, g_mix, modc, att, conv, pool, w_in, w_branch, w_out, l)
        if l + 1 < DEPTH:
            x, modc_next = _ffn(x, g_ffn, modc, w_gate_up, w_down, l, cond8, w_mod, b_mod)
            modc = modc_next
            xs = (x,)
        else:
            y_prompt, y_sample, *by_kind = _ffn(x, g_ffn, modc, w_gate_up, w_down, l, new_kv=new_kv)

    outs = [a.reshape(BATCH, DEPTH, N_KV_HEADS, HEAD_DIM, SEQ).transpose(0, 1, 4, 2, 3) for a in by_kind]
    return (y_prompt.reshape(BATCH, SEQ, D_MODEL), y_sample.reshape(DEC_BATCH, DEC_SEQ, D_MODEL), *outs)
```

```python
import functools

import numpy as np
import jax
import jax.numpy as jnp
from jax import lax
from jax.experimental import pallas as pl
from jax.experimental.pallas import tpu as pltpu

D_MODEL = 1024
BATCH = 16
SEQ = 256
DEPTH = 4
DEC_BATCH = 2
DEC_SEQ = 1024
PAST_LEN = 512
GRID_W = 64
HEAD_DIM = 64
N_Q_HEADS = 8
N_KV_HEADS = 2
ATT_W = N_Q_HEADS * HEAD_DIM
KV_W = N_KV_HEADS * HEAD_DIM
N_FREQ = HEAD_DIM // 4
ROPE_THETA = 10000.0
CONV_W = 512
POOL_W = 512
POOL_SIZES = (2, 4, 8, 16)
N_BRANCH = 4
BRANCH_W = 512
D_FF = 2816
WINDOW = 128
EPS = 1e-6
NEG = -1e30

P_ROWS = BATCH * SEQ
S_ROWS = DEC_BATCH * DEC_SEQ
ROWS = P_ROWS + S_ROWS
CHUNK = 1024
N_CHUNKS = ROWS // CHUNK
N_P_CHUNKS = P_ROWS // CHUNK
SEQ_PER_CHUNK = CHUNK // SEQ
TN = 256
LANES = 128
VMEM_LIMIT = 58 * 1024 * 1024

COL_QA, COL_KVA, COL_QC, COL_KVC = 0, 2, 3, 5
COL_U, COL_GB, COL_GC, COL_PV, COL_GATE = 6, 8, 10, 12, 14

F32 = jnp.float32
BF16 = jnp.bfloat16

assert SEQ & (SEQ - 1) == 0 and CHUNK % SEQ == 0 and DEC_SEQ == CHUNK and P_ROWS % CHUNK == 0
assert HEAD_DIM * 2 == LANES and KV_W == LANES and ATT_W % TN == 0 and POOL_W == LANES * len(POOL_SIZES)
assert N_Q_HEADS // N_KV_HEADS == 4 and COL_GATE * TN == 2 * ATT_W + 4 * KV_W + 3 * CONV_W + POOL_W


def _params():
    return pltpu.CompilerParams(dimension_semantics=("arbitrary",), vmem_limit_bytes=VMEM_LIMIT)


def _dot(a, b):
    return jnp.dot(a, b, preferred_element_type=F32)


def _dot_nt(a, b):
    return lax.dot_general(a, b, (((1,), (1,)), ((), ())), preferred_element_type=F32)


def _sigmoid(x):
    return 1.0 / (1.0 + jnp.exp(-x))


def _norm_modulate(x, g, scale, shift):
    y = x * lax.rsqrt(jnp.mean(x * x, axis=-1, keepdims=True) + EPS)
    return ((y * g) * (1.0 + scale) + shift).astype(BF16)


def _head_segments(width):
    r = lax.broadcasted_iota(jnp.int32, (width, width), 0) // HEAD_DIM
    c = lax.broadcasted_iota(jnp.int32, (width, width), 1) // HEAD_DIM
    return jnp.where(r == c, 1.0 / HEAD_DIM, 0.0).astype(BF16)


def _head_mean_square(y, seg):
    return _dot((y * y).astype(BF16), seg)


def _swap16(x):
    lane = lax.broadcasted_iota(jnp.int32, x.shape, 1)
    fwd = pltpu.roll(x, LANES - 16, axis=1)
    bwd = pltpu.roll(x, 16, axis=1)
    return jnp.where((lane & 16) == 0, fwd, bwd)


def _rope128(x, cos, sin):
    return x * cos + _swap16(x) * sin


N_MOD = 6 * D_MODEL
COND_ROWS = 8
MOD_TN = 768
MOD_TILES = N_MOD // MOD_TN


def _cond_row(chunk):
    return jnp.maximum(chunk - (N_P_CHUNKS - 1), 0)


def _mod_tile(cond_ref, w_ref, b_ref, o_ref):
    c = cond_ref[...]
    s = (c * _sigmoid(c)).astype(BF16)
    res = _dot(s, w_ref[...].astype(BF16)) + b_ref[...]
    for r in range(COND_ROWS):
        o_ref[r] = res[r:r + 1, :]


def _modulation(cond8, w_mod, b_mod, l):
    tn = 2 * MOD_TN
    return pl.pallas_call(
        _mod_tile,
        grid=(N_MOD // tn,),
        in_specs=[pl.BlockSpec((COND_ROWS, D_MODEL), lambda j: (0, 0)),
                  pl.BlockSpec((None, D_MODEL, tn), lambda j: (l, 0, j)),
                  pl.BlockSpec((None, 1, tn), lambda j: (l, 0, j))],
        out_specs=pl.BlockSpec((COND_ROWS, 1, tn), lambda j: (0, 0, j)),
        out_shape=jax.ShapeDtypeStruct((COND_ROWS, 1, N_MOD), F32),
        compiler_params=_params(),
        name="modulation",
    )(cond8, w_mod, b_mod.reshape(DEPTH, 1, N_MOD))


PJ_TILES = COL_GATE


def _proj_loads(first):
    per_load = 1 if first else 7
    return per_load, PJ_TILES // per_load


def _seq_pos(latent, width):
    row = lax.broadcasted_iota(jnp.int32, (CHUNK, width), 0)
    return row if latent else row & (SEQ - 1)


def _shift_rows(x, k, pos, seq_len):
    rolled = pltpu.roll(x, (-k) % CHUNK, axis=0)
    ok = (pos + k >= 0) & (pos + k < seq_len)
    return jnp.where(ok, rolled, 0.0)


def _window_mean_minus_token(p, half, pos, seq_len):
    fwd = p
    bwd = _shift_rows(p, -1, pos, seq_len)
    m = 1
    while m < half:
        fwd = fwd + _shift_rows(fwd, m, pos, seq_len)
        bwd = bwd + _shift_rows(bwd, -m, pos, seq_len)
        m *= 2
    count = jnp.minimum(pos + half, seq_len) - jnp.maximum(pos - half, 0)
    return (fwd + bwd) / count.astype(F32) - p


def _proj_chunk(latent, x_ref, g_ref, sh_ref, sc_ref, gain_ref, cos_ref, sin_ref, cw_ref, pw_ref,
                ps_ref, q_ref, kv_ref, conv_ref, pool_ref, new_ref, w_s, h_s):
    h_s[...] = _norm_modulate(x_ref[...], g_ref[...], sc_ref[...], sh_ref[...])
    seq_len = DEC_SEQ if latent else SEQ
    seg_q = _head_segments(TN)
    seg_k = _head_segments(KV_W)
    pos_wide = _seq_pos(latent, TN)
    pos = _seq_pos(latent, LANES)

    def q_tile(mixer, j, y):
        y = y * lax.rsqrt(_head_mean_square(y, seg_q) + EPS) * (gain_ref[mixer] * (HEAD_DIM ** -0.5))
        for s in range(TN // LANES):
            part = y[:, s * LANES:(s + 1) * LANES]
            if latent:
                part = _rope128(part, cos_ref[...], sin_ref[...])
            out_col = mixer * ATT_W + j * TN + s * LANES
            q_ref[:, out_col:out_col + LANES] = part.astype(BF16)

    def kv_tile(mixer, y):
        k = y[:, :KV_W]
        v = y[:, KV_W:]
        k = k * lax.rsqrt(_head_mean_square(k, seg_k) + EPS) * gain_ref[2 + mixer][:, :KV_W]
        if latent:
            k = _rope128(k, cos_ref[...], sin_ref[...])
        else:
            for b in range(SEQ_PER_CHUNK):
                new_ref[b, 2 * mixer] = k[b * SEQ:(b + 1) * SEQ, :].T
                new_ref[b, 2 * mixer + 1] = v[b * SEQ:(b + 1) * SEQ, :].T
        kv_ref[:, 2 * mixer * KV_W:(2 * mixer + 1) * KV_W] = k.astype(BF16)
        kv_ref[:, (2 * mixer + 1) * KV_W:(2 * mixer + 2) * KV_W] = v.astype(BF16)

    def conv_tile(j, gc, u, gb):
        cols = slice(j * TN, (j + 1) * TN)
        z = gc * u
        y = (_shift_rows(z, -1, pos_wide, seq_len) * cw_ref[0:1, cols] + z * cw_ref[1:2, cols]
             + _shift_rows(z, 1, pos_wide, seq_len) * cw_ref[2:3, cols])
        conv_ref[:, cols] = (gb * y).astype(BF16)

    def pool_tile(j, p):
        g0 = j * (TN // LANES)
        ys = [_window_mean_minus_token(p[:, s * LANES:(s + 1) * LANES], POOL_SIZES[g0 + s] // 2, pos, seq_len)
              for s in range(TN // LANES)]
        zero = jnp.zeros((LANES, LANES), BF16)
        w = jnp.concatenate([jnp.concatenate([pw_ref[g0].astype(BF16), zero], axis=1),
                             jnp.concatenate([zero, pw_ref[g0 + 1].astype(BF16)], axis=1)], axis=0)
        cols = slice(j * TN, (j + 1) * TN)
        y = jnp.concatenate(ys, axis=1).astype(BF16)
        pool_ref[:, cols] = (_dot(y, w) * ps_ref[:, cols]).astype(BF16)

    tasks = []
    for j in range(POOL_W // TN):
        tasks.append(((COL_PV + j,), functools.partial(pool_tile, j)))
        tasks.append(((COL_GC + j, COL_U + j, COL_GB + j), functools.partial(conv_tile, j)))
    for mixer, col in enumerate((COL_QA, COL_QC)):
        for j in range(ATT_W // TN):
            tasks.append(((col + j,), functools.partial(q_tile, mixer, j)))
    for mixer, col in enumerate((COL_KVA, COL_KVC)):
        tasks.append(((col,), functools.partial(kv_tile, mixer)))

    def matmuls(tiles):
        return [_dot(h_s[...], w_s[t]) for t in tiles]

    ys = matmuls(tasks[0][0])
    for i, (_, epilogue) in enumerate(tasks):
        nxt = matmuls(tasks[i + 1][0]) if i + 1 < len(tasks) else None
        epilogue(*ys)
        ys = nxt


def _proj_kernel(*refs, first):
    if first:
        xp_ref, xs_ref, *refs = refs
    else:
        xp_ref = xs_ref = refs[0]
        refs = refs[1:]
    (g_ref, sh_ref, sc_ref, w_ref, gain_ref, cos_ref, sin_ref, cw_ref, pw_ref, ps_ref,
     q_ref, kv_ref, conv_ref, pool_ref, new_ref, *refs) = refs
    xcat_ref = refs[0] if first else None
    w_s, h_s = refs[-2:]
    per_load, n_loads = _proj_loads(first)
    s = pl.program_id(0)
    rest = (g_ref, sh_ref, sc_ref, gain_ref, cos_ref, sin_ref, cw_ref, pw_ref, ps_ref,
            q_ref, kv_ref, conv_ref, pool_ref, new_ref, w_s, h_s)

    @pl.when(s < n_loads)
    def _stream():
        for j in range(per_load):
            w_s[per_load * s + j] = w_ref[:, j * TN:(j + 1) * TN].astype(BF16)

    @pl.when((s >= n_loads) & (s < n_loads + N_P_CHUNKS))
    def _prompt():
        if first:
            xcat_ref[...] = xp_ref[...]
        _proj_chunk(False, xp_ref, *rest)

    @pl.when(s >= n_loads + N_P_CHUNKS)
    def _latent():
        if first:
            xcat_ref[...] = xs_ref[...]
        _proj_chunk(True, xs_ref, *rest)


def _proj(xs, g_mix, modc, w_in, gains, cos, sin, conv_w, pool_w, pool_scale, l):
    first = len(xs) == 2
    per_load, n_loads = _proj_loads(first)

    def chunk(s):
        return jnp.clip(s - n_loads, 0, N_CHUNKS - 1)

    def p_chunk(s):
        return jnp.clip(s - n_loads, 0, N_P_CHUNKS - 1)

    def s_chunk(s):
        return jnp.clip(s - n_loads - N_P_CHUNKS, 0, N_CHUNKS - N_P_CHUNKS - 1)

    def mod_spec(k):
        return pl.BlockSpec((None, 1, D_MODEL), lambda s: (_cond_row(chunk(s)), 0, k))

    def const(shape):
        return pl.BlockSpec(shape, lambda s: (l,) + (0,) * (len(shape) - 1))

    def rows_out(width):
        return pl.BlockSpec((CHUNK, width), lambda s: (chunk(s), 0))

    if first:
        x_specs = [pl.BlockSpec((CHUNK, D_MODEL), lambda s: (p_chunk(s), 0)),
                   pl.BlockSpec((CHUNK, D_MODEL), lambda s: (s_chunk(s), 0))]
    else:
        x_specs = [rows_out(D_MODEL)]
    out_specs = [rows_out(2 * ATT_W), rows_out(4 * KV_W), rows_out(CONV_W), rows_out(POOL_W),
                 pl.BlockSpec((SEQ_PER_CHUNK, 4, KV_W, SEQ), lambda s: (p_chunk(s), 0, 0, 0))]
    out_shape = [jax.ShapeDtypeStruct((ROWS, 2 * ATT_W), BF16),
                 jax.ShapeDtypeStruct((ROWS, 4 * KV_W), BF16),
                 jax.ShapeDtypeStruct((ROWS, CONV_W), BF16),
                 jax.ShapeDtypeStruct((ROWS, POOL_W), BF16),
                 jax.ShapeDtypeStruct((BATCH, 4, KV_W, SEQ), F32)]
    if first:
        out_specs.append(rows_out(D_MODEL))
        out_shape.append(jax.ShapeDtypeStruct((ROWS, D_MODEL), F32))
    return pl.pallas_call(
        functools.partial(_proj_kernel, first=first),
        grid=(n_loads + N_CHUNKS,),
        in_specs=x_specs + [
            const((None, 1, D_MODEL)), mod_spec(0), mod_spec(1),
            pl.BlockSpec((None, D_MODEL, per_load * TN), lambda s: (l, 0, jnp.minimum(s, n_loads - 1))),
            const((None, 4, 1, TN)),
            pl.BlockSpec((DEC_SEQ, LANES), lambda s: (0, 0)),
            pl.BlockSpec((DEC_SEQ, LANES), lambda s: (0, 0)),
            const((None, 3, CONV_W)), const((None, len(POOL_SIZES), LANES, LANES)),
            const((None, 1, POOL_W))],
        out_specs=out_specs,
        out_shape=out_shape,
        scratch_shapes=[pltpu.VMEM((PJ_TILES, D_MODEL, TN), BF16),
                        pltpu.VMEM((CHUNK, D_MODEL), BF16)],
        compiler_params=_params(),
        name="proj",
    )(*xs, g_mix.reshape(DEPTH, 1, D_MODEL), modc, modc, w_in, gains, cos, sin,
      conv_w, pool_w, pool_scale.reshape(DEPTH, 1, POOL_W))


TQ = 256
N_QT = DEC_SEQ // TQ
WIN_KEYS = TQ + 2 * WINDOW
PAD_SEQ = WINDOW + DEC_SEQ + WINDOW


def _split_lanes(x, ones=False):
    xr = pltpu.roll(x, HEAD_DIM, axis=1)
    lane = lax.broadcasted_iota(jnp.int32, x.shape, 1)
    low = lane < HEAD_DIM
    fill_lo = jnp.where(lane == HEAD_DIM, 1.0, 0.0) if ones else 0.0
    fill_hi = jnp.where(lane == 0, 1.0, 0.0) if ones else 0.0
    return (jnp.where(low, x, fill_lo).astype(BF16), jnp.where(low, fill_hi, xr).astype(BF16),
            jnp.where(low, xr, fill_lo).astype(BF16), jnp.where(low, fill_hi, x).astype(BF16))


def _split_rows(xt, ones=False):
    z = jnp.zeros((HEAD_DIM, xt.shape[1]), F32)
    if ones:
        z = jnp.where(lax.broadcasted_iota(jnp.int32, z.shape, 0) == 0, 1.0, z)
    h0, h1 = xt[:HEAD_DIM], xt[HEAD_DIM:]
    return tuple(jnp.concatenate(p, axis=0).astype(BF16) for p in ((h0, z), (z, h0), (h1, z), (z, h1)))


def _attend_pairs(tasks, denom_from_values=False):
    units = [(t, parity) for t in range(len(tasks)) for parity in range(2)]

    def scores_of(unit):
        t, parity = unit
        q, pieces, _, _ = tasks[t]
        out = []
        for k_lo, k_hi, k_t, _, _, _, mask in pieces:
            k = k_hi if parity else k_lo
            s = _dot(q, k) if k_t else _dot_nt(q, k)
            if mask is not None:
                s = jnp.where(mask, s, NEG)
            out.append(s)
        return out

    scores = scores_of(units[0])
    even = None
    for i, (t, parity) in enumerate(units):
        nxt = scores_of(units[i + 1]) if i + 1 < len(units) else None
        _, pieces, sinks, write = tasks[t]
        m = functools.reduce(jnp.maximum, [jnp.max(s, axis=1, keepdims=True) for s in scores])
        if sinks is not None:
            m = jnp.maximum(m, sinks[parity])
        acc = None
        denom = None
        for s, (_, _, _, v_lo, v_hi, v_t, _) in zip(scores, pieces):
            p = jnp.exp(s - m)
            if not denom_from_values:
                d = jnp.sum(p, axis=1, keepdims=True)
                denom = d if denom is None else denom + d
            v = v_hi if parity else v_lo
            a = _dot_nt(p.astype(BF16), v) if v_t else _dot(p.astype(BF16), v)
            acc = a if acc is None else acc + a
        if denom_from_values:
            ones_lane = 0 if parity else HEAD_DIM
            denom = acc[:, ones_lane:ones_lane + 1]
        if sinks is not None:
            denom = denom + jnp.exp(sinks[parity] - m)
        acc = acc / denom
        if denom_from_values:
            lane = lax.broadcasted_iota(jnp.int32, acc.shape, 1)
            acc = jnp.where((lane >= HEAD_DIM) if parity else (lane < HEAD_DIM), acc, 0.0)
        if parity == 0:
            even = acc
        else:
            write(even + acc)
        scores = nxt


def _attn_prompt_seq(rows, sink_ref, q_ref, kv_ref, o_ref, layer):
    def writer(col):
        def write(out):
            o_ref[rows, col:col + LANES] = out.astype(BF16)
        return write

    tasks = []
    for mixer in range(2):
        k = _split_lanes(kv_ref[rows, 2 * mixer * KV_W:(2 * mixer + 1) * KV_W].astype(F32))
        v = _split_lanes(kv_ref[rows, (2 * mixer + 1) * KV_W:(2 * mixer + 2) * KV_W].astype(F32))
        for pair in range(N_Q_HEADS // 2):
            lo = 2 * (pair // 2)
            piece = (k[lo], k[lo + 1], False, v[lo], v[lo + 1], False, None)
            sinks = None
            if mixer == 1:
                sinks = (sink_ref[layer, 2 * pair], sink_ref[layer, 2 * pair + 1])
            col = mixer * ATT_W + pair * LANES
            tasks.append((q_ref[rows, col:col + LANES], [piece], sinks, writer(col)))
    _attend_pairs(tasks)


def _attn_latent_fill(kv_ref, cka_ref, cva_ref, ckc_ref, cvc_ref, ctx_s, ka_s, va_s, kc_s, vc_s):
    for i, ref in enumerate((cka_ref, cva_ref, ckc_ref, cvc_ref)):
        for j, part in enumerate(_split_rows(ref[...], ones=i % 2 == 1)):
            ctx_s[4 * i + j] = part
    for dst, col in ((ka_s, 0), (va_s, KV_W)):
        for j, part in enumerate(_split_lanes(kv_ref[:, col:col + KV_W].astype(F32), ones=dst is va_s)):
            dst[j] = part
    zeros = jnp.zeros((WINDOW, LANES), BF16)
    for dst, col in ((kc_s, 2 * KV_W), (vc_s, 3 * KV_W)):
        for j, part in enumerate(_split_lanes(kv_ref[:, col:col + KV_W].astype(F32), ones=dst is vc_s)):
            dst[j, 0:WINDOW, :] = zeros
            dst[j, WINDOW:WINDOW + DEC_SEQ, :] = part
            dst[j, WINDOW + DEC_SEQ:PAD_SEQ, :] = zeros


def _attn_latent_tile(qt, sink_ref, q_ref, o_ref, ctx_s, ka_s, va_s, kc_s, vc_s, layer):
    q0 = qt * TQ
    rows = pl.ds(pl.multiple_of(q0, TQ), TQ)
    win = pl.ds(pl.multiple_of(q0, TQ), WIN_KEYS)
    r = lax.broadcasted_iota(jnp.int32, (TQ, WIN_KEYS), 0)
    jk = lax.broadcasted_iota(jnp.int32, (TQ, WIN_KEYS), 1)
    kpos = q0 - WINDOW + jk
    band = jnp.where((jk - r >= 0) & (jk - r <= 2 * WINDOW), kpos, -1)
    mask = (band >= 0) & (band < DEC_SEQ)

    def writer(col):
        def write(out):
            o_ref[rows, col:col + LANES] = out.astype(BF16)
        return write

    tasks = []
    for pair in range(N_Q_HEADS // 2):
        lo = 2 * (pair // 2)
        hi = lo + 1
        col = pair * LANES
        ctx = (ctx_s[lo], ctx_s[hi], True, ctx_s[4 + lo], ctx_s[4 + hi], True, None)
        cur = (ka_s[lo], ka_s[hi], False, va_s[lo], va_s[hi], False, None)
        tasks.append((q_ref[rows, col:col + LANES], [ctx, cur], None, writer(col)))

        col = ATT_W + pair * LANES
        ctx = (ctx_s[8 + lo], ctx_s[8 + hi], True, ctx_s[12 + lo], ctx_s[12 + hi], True, None)
        near = (kc_s[lo, win, :], kc_s[hi, win, :], False, vc_s[lo, win, :], vc_s[hi, win, :], False, mask)
        sinks = (sink_ref[layer, 2 * pair], sink_ref[layer, 2 * pair + 1])
        tasks.append((q_ref[rows, col:col + LANES], [ctx, near], sinks, writer(col)))
    _attend_pairs(tasks, denom_from_values=True)


def _attn_kernel(sink_ref, q_ref, kv_ref, cka_ref, cva_ref, ckc_ref, cvc_ref, o_ref,
                 ctx_s, ka_s, va_s, kc_s, vc_s, *, layer):
    s = pl.program_id(0)

    @pl.when(s < N_P_CHUNKS)
    def _prompt():
        def seq(b, carry):
            _attn_prompt_seq(pl.ds(pl.multiple_of(b * SEQ, SEQ), SEQ), sink_ref, q_ref, kv_ref, o_ref, layer)
            return carry
        lax.fori_loop(0, SEQ_PER_CHUNK, seq, 0)

    @pl.when(s >= N_P_CHUNKS)
    def _latent():
        _attn_latent_fill(kv_ref, cka_ref, cva_ref, ckc_ref, cvc_ref, ctx_s, ka_s, va_s, kc_s, vc_s)

        def tile(qt, carry):
            _attn_latent_tile(qt, sink_ref, q_ref, o_ref, ctx_s, ka_s, va_s, kc_s, vc_s, layer)
            return carry
        lax.fori_loop(0, N_QT, tile, 0)


def _attention(sink, q, kv, cka, cva, ckc, cvc, l):
    cache_spec = pl.BlockSpec((None, None, KV_W, PAST_LEN),
                              lambda s: (jnp.maximum(s - N_P_CHUNKS, 0), l, 0, 0))
    return pl.pallas_call(
        functools.partial(_attn_kernel, layer=l),
        grid=(N_CHUNKS,),
        in_specs=[pl.BlockSpec(memory_space=pltpu.SMEM),
                  pl.BlockSpec((CHUNK, 2 * ATT_W), lambda s: (s, 0)),
                  pl.BlockSpec((CHUNK, 4 * KV_W), lambda s: (s, 0)),
                  cache_spec, cache_spec, cache_spec, cache_spec],
        out_specs=pl.BlockSpec((CHUNK, 2 * ATT_W), lambda s: (s, 0)),
        out_shape=jax.ShapeDtypeStruct((ROWS, 2 * ATT_W), BF16),
        scratch_shapes=[pltpu.VMEM((16, KV_W, PAST_LEN), BF16),
                        pltpu.VMEM((4, DEC_SEQ, LANES), BF16), pltpu.VMEM((4, DEC_SEQ, LANES), BF16),
                        pltpu.VMEM((4, PAD_SEQ, LANES), BF16), pltpu.VMEM((4, PAD_SEQ, LANES), BF16)],
        compiler_params=_params(),
        name="attention",
    )(sink, q, kv, cka, cva, ckc, cvc)


MG_ROWS = 512
MG_CHUNKS = ROWS // MG_ROWS
MG_TILES = D_MODEL // TN
MG_STEPS = MG_TILES + MG_CHUNKS


def _mixer_out_kernel(x_ref, xn_ref, g_ref, sh_ref, sc_ref, gate_ref, att_ref, conv_ref, pool_ref,
                      wg0, wg1, wg2, wg3, wb0, wb1, wb2, wb3, wo_ref, o_ref,
                      wg_s, wb_s, wo_s, h_s, mix_s):
    s = pl.program_id(0)

    def norm():
        h_s[...] = _norm_modulate(xn_ref[...], g_ref[...], sc_ref[...], sh_ref[...])

    @pl.when(s == 0)
    def _first_norm():
        norm()

    def mix_tile(wgs, wbs):
        h = h_s[...]
        branches = (att_ref[:, :ATT_W], conv_ref[...], att_ref[:, ATT_W:], pool_ref[...])
        acc = None
        for br, wg, wb in zip(branches, wgs, wbs):
            term = _sigmoid(_dot(h, wg)) * _dot(br, wb)
            acc = term if acc is None else acc + term
        return acc.astype(BF16)

    def out():
        o_ref[...] = x_ref[...] + gate_ref[...] * _dot(mix_s[...], wo_s[...])

    @pl.when(s < MG_TILES)
    def _stream():
        wgs = [r[...].astype(BF16) for r in (wg0, wg1, wg2, wg3)]
        wbs = [r[...].astype(BF16) for r in (wb0, wb1, wb2, wb3)]
        for k in range(N_BRANCH):
            wg_s[s * N_BRANCH + k] = wgs[k]
            wb_s[s * N_BRANCH + k] = wbs[k]
        wo_s[pl.ds(pl.multiple_of(s * TN, TN), TN), :] = wo_ref[...].astype(BF16)
        v = mix_tile(wgs, wbs)
        for n in range(MG_TILES):
            @pl.when(s == n)
            def _store(n=n):
                mix_s[:, n * TN:(n + 1) * TN] = v

    @pl.when((s >= MG_TILES) & (s < MG_STEPS - 1))
    def _chunk():
        out()
        norm()
        for n in range(MG_TILES):
            mix_s[:, n * TN:(n + 1) * TN] = mix_tile(
                [wg_s[n * N_BRANCH + k] for k in range(N_BRANCH)],
                [wb_s[n * N_BRANCH + k] for k in range(N_BRANCH)])

    @pl.when(s == MG_STEPS - 1)
    def _last_chunk_out():
        out()


def _mixer_out(x, g_mix, modc, att, conv, pool, w_in, w_branch, w_out, l):
    def chunk(s):
        return jnp.maximum(s - MG_TILES, 0)

    def mix_chunk(s):
        return jnp.clip(s - (MG_TILES - 1), 0, MG_CHUNKS - 1)

    def tile(s):
        return jnp.minimum(s, MG_TILES - 1)

    def mod_spec(k, which):
        return pl.BlockSpec((None, 1, D_MODEL), lambda s: (_cond_row(which(s) * MG_ROWS // CHUNK), 0, k))

    def rows_in(width):
        return pl.BlockSpec((MG_ROWS, width), lambda s: (mix_chunk(s), 0))

    gate_specs = [pl.BlockSpec((None, D_MODEL, TN), functools.partial(
        lambda s, k: (l, 0, COL_GATE + MG_TILES * k + tile(s)), k=k)) for k in range(N_BRANCH)]
    br_specs = [pl.BlockSpec((None, None, BRANCH_W, TN), functools.partial(
        lambda s, k: (l, k, 0, tile(s)), k=k)) for k in range(N_BRANCH)]
    return pl.pallas_call(
        _mixer_out_kernel,
        grid=(MG_STEPS,),
        in_specs=[pl.BlockSpec((MG_ROWS, D_MODEL), lambda s: (chunk(s), 0)),
                  pl.BlockSpec((MG_ROWS, D_MODEL), lambda s: (mix_chunk(s), 0)),
                  pl.BlockSpec((None, 1, D_MODEL), lambda s: (l, 0, 0)),
                  mod_spec(0, mix_chunk), mod_spec(1, mix_chunk), mod_spec(2, chunk),
                  rows_in(2 * ATT_W), rows_in(CONV_W), rows_in(POOL_W)] + gate_specs + br_specs
                 + [pl.BlockSpec((None, TN, D_MODEL), lambda s: (l, tile(s), 0))],
        out_specs=pl.BlockSpec((MG_ROWS, D_MODEL), lambda s: (chunk(s), 0)),
        out_shape=jax.ShapeDtypeStruct((ROWS, D_MODEL), F32),
        scratch_shapes=[pltpu.VMEM((MG_TILES * N_BRANCH, D_MODEL, TN), BF16),
                        pltpu.VMEM((MG_TILES * N_BRANCH, BRANCH_W, TN), BF16),
                        pltpu.VMEM((D_MODEL, D_MODEL), BF16),
                        pltpu.VMEM((MG_ROWS, D_MODEL), BF16),
                        pltpu.VMEM((MG_ROWS, D_MODEL), BF16)],
        compiler_params=_params(),
        name="mixer_out",
    )(x, x, g_mix.reshape(DEPTH, 1, D_MODEL), modc, modc, modc, att, conv, pool,
      *([w_in] * N_BRANCH), *([w_branch] * N_BRANCH), w_out)


FF_ROWS = 512
FF_CHUNKS = ROWS // FF_ROWS
FF_P_CHUNKS = P_ROWS // FF_ROWS
FF_TILES = D_FF // TN
FF_STEPS = FF_TILES + FF_CHUNKS
FF_FIRST_LATENT_STEP = FF_TILES + FF_P_CHUNKS
assert MOD_TILES <= FF_CHUNKS - 1
FF_REGROUP_EARLY = BATCH - FF_CHUNKS
assert 0 <= FF_REGROUP_EARLY <= FF_TILES


def _ffn_kernel(x_ref, xn_ref, g_ref, sh_ref, sc_ref, gate_ref, wa_ref, wb_ref, wd_ref, *rest, last):
    if last:
        new_refs = rest[:DEPTH]
        op_ref, os_ref = rest[DEPTH:DEPTH + 2]
        kind_refs = rest[DEPTH + 2:DEPTH + 6]
        wa_s, wb_s, wd_s, h_s, act_s = rest[DEPTH + 6:]
    else:
        cond_ref, wm_ref, bm_ref, op_ref, mod_ref, wa_s, wb_s, wd_s, h_s, act_s = rest
        os_ref = op_ref
    s = pl.program_id(0)

    def norm():
        h_s[...] = _norm_modulate(xn_ref[...], g_ref[...], sc_ref[...], sh_ref[...])

    def up(wa, wb):
        h = h_s[...]
        a = _dot(h, wa)
        return (a * _sigmoid(a) * _dot(h, wb)).astype(BF16)

    def down():
        return x_ref[...] + gate_ref[...] * _dot(act_s[...], wd_s[...])

    def regroup_new_cache():
        for kind, dst in enumerate(kind_refs):
            for depth, src in enumerate(new_refs):
                dst[0, depth] = src[0, kind]

    def store(out):
        if last:
            @pl.when(s < FF_FIRST_LATENT_STEP)
            def _store_prompt():
                op_ref[...] = out

            @pl.when(s >= FF_FIRST_LATENT_STEP)
            def _store_latent():
                os_ref[...] = out
        else:
            op_ref[...] = out

    @pl.when(s == 0)
    def _first_norm():
        norm()

    @pl.when(s < FF_TILES)
    def _stream():
        wa = wa_ref[...].astype(BF16)
        wb = wb_ref[...].astype(BF16)
        wa_s[s] = wa
        wb_s[s] = wb
        wd_s[pl.ds(pl.multiple_of(s * TN, TN), TN), :] = wd_ref[...].astype(BF16)
        v = up(wa, wb)
        for j in range(FF_TILES):
            @pl.when(s == j)
            def _store(j=j):
                act_s[:, j * TN:(j + 1) * TN] = v

    @pl.when((s >= FF_TILES) & (s < FF_STEPS - 1))
    def _chunk():
        out = down()
        norm()
        store(out)
        for j in range(FF_TILES):
            act_s[:, j * TN:(j + 1) * TN] = up(wa_s[j], wb_s[j])
        if last:
            regroup_new_cache()
        else:
            _mod_tile(cond_ref, wm_ref, bm_ref, mod_ref)

    @pl.when(s == FF_STEPS - 1)
    def _last_chunk_down():
        store(down())
        if last:
            regroup_new_cache()

    if last:
        @pl.when(s < FF_REGROUP_EARLY)
        def _regroup_early():
            regroup_new_cache()


def _ffn(x, g_ffn, modc, w_gate_up, w_down, l, cond8=None, w_mod=None, b_mod=None, new_kv=None):
    last = cond8 is None

    def chunk(s):
        return jnp.maximum(s - FF_TILES, 0)

    def up_chunk(s):
        return jnp.clip(s - (FF_TILES - 1), 0, FF_CHUNKS - 1)

    def tile(s):
        return jnp.minimum(s, FF_TILES - 1)

    def mod_tile(s):
        return jnp.clip(s - FF_TILES, 0, MOD_TILES - 1)

    def mod_spec(k, which):
        return pl.BlockSpec((None, 1, D_MODEL), lambda s: (_cond_row(which(s) * FF_ROWS // CHUNK), 0, k))

    in_specs = [pl.BlockSpec((FF_ROWS, D_MODEL), lambda s: (chunk(s), 0)),
                pl.BlockSpec((FF_ROWS, D_MODEL), lambda s: (up_chunk(s), 0)),
                pl.BlockSpec((None, 1, D_MODEL), lambda s: (l, 0, 0)),
                mod_spec(3, up_chunk), mod_spec(4, up_chunk), mod_spec(5, chunk),
                pl.BlockSpec((None, D_MODEL, TN), lambda s: (l, 0, tile(s))),
                pl.BlockSpec((None, D_MODEL, TN), lambda s: (l, 0, FF_TILES + tile(s))),
                pl.BlockSpec((None, TN, D_MODEL), lambda s: (l, tile(s), 0))]
    args = [x, x, g_ffn.reshape(DEPTH, 1, D_MODEL), modc, modc, modc, w_gate_up, w_gate_up, w_down]
    if last:
        def seq(s):
            return jnp.minimum(s, FF_REGROUP_EARLY - 1) + jnp.maximum(s - (FF_TILES - 1), 0)

        in_specs += [pl.BlockSpec((1, 4, KV_W, SEQ), lambda s: (seq(s), 0, 0, 0))] * DEPTH
        args += list(new_kv)
        out_specs = [pl.BlockSpec((FF_ROWS, D_MODEL), lambda s: (jnp.minimum(chunk(s), FF_P_CHUNKS - 1), 0)),
                     pl.BlockSpec((FF_ROWS, D_MODEL), lambda s: (jnp.maximum(chunk(s) - FF_P_CHUNKS, 0), 0))]
        out_shape = [jax.ShapeDtypeStruct((P_ROWS, D_MODEL), F32), jax.ShapeDtypeStruct((S_ROWS, D_MODEL), F32)]
        out_specs += [pl.BlockSpec((1, DEPTH, KV_W, SEQ), lambda s: (seq(s), 0, 0, 0))] * 4
        out_shape += [jax.ShapeDtypeStruct((BATCH, DEPTH, KV_W, SEQ), F32)] * 4
    else:
        in_specs += [pl.BlockSpec((COND_ROWS, D_MODEL), lambda s: (0, 0)),
                     pl.BlockSpec((None, D_MODEL, MOD_TN), lambda s: (l + 1, 0, mod_tile(s))),
                     pl.BlockSpec((None, 1, MOD_TN), lambda s: (l + 1, 0, mod_tile(s)))]
        args += [cond8, w_mod, b_mod.reshape(DEPTH, 1, N_MOD)]
        out_specs = [pl.BlockSpec((FF_ROWS, D_MODEL), lambda s: (chunk(s), 0)),
                     pl.BlockSpec((COND_ROWS, 1, MOD_TN), lambda s: (0, 0, mod_tile(s)))]
        out_shape = [jax.ShapeDtypeStruct((ROWS, D_MODEL), F32),
                     jax.ShapeDtypeStruct((COND_ROWS, 1, N_MOD), F32)]
    return pl.pallas_call(
        functools.partial(_ffn_kernel, last=last),
        grid=(FF_STEPS,),
        in_specs=in_specs,
        out_specs=out_specs,
        out_shape=out_shape,
        scratch_shapes=[pltpu.VMEM((FF_TILES, D_MODEL, TN), BF16),
                        pltpu.VMEM((FF_TILES, D_MODEL, TN), BF16),
                        pltpu.VMEM((D_FF, D_MODEL), BF16),
                        pltpu.VMEM((FF_ROWS, D_MODEL), BF16),
                        pltpu.VMEM((FF_ROWS, D_FF), BF16)],
        compiler_params=_params(),
        name="ffn",
    )(*args)


def _rope_tables():
    rows = DEC_SEQ // GRID_W
    row = np.repeat(np.arange(rows, dtype=np.float32), GRID_W)
    col = np.tile(np.arange(GRID_W, dtype=np.float32), rows)
    inv = (1.0 / (np.float32(ROPE_THETA) ** (np.arange(N_FREQ, dtype=np.float32) / N_FREQ))).astype(np.float32)
    ang_r, ang_c = row[:, None] * inv, col[:, None] * inv
    cr, sr, cc, sc = np.cos(ang_r), np.sin(ang_r), np.cos(ang_c), np.sin(ang_c)
    cos = np.concatenate([cr, cr, cc, cc], axis=1)
    sin = np.concatenate([-sr, sr, -sc, sc], axis=1)
    reps = LANES // HEAD_DIM
    return (jnp.asarray(np.tile(cos, (1, reps)), F32), jnp.asarray(np.tile(sin, (1, reps)), F32))


def kernel(x_prompt, x_sample, cache_k_attn, cache_v_attn, cache_k_win, cache_v_win, c, c_ctx,
           w_mod, b_mod, g_mix, g_ffn, w_in, gq_attn, gk_attn, gq_win, gk_win, sink_win,
           conv_w, pool_w, pool_scale, w_branch, w_out, w_gate_up, w_down):
    assert x_prompt.shape == (BATCH, SEQ, D_MODEL) and x_sample.shape == (DEC_BATCH, DEC_SEQ, D_MODEL)
    assert cache_k_attn.shape == (DEC_BATCH, DEPTH, PAST_LEN, N_KV_HEADS, HEAD_DIM)
    assert w_in.shape == (DEPTH, D_MODEL, (COL_GATE + N_BRANCH * D_MODEL // TN) * TN)
    assert w_gate_up.shape == (DEPTH, D_MODEL, 2 * D_FF) and w_down.shape == (DEPTH, D_FF, D_MODEL)
    assert all(a.dtype == F32 for a in (x_prompt, x_sample, cache_k_attn, w_in, w_mod, w_gate_up, w_down))
    cond8 = jnp.zeros((COND_ROWS, D_MODEL), F32).at[0].set(c_ctx).at[1:1 + DEC_BATCH].set(c)
    modc = _modulation(cond8, w_mod, b_mod, 0)

    cos, sin = _rope_tables()
    gains = jnp.tile(jnp.stack([gq_attn, gq_win, gk_attn, gk_win], axis=1), (1, 1, TN // HEAD_DIM))
    gains = gains.reshape(DEPTH, 4, 1, TN)
    caches = [a.transpose(0, 1, 3, 4, 2).reshape(DEC_BATCH, DEPTH, KV_W, PAST_LEN)
              for a in (cache_k_attn, cache_v_attn, cache_k_win, cache_v_win)]

    xs = (x_prompt.reshape(P_ROWS, D_MODEL), x_sample.reshape(S_ROWS, D_MODEL))
    new_kv = []
    for l in range(DEPTH):
        q, kv, conv, pool, new, *stacked = _proj(xs, g_mix, modc, w_in, gains, cos, sin,
                                                 conv_w, pool_w, pool_scale, l)
        x = stacked[0] if stacked else xs[0]
        new_kv.append(new)
        att = _attention(sink_win, q, kv, *caches, l)
        x = _mixer_out(x, g_mix, modc, att, conv, pool, w_in, w_branch, w_out, l)
        if l + 1 < DEPTH:
            x, modc_next = _ffn(x, g_ffn, modc, w_gate_up, w_down, l, cond8, w_mod, b_mod)
            modc = modc_next
            xs = (x,)
        else:
            y_prompt, y_sample, *by_kind = _ffn(x, g_ffn, modc, w_gate_up, w_down, l, new_kv=new_kv)

    outs = [a.reshape(BATCH, DEPTH, N_KV_HEADS, HEAD_DIM, SEQ).transpose(0, 1, 4, 2, 3) for a in by_kind]
    return (y_prompt.reshape(BATCH, SEQ, D_MODEL), y_sample.reshape(DEC_BATCH, DEC_SEQ, D_MODEL), *outs)
```

```python
import functools

import numpy as np
import jax
import jax.numpy as jnp
from jax import lax
from jax.experimental import pallas as pl
from jax.experimental.pallas import tpu as pltpu

D_MODEL = 1024
BATCH = 16
SEQ = 256
DEPTH = 4
DEC_BATCH = 2
DEC_SEQ = 1024
PAST_LEN = 512
GRID_W = 64
HEAD_DIM = 64
N_Q_HEADS = 8
N_KV_HEADS = 2
ATT_W = N_Q_HEADS * HEAD_DIM
KV_W = N_KV_HEADS * HEAD_DIM
N_FREQ = HEAD_DIM // 4
ROPE_THETA = 10000.0
CONV_W = 512
POOL_W = 512
POOL_SIZES = (2, 4, 8, 16)
N_BRANCH = 4
BRANCH_W = 512
D_FF = 2816
WINDOW = 128
EPS = 1e-6
NEG = -1e30

P_ROWS = BATCH * SEQ
S_ROWS = DEC_BATCH * DEC_SEQ
ROWS = P_ROWS + S_ROWS
CHUNK = 1024
N_CHUNKS = ROWS // CHUNK
N_P_CHUNKS = P_ROWS // CHUNK
SEQ_PER_CHUNK = CHUNK // SEQ
TN = 256
LANES = 128
VMEM_LIMIT = 58 * 1024 * 1024

COL_QA, COL_KVA, COL_QC, COL_KVC = 0, 2, 3, 5
COL_U, COL_GB, COL_GC, COL_PV, COL_GATE = 6, 8, 10, 12, 14

F32 = jnp.float32
BF16 = jnp.bfloat16

assert SEQ & (SEQ - 1) == 0 and CHUNK % SEQ == 0 and DEC_SEQ == CHUNK and P_ROWS % CHUNK == 0
assert HEAD_DIM * 2 == LANES and KV_W == LANES and ATT_W % TN == 0 and POOL_W == LANES * len(POOL_SIZES)
assert N_Q_HEADS // N_KV_HEADS == 4 and COL_GATE * TN == 2 * ATT_W + 4 * KV_W + 3 * CONV_W + POOL_W


def _params():
    return pltpu.CompilerParams(dimension_semantics=("arbitrary",), vmem_limit_bytes=VMEM_LIMIT)


def _dot(a, b):
    return jnp.dot(a, b, preferred_element_type=F32)


def _dot_nt(a, b):
    return lax.dot_general(a, b, (((1,), (1,)), ((), ())), preferred_element_type=F32)


def _sigmoid(x):
    return 1.0 / (1.0 + jnp.exp(-x))


def _norm_modulate(x, g, scale, shift):
    y = x * lax.rsqrt(jnp.mean(x * x, axis=-1, keepdims=True) + EPS)
    return ((y * g) * (1.0 + scale) + shift).astype(BF16)


def _head_segments(width):
    r = lax.broadcasted_iota(jnp.int32, (width, width), 0) // HEAD_DIM
    c = lax.broadcasted_iota(jnp.int32, (width, width), 1) // HEAD_DIM
    return jnp.where(r == c, 1.0 / HEAD_DIM, 0.0).astype(BF16)


def _head_mean_square(y, seg):
    return _dot((y * y).astype(BF16), seg)


def _swap16(x):
    lane = lax.broadcasted_iota(jnp.int32, x.shape, 1)
    fwd = pltpu.roll(x, LANES - 16, axis=1)
    bwd = pltpu.roll(x, 16, axis=1)
    return jnp.where((lane & 16) == 0, fwd, bwd)


def _rope128(x, cos, sin):
    return x * cos + _swap16(x) * sin


N_MOD = 6 * D_MODEL
COND_ROWS = 8
MOD_TN = 768
MOD_TILES = N_MOD // MOD_TN


def _cond_row(chunk):
    return jnp.maximum(chunk - (N_P_CHUNKS - 1), 0)


def _mod_tile(cond_ref, w_ref, b_ref, o_ref):
    c = cond_ref[...]
    s = (c * _sigmoid(c)).astype(BF16)
    res = _dot(s, w_ref[...].astype(BF16)) + b_ref[...]
    for r in range(COND_ROWS):
        o_ref[r] = res[r:r + 1, :]


def _modulation(cond8, w_mod, b_mod, l):
    tn = 2 * MOD_TN
    return pl.pallas_call(
        _mod_tile,
        grid=(N_MOD // tn,),
        in_specs=[pl.BlockSpec((COND_ROWS, D_MODEL), lambda j: (0, 0)),
                  pl.BlockSpec((None, D_MODEL, tn), lambda j: (l, 0, j)),
                  pl.BlockSpec((None, 1, tn), lambda j: (l, 0, j))],
        out_specs=pl.BlockSpec((COND_ROWS, 1, tn), lambda j: (0, 0, j)),
        out_shape=jax.ShapeDtypeStruct((COND_ROWS, 1, N_MOD), F32),
        compiler_params=_params(),
        name="modulation",
    )(cond8, w_mod, b_mod.reshape(DEPTH, 1, N_MOD))


PJ_TILES = COL_GATE


def _proj_loads(first):
    per_load = 1 if first else 7
    return per_load, PJ_TILES // per_load


def _seq_pos(latent, width):
    row = lax.broadcasted_iota(jnp.int32, (CHUNK, width), 0)
    return row if latent else row & (SEQ - 1)


def _shift_rows(x, k, pos, seq_len):
    rolled = pltpu.roll(x, (-k) % CHUNK, axis=0)
    ok = (pos + k >= 0) & (pos + k < seq_len)
    return jnp.where(ok, rolled, 0.0)


def _window_mean_minus_token(p, half, pos, seq_len):
    fwd = p
    bwd = _shift_rows(p, -1, pos, seq_len)
    m = 1
    while m < half:
        fwd = fwd + _shift_rows(fwd, m, pos, seq_len)
        bwd = bwd + _shift_rows(bwd, -m, pos, seq_len)
        m *= 2
    count = jnp.minimum(pos + half, seq_len) - jnp.maximum(pos - half, 0)
    return (fwd + bwd) / count.astype(F32) - p


def _proj_chunk(latent, x_ref, g_ref, sh_ref, sc_ref, gain_ref, cos_ref, sin_ref, cw_ref, pw_ref,
                ps_ref, q_ref, kv_ref, conv_ref, pool_ref, new_ref, w_s, h_s):
    h_s[...] = _norm_modulate(x_ref[...], g_ref[...], sc_ref[...], sh_ref[...])
    seq_len = DEC_SEQ if latent else SEQ
    seg_q = _head_segments(TN)
    seg_k = _head_segments(KV_W)
    pos_wide = _seq_pos(latent, TN)
    pos = _seq_pos(latent, LANES)

    def q_tile(mixer, j, y):
        y = y * lax.rsqrt(_head_mean_square(y, seg_q) + EPS) * (gain_ref[mixer] * (HEAD_DIM ** -0.5))
        for s in range(TN // LANES):
            part = y[:, s * LANES:(s + 1) * LANES]
            if latent:
                part = _rope128(part, cos_ref[...], sin_ref[...])
            out_col = mixer * ATT_W + j * TN + s * LANES
            q_ref[:, out_col:out_col + LANES] = part.astype(BF16)

    def kv_tile(mixer, y):
        k = y[:, :KV_W]
        v = y[:, KV_W:]
        k = k * lax.rsqrt(_head_mean_square(k, seg_k) + EPS) * gain_ref[2 + mixer][:, :KV_W]
        if latent:
            k = _rope128(k, cos_ref[...], sin_ref[...])
        else:
            for b in range(SEQ_PER_CHUNK):
                new_ref[b, 2 * mixer] = k[b * SEQ:(b + 1) * SEQ, :].T
                new_ref[b, 2 * mixer + 1] = v[b * SEQ:(b + 1) * SEQ, :].T
        kv_ref[:, 2 * mixer * KV_W:(2 * mixer + 1) * KV_W] = k.astype(BF16)
        kv_ref[:, (2 * mixer + 1) * KV_W:(2 * mixer + 2) * KV_W] = v.astype(BF16)

    def conv_tile(j, gc, u, gb):
        cols = slice(j * TN, (j + 1) * TN)
        z = gc * u
        y = (_shift_rows(z, -1, pos_wide, seq_len) * cw_ref[0:1, cols] + z * cw_ref[1:2, cols]
             + _shift_rows(z, 1, pos_wide, seq_len) * cw_ref[2:3, cols])
        conv_ref[:, cols] = (gb * y).astype(BF16)

    def pool_tile(j, p):
        g0 = j * (TN // LANES)
        ys = [_window_mean_minus_token(p[:, s * LANES:(s + 1) * LANES], POOL_SIZES[g0 + s] // 2, pos, seq_len)
              for s in range(TN // LANES)]
        zero = jnp.zeros((LANES, LANES), BF16)
        w = jnp.concatenate([jnp.concatenate([pw_ref[g0].astype(BF16), zero], axis=1),
                             jnp.concatenate([zero, pw_ref[g0 + 1].astype(BF16)], axis=1)], axis=0)
        cols = slice(j * TN, (j + 1) * TN)
        y = jnp.concatenate(ys, axis=1).astype(BF16)
        pool_ref[:, cols] = (_dot(y, w) * ps_ref[:, cols]).astype(BF16)

    tasks = []
    for j in range(POOL_W // TN):
        tasks.append(((COL_PV + j,), functools.partial(pool_tile, j)))
        tasks.append(((COL_GC + j, COL_U + j, COL_GB + j), functools.partial(conv_tile, j)))
    for mixer, col in enumerate((COL_QA, COL_QC)):
        for j in range(ATT_W // TN):
            tasks.append(((col + j,), functools.partial(q_tile, mixer, j)))
    for mixer, col in enumerate((COL_KVA, COL_KVC)):
        tasks.append(((col,), functools.partial(kv_tile, mixer)))

    def matmuls(tiles):
        return [_dot(h_s[...], w_s[t]) for t in tiles]

    ys = matmuls(tasks[0][0])
    for i, (_, epilogue) in enumerate(tasks):
        nxt = matmuls(tasks[i + 1][0]) if i + 1 < len(tasks) else None
        epilogue(*ys)
        ys = nxt


def _proj_kernel(*refs, first):
    if first:
        xp_ref, xs_ref, *refs = refs
    else:
        xp_ref = xs_ref = refs[0]
        refs = refs[1:]
    (g_ref, sh_ref, sc_ref, w_ref, gain_ref, cos_ref, sin_ref, cw_ref, pw_ref, ps_ref,
     q_ref, kv_ref, conv_ref, pool_ref, new_ref, *refs) = refs
    xcat_ref = refs[0] if first else None
    w_s, h_s = refs[-2:]
    per_load, n_loads = _proj_loads(first)
    s = pl.program_id(0)
    rest = (g_ref, sh_ref, sc_ref, gain_ref, cos_ref, sin_ref, cw_ref, pw_ref, ps_ref,
            q_ref, kv_ref, conv_ref, pool_ref, new_ref, w_s, h_s)

    @pl.when(s < n_loads)
    def _stream():
        for j in range(per_load):
            w_s[per_load * s + j] = w_ref[:, j * TN:(j + 1) * TN].astype(BF16)

    @pl.when((s >= n_loads) & (s < n_loads + N_P_CHUNKS))
    def _prompt():
        if first:
            xcat_ref[...] = xp_ref[...]
        _proj_chunk(False, xp_ref, *rest)

    @pl.when(s >= n_loads + N_P_CHUNKS)
    def _latent():
        if first:
            xcat_ref[...] = xs_ref[...]
        _proj_chunk(True, xs_ref, *rest)


def _proj(xs, g_mix, modc, w_in, gains, cos, sin, conv_w, pool_w, pool_scale, l):
    first = len(xs) == 2
    per_load, n_loads = _proj_loads(first)

    def chunk(s):
        return jnp.clip(s - n_loads, 0, N_CHUNKS - 1)

    def p_chunk(s):
        return jnp.clip(s - n_loads, 0, N_P_CHUNKS - 1)

    def s_chunk(s):
        return jnp.clip(s - n_loads - N_P_CHUNKS, 0, N_CHUNKS - N_P_CHUNKS - 1)

    def mod_spec(k):
        return pl.BlockSpec((None, 1, D_MODEL), lambda s: (_cond_row(chunk(s)), 0, k))

    def const(shape):
        return pl.BlockSpec(shape, lambda s: (l,) + (0,) * (len(shape) - 1))

    def rows_out(width):
        return pl.BlockSpec((CHUNK, width), lambda s: (chunk(s), 0))

    if first:
        x_specs = [pl.BlockSpec((CHUNK, D_MODEL), lambda s: (p_chunk(s), 0)),
                   pl.BlockSpec((CHUNK, D_MODEL), lambda s: (s_chunk(s), 0))]
    else:
        x_specs = [rows_out(D_MODEL)]
    out_specs = [rows_out(2 * ATT_W), rows_out(4 * KV_W), rows_out(CONV_W), rows_out(POOL_W),
                 pl.BlockSpec((SEQ_PER_CHUNK, 4, KV_W, SEQ), lambda s: (p_chunk(s), 0, 0, 0))]
    out_shape = [jax.ShapeDtypeStruct((ROWS, 2 * ATT_W), BF16),
                 jax.ShapeDtypeStruct((ROWS, 4 * KV_W), BF16),
                 jax.ShapeDtypeStruct((ROWS, CONV_W), BF16),
                 jax.ShapeDtypeStruct((ROWS, POOL_W), BF16),
                 jax.ShapeDtypeStruct((BATCH, 4, KV_W, SEQ), F32)]
    if first:
        out_specs.append(rows_out(D_MODEL))
        out_shape.append(jax.ShapeDtypeStruct((ROWS, D_MODEL), F32))
    return pl.pallas_call(
        functools.partial(_proj_kernel, first=first),
        grid=(n_loads + N_CHUNKS,),
        in_specs=x_specs + [
            const((None, 1, D_MODEL)), mod_spec(0), mod_spec(1),
            pl.BlockSpec((None, D_MODEL, per_load * TN), lambda s: (l, 0, jnp.minimum(s, n_loads - 1))),
            const((None, 4, 1, TN)),
            pl.BlockSpec((DEC_SEQ, LANES), lambda s: (0, 0)),
            pl.BlockSpec((DEC_SEQ, LANES), lambda s: (0, 0)),
            const((None, 3, CONV_W)), const((None, len(POOL_SIZES), LANES, LANES)),
            const((None, 1, POOL_W))],
        out_specs=out_specs,
        out_shape=out_shape,
        scratch_shapes=[pltpu.VMEM((PJ_TILES, D_MODEL, TN), BF16),
                        pltpu.VMEM((CHUNK, D_MODEL), BF16)],
        compiler_params=_params(),
        name="proj",
    )(*xs, g_mix.reshape(DEPTH, 1, D_MODEL), modc, modc, w_in, gains, cos, sin,
      conv_w, pool_w, pool_scale.reshape(DEPTH, 1, POOL_W))


TQ = 256
N_QT = DEC_SEQ // TQ
WIN_KEYS = TQ + 2 * WINDOW
PAD_SEQ = WINDOW + DEC_SEQ + WINDOW


def _split_lanes(x, ones=False):
    xr = pltpu.roll(x, HEAD_DIM, axis=1)
    lane = lax.broadcasted_iota(jnp.int32, x.shape, 1)
    low = lane < HEAD_DIM
    fill_lo = jnp.where(lane == HEAD_DIM, 1.0, 0.0) if ones else 0.0
    fill_hi = jnp.where(lane == 0, 1.0, 0.0) if ones else 0.0
    return (jnp.where(low, x, fill_lo).astype(BF16), jnp.where(low, fill_hi, xr).astype(BF16),
            jnp.where(low, xr, fill_lo).astype(BF16), jnp.where(low, fill_hi, x).astype(BF16))


def _split_rows(xt, ones=False):
    z = jnp.zeros((HEAD_DIM, xt.shape[1]), F32)
    if ones:
        z = jnp.where(lax.broadcasted_iota(jnp.int32, z.shape, 0) == 0, 1.0, z)
    h0, h1 = xt[:HEAD_DIM], xt[HEAD_DIM:]
    return tuple(jnp.concatenate(p, axis=0).astype(BF16) for p in ((h0, z), (z, h0), (h1, z), (z, h1)))


def _attend_pairs(tasks, denom_from_values=False):
    units = [(t, parity) for t in range(len(tasks)) for parity in range(2)]

    def value(operand):
        return operand() if callable(operand) else operand

    def scores_of(unit):
        t, parity = unit
        q, pieces, _, _ = tasks[t]
        q = value(q)
        out = []
        for k_lo, k_hi, k_t, _, _, _, mask in pieces:
            k = value(k_hi if parity else k_lo)
            s = _dot(q, k) if k_t else _dot_nt(q, k)
            if mask is not None:
                s = jnp.where(mask, s, NEG)
            out.append(s)
        return out

    scores = scores_of(units[0])
    even = None
    for i, (t, parity) in enumerate(units):
        nxt = scores_of(units[i + 1]) if i + 1 < len(units) else None
        _, pieces, sinks, write = tasks[t]
        m = functools.reduce(jnp.maximum, [jnp.max(s, axis=1, keepdims=True) for s in scores])
        if sinks is not None:
            m = jnp.maximum(m, sinks[parity])
        acc = None
        denom = None
        for s, (_, _, _, v_lo, v_hi, v_t, _) in zip(scores, pieces):
            p = jnp.exp(s - m)
            if not denom_from_values:
                d = jnp.sum(p, axis=1, keepdims=True)
                denom = d if denom is None else denom + d
            v = value(v_hi if parity else v_lo)
            a = _dot_nt(p.astype(BF16), v) if v_t else _dot(p.astype(BF16), v)
            acc = a if acc is None else acc + a
        if denom_from_values:
            ones_lane = 0 if parity else HEAD_DIM
            denom = acc[:, ones_lane:ones_lane + 1]
        if sinks is not None:
            denom = denom + jnp.exp(sinks[parity] - m)
        acc = acc / denom
        if denom_from_values:
            lane = lax.broadcasted_iota(jnp.int32, acc.shape, 1)
            acc = jnp.where((lane >= HEAD_DIM) if parity else (lane < HEAD_DIM), acc, 0.0)
        if parity == 0:
            even = acc
        else:
            write(even + acc)
        scores = nxt


def _attn_prompt_seq(rows, sink_ref, q_ref, kv_ref, o_ref, layer):
    def writer(col):
        def write(out):
            o_ref[rows, col:col + LANES] = out.astype(BF16)
        return write

    tasks = []
    for mixer in range(2):
        k = _split_lanes(kv_ref[rows, 2 * mixer * KV_W:(2 * mixer + 1) * KV_W].astype(F32))
        v = _split_lanes(kv_ref[rows, (2 * mixer + 1) * KV_W:(2 * mixer + 2) * KV_W].astype(F32))
        for pair in range(N_Q_HEADS // 2):
            lo = 2 * (pair // 2)
            piece = (k[lo], k[lo + 1], False, v[lo], v[lo + 1], False, None)
            sinks = None
            if mixer == 1:
                sinks = (sink_ref[layer, 2 * pair], sink_ref[layer, 2 * pair + 1])
            col = mixer * ATT_W + pair * LANES
            tasks.append((q_ref[rows, col:col + LANES], [piece], sinks, writer(col)))
    _attend_pairs(tasks)


def _attn_latent_fill(kv_ref, cka_ref, cva_ref, ckc_ref, cvc_ref, ctx_s, ka_s, va_s, kc_s, vc_s):
    for i, ref in enumerate((cka_ref, cva_ref, ckc_ref, cvc_ref)):
        for j, part in enumerate(_split_rows(ref[...], ones=i % 2 == 1)):
            ctx_s[4 * i + j] = part
    for dst, col in ((ka_s, 0), (va_s, KV_W)):
        for j, part in enumerate(_split_lanes(kv_ref[:, col:col + KV_W].astype(F32), ones=dst is va_s)):
            dst[j] = part
    zeros = jnp.zeros((WINDOW, LANES), BF16)
    for dst, col in ((kc_s, 2 * KV_W), (vc_s, 3 * KV_W)):
        for j, part in enumerate(_split_lanes(kv_ref[:, col:col + KV_W].astype(F32), ones=dst is vc_s)):
            dst[j, 0:WINDOW, :] = zeros
            dst[j, WINDOW:WINDOW + DEC_SEQ, :] = part
            dst[j, WINDOW + DEC_SEQ:PAD_SEQ, :] = zeros


def _attn_latent_tile(qt, sink_ref, q_ref, o_ref, ctx_s, ka_s, va_s, kc_s, vc_s, layer):
    q0 = qt * TQ
    rows = pl.ds(pl.multiple_of(q0, TQ), TQ)
    win = pl.ds(pl.multiple_of(q0, TQ), WIN_KEYS)
    r = lax.broadcasted_iota(jnp.int32, (TQ, WIN_KEYS), 0)
    jk = lax.broadcasted_iota(jnp.int32, (TQ, WIN_KEYS), 1)
    kpos = q0 - WINDOW + jk
    band = jnp.where((jk - r >= 0) & (jk - r <= 2 * WINDOW), kpos, -1)
    mask = (band >= 0) & (band < DEC_SEQ)

    def writer(col):
        def write(out):
            o_ref[rows, col:col + LANES] = out.astype(BF16)
        return write

    def whole(ref, i):
        return lambda: ref[i]

    def window(ref, i):
        return lambda: ref[i, win, :]

    def queries(col):
        return lambda: q_ref[rows, col:col + LANES]

    tasks = []
    for pair in range(N_Q_HEADS // 2):
        lo = 2 * (pair // 2)
        hi = lo + 1
        col = pair * LANES
        ctx = (whole(ctx_s, lo), whole(ctx_s, hi), True, whole(ctx_s, 4 + lo), whole(ctx_s, 4 + hi), True, None)
        cur = (whole(ka_s, lo), whole(ka_s, hi), False, whole(va_s, lo), whole(va_s, hi), False, None)
        tasks.append((queries(col), [ctx, cur], None, writer(col)))

        col = ATT_W + pair * LANES
        ctx = (whole(ctx_s, 8 + lo), whole(ctx_s, 8 + hi), True,
               whole(ctx_s, 12 + lo), whole(ctx_s, 12 + hi), True, None)
        near = (window(kc_s, lo), window(kc_s, hi), False, window(vc_s, lo), window(vc_s, hi), False, mask)
        sinks = (sink_ref[layer, 2 * pair], sink_ref[layer, 2 * pair + 1])
        tasks.append((queries(col), [ctx, near], sinks, writer(col)))
    _attend_pairs(tasks, denom_from_values=True)


def _attn_kernel(sink_ref, q_ref, kv_ref, cka_ref, cva_ref, ckc_ref, cvc_ref, o_ref,
                 ctx_s, ka_s, va_s, kc_s, vc_s, *, layer):
    s = pl.program_id(0)

    @pl.when(s < N_P_CHUNKS)
    def _prompt():
        def seq(b, carry):
            _attn_prompt_seq(pl.ds(pl.multiple_of(b * SEQ, SEQ), SEQ), sink_ref, q_ref, kv_ref, o_ref, layer)
            return carry
        lax.fori_loop(0, SEQ_PER_CHUNK, seq, 0)

    @pl.when(s >= N_P_CHUNKS)
    def _latent():
        _attn_latent_fill(kv_ref, cka_ref, cva_ref, ckc_ref, cvc_ref, ctx_s, ka_s, va_s, kc_s, vc_s)

        def tile(qt, carry):
            _attn_latent_tile(qt, sink_ref, q_ref, o_ref, ctx_s, ka_s, va_s, kc_s, vc_s, layer)
            return carry
        lax.fori_loop(0, N_QT, tile, 0)


def _attention(sink, q, kv, cka, cva, ckc, cvc, l):
    cache_spec = pl.BlockSpec((None, None, KV_W, PAST_LEN),
                              lambda s: (jnp.maximum(s - N_P_CHUNKS, 0), l, 0, 0))
    return pl.pallas_call(
        functools.partial(_attn_kernel, layer=l),
        grid=(N_CHUNKS,),
        in_specs=[pl.BlockSpec(memory_space=pltpu.SMEM),
                  pl.BlockSpec((CHUNK, 2 * ATT_W), lambda s: (s, 0)),
                  pl.BlockSpec((CHUNK, 4 * KV_W), lambda s: (s, 0)),
                  cache_spec, cache_spec, cache_spec, cache_spec],
        out_specs=pl.BlockSpec((CHUNK, 2 * ATT_W), lambda s: (s, 0)),
        out_shape=jax.ShapeDtypeStruct((ROWS, 2 * ATT_W), BF16),
        scratch_shapes=[pltpu.VMEM((16, KV_W, PAST_LEN), BF16),
                        pltpu.VMEM((4, DEC_SEQ, LANES), BF16), pltpu.VMEM((4, DEC_SEQ, LANES), BF16),
                        pltpu.VMEM((4, PAD_SEQ, LANES), BF16), pltpu.VMEM((4, PAD_SEQ, LANES), BF16)],
        compiler_params=_params(),
        name="attention",
    )(sink, q, kv, cka, cva, ckc, cvc)


MG_ROWS = 512
MG_CHUNKS = ROWS // MG_ROWS
MG_TILES = D_MODEL // TN
MG_STEPS = MG_TILES + MG_CHUNKS


def _mixer_out_kernel(x_ref, xn_ref, g_ref, sh_ref, sc_ref, gate_ref, att_ref, conv_ref, pool_ref,
                      wg0, wg1, wg2, wg3, wb0, wb1, wb2, wb3, wo_ref, o_ref,
                      wg_s, wb_s, wo_s, h_s, mix_s):
    s = pl.program_id(0)

    def norm():
        h_s[...] = _norm_modulate(xn_ref[...], g_ref[...], sc_ref[...], sh_ref[...])

    @pl.when(s == 0)
    def _first_norm():
        norm()

    def mix_tile(wgs, wbs):
        h = h_s[...]
        branches = (att_ref[:, :ATT_W], conv_ref[...], att_ref[:, ATT_W:], pool_ref[...])
        acc = None
        for br, wg, wb in zip(branches, wgs, wbs):
            term = _sigmoid(_dot(h, wg)) * _dot(br, wb)
            acc = term if acc is None else acc + term
        return acc.astype(BF16)

    def out():
        o_ref[...] = x_ref[...] + gate_ref[...] * _dot(mix_s[...], wo_s[...])

    @pl.when(s < MG_TILES)
    def _stream():
        wgs = [r[...].astype(BF16) for r in (wg0, wg1, wg2, wg3)]
        wbs = [r[...].astype(BF16) for r in (wb0, wb1, wb2, wb3)]
        for k in range(N_BRANCH):
            wg_s[s * N_BRANCH + k] = wgs[k]
            wb_s[s * N_BRANCH + k] = wbs[k]
        wo_s[pl.ds(pl.multiple_of(s * TN, TN), TN), :] = wo_ref[...].astype(BF16)
        v = mix_tile(wgs, wbs)
        for n in range(MG_TILES):
            @pl.when(s == n)
            def _store(n=n):
                mix_s[:, n * TN:(n + 1) * TN] = v

    @pl.when((s >= MG_TILES) & (s < MG_STEPS - 1))
    def _chunk():
        out()
        norm()
        for n in range(MG_TILES):
            mix_s[:, n * TN:(n + 1) * TN] = mix_tile(
                [wg_s[n * N_BRANCH + k] for k in range(N_BRANCH)],
                [wb_s[n * N_BRANCH + k] for k in range(N_BRANCH)])

    @pl.when(s == MG_STEPS - 1)
    def _last_chunk_out():
        out()


def _mixer_out(x, g_mix, modc, att, conv, pool, w_in, w_branch, w_out, l):
    def chunk(s):
        return jnp.maximum(s - MG_TILES, 0)

    def mix_chunk(s):
        return jnp.clip(s - (MG_TILES - 1), 0, MG_CHUNKS - 1)

    def tile(s):
        return jnp.minimum(s, MG_TILES - 1)

    def mod_spec(k, which):
        return pl.BlockSpec((None, 1, D_MODEL), lambda s: (_cond_row(which(s) * MG_ROWS // CHUNK), 0, k))

    def rows_in(width):
        return pl.BlockSpec((MG_ROWS, width), lambda s: (mix_chunk(s), 0))

    gate_specs = [pl.BlockSpec((None, D_MODEL, TN), functools.partial(
        lambda s, k: (l, 0, COL_GATE + MG_TILES * k + tile(s)), k=k)) for k in range(N_BRANCH)]
    br_specs = [pl.BlockSpec((None, None, BRANCH_W, TN), functools.partial(
        lambda s, k: (l, k, 0, tile(s)), k=k)) for k in range(N_BRANCH)]
    return pl.pallas_call(
        _mixer_out_kernel,
        grid=(MG_STEPS,),
        in_specs=[pl.BlockSpec((MG_ROWS, D_MODEL), lambda s: (chunk(s), 0)),
                  pl.BlockSpec((MG_ROWS, D_MODEL), lambda s: (mix_chunk(s), 0)),
                  pl.BlockSpec((None, 1, D_MODEL), lambda s: (l, 0, 0)),
                  mod_spec(0, mix_chunk), mod_spec(1, mix_chunk), mod_spec(2, chunk),
                  rows_in(2 * ATT_W), rows_in(CONV_W), rows_in(POOL_W)] + gate_specs + br_specs
                 + [pl.BlockSpec((None, TN, D_MODEL), lambda s: (l, tile(s), 0))],
        out_specs=pl.BlockSpec((MG_ROWS, D_MODEL), lambda s: (chunk(s), 0)),
        out_shape=jax.ShapeDtypeStruct((ROWS, D_MODEL), F32),
        scratch_shapes=[pltpu.VMEM((MG_TILES * N_BRANCH, D_MODEL, TN), BF16),
                        pltpu.VMEM((MG_TILES * N_BRANCH, BRANCH_W, TN), BF16),
                        pltpu.VMEM((D_MODEL, D_MODEL), BF16),
                        pltpu.VMEM((MG_ROWS, D_MODEL), BF16),
                        pltpu.VMEM((MG_ROWS, D_MODEL), BF16)],
        compiler_params=_params(),
        name="mixer_out",
    )(x, x, g_mix.reshape(DEPTH, 1, D_MODEL), modc, modc, modc, att, conv, pool,
      *([w_in] * N_BRANCH), *([w_branch] * N_BRANCH), w_out)


FF_ROWS = 512
FF_CHUNKS = ROWS // FF_ROWS
FF_P_CHUNKS = P_ROWS // FF_ROWS
FF_TILES = D_FF // TN
FF_STEPS = FF_TILES + FF_CHUNKS
FF_FIRST_LATENT_STEP = FF_TILES + FF_P_CHUNKS
assert MOD_TILES <= FF_CHUNKS - 1
FF_REGROUP_EARLY = BATCH - FF_CHUNKS
assert 0 <= FF_REGROUP_EARLY <= FF_TILES


def _ffn_kernel(x_ref, xn_ref, g_ref, sh_ref, sc_ref, gate_ref, wa_ref, wb_ref, wd_ref, *rest, last):
    if last:
        new_refs = rest[:DEPTH]
        op_ref, os_ref = rest[DEPTH:DEPTH + 2]
        kind_refs = rest[DEPTH + 2:DEPTH + 6]
        wa_s, wb_s, wd_s, h_s, act_s = rest[DEPTH + 6:]
    else:
        cond_ref, wm_ref, bm_ref, op_ref, mod_ref, wa_s, wb_s, wd_s, h_s, act_s = rest
        os_ref = op_ref
    s = pl.program_id(0)

    def norm():
        h_s[...] = _norm_modulate(xn_ref[...], g_ref[...], sc_ref[...], sh_ref[...])

    def up(wa, wb):
        h = h_s[...]
        a = _dot(h, wa)
        return (a * _sigmoid(a) * _dot(h, wb)).astype(BF16)

    def down():
        return x_ref[...] + gate_ref[...] * _dot(act_s[...], wd_s[...])

    def regroup_new_cache():
        for kind, dst in enumerate(kind_refs):
            for depth, src in enumerate(new_refs):
                dst[0, depth] = src[0, kind]

    def store(out):
        if last:
            @pl.when(s < FF_FIRST_LATENT_STEP)
            def _store_prompt():
                op_ref[...] = out

            @pl.when(s >= FF_FIRST_LATENT_STEP)
            def _store_latent():
                os_ref[...] = out
        else:
            op_ref[...] = out

    @pl.when(s == 0)
    def _first_norm():
        norm()

    @pl.when(s < FF_TILES)
    def _stream():
        wa = wa_ref[...].astype(BF16)
        wb = wb_ref[...].astype(BF16)
        wa_s[s] = wa
        wb_s[s] = wb
        wd_s[pl.ds(pl.multiple_of(s * TN, TN), TN), :] = wd_ref[...].astype(BF16)
        v = up(wa, wb)
        for j in range(FF_TILES):
            @pl.when(s == j)
            def _store(j=j):
                act_s[:, j * TN:(j + 1) * TN] = v

    @pl.when((s >= FF_TILES) & (s < FF_STEPS - 1))
    def _chunk():
        out = down()
        norm()
        store(out)
        for j in range(FF_TILES):
            act_s[:, j * TN:(j + 1) * TN] = up(wa_s[j], wb_s[j])
        if last:
            regroup_new_cache()
        else:
            _mod_tile(cond_ref, wm_ref, bm_ref, mod_ref)

    @pl.when(s == FF_STEPS - 1)
    def _last_chunk_down():
        store(down())
        if last:
            regroup_new_cache()

    if last:
        @pl.when(s < FF_REGROUP_EARLY)
        def _regroup_early():
            regroup_new_cache()


def _ffn(x, g_ffn, modc, w_gate_up, w_down, l, cond8=None, w_mod=None, b_mod=None, new_kv=None):
    last = cond8 is None

    def chunk(s):
        return jnp.maximum(s - FF_TILES, 0)

    def up_chunk(s):
        return jnp.clip(s - (FF_TILES - 1), 0, FF_CHUNKS - 1)

    def tile(s):
        return jnp.minimum(s, FF_TILES - 1)

    def mod_tile(s):
        return jnp.clip(s - FF_TILES, 0, MOD_TILES - 1)

    def mod_spec(k, which):
        return pl.BlockSpec((None, 1, D_MODEL), lambda s: (_cond_row(which(s) * FF_ROWS // CHUNK), 0, k))

    in_specs = [pl.BlockSpec((FF_ROWS, D_MODEL), lambda s: (chunk(s), 0)),
                pl.BlockSpec((FF_ROWS, D_MODEL), lambda s: (up_chunk(s), 0)),
                pl.BlockSpec((None, 1, D_MODEL), lambda s: (l, 0, 0)),
                mod_spec(3, up_chunk), mod_spec(4, up_chunk), mod_spec(5, chunk),
                pl.BlockSpec((None, D_MODEL, TN), lambda s: (l, 0, tile(s))),
                pl.BlockSpec((None, D_MODEL, TN), lambda s: (l, 0, FF_TILES + tile(s))),
                pl.BlockSpec((None, TN, D_MODEL), lambda s: (l, tile(s), 0))]
    args = [x, x, g_ffn.reshape(DEPTH, 1, D_MODEL), modc, modc, modc, w_gate_up, w_gate_up, w_down]
    if last:
        def seq(s):
            return jnp.minimum(s, FF_REGROUP_EARLY - 1) + jnp.maximum(s - (FF_TILES - 1), 0)

        in_specs += [pl.BlockSpec((1, 4, KV_W, SEQ), lambda s: (seq(s), 0, 0, 0))] * DEPTH
        args += list(new_kv)
        out_specs = [pl.BlockSpec((FF_ROWS, D_MODEL), lambda s: (jnp.minimum(chunk(s), FF_P_CHUNKS - 1), 0)),
                     pl.BlockSpec((FF_ROWS, D_MODEL), lambda s: (jnp.maximum(chunk(s) - FF_P_CHUNKS, 0), 0))]
        out_shape = [jax.ShapeDtypeStruct((P_ROWS, D_MODEL), F32), jax.ShapeDtypeStruct((S_ROWS, D_MODEL), F32)]
        out_specs += [pl.BlockSpec((1, DEPTH, KV_W, SEQ), lambda s: (seq(s), 0, 0, 0))] * 4
        out_shape += [jax.ShapeDtypeStruct((BATCH, DEPTH, KV_W, SEQ), F32)] * 4
    else:
        in_specs += [pl.BlockSpec((COND_ROWS, D_MODEL), lambda s: (0, 0)),
                     pl.BlockSpec((None, D_MODEL, MOD_TN), lambda s: (l + 1, 0, mod_tile(s))),
                     pl.BlockSpec((None, 1, MOD_TN), lambda s: (l + 1, 0, mod_tile(s)))]
        args += [cond8, w_mod, b_mod.reshape(DEPTH, 1, N_MOD)]
        out_specs = [pl.BlockSpec((FF_ROWS, D_MODEL), lambda s: (chunk(s), 0)),
                     pl.BlockSpec((COND_ROWS, 1, MOD_TN), lambda s: (0, 0, mod_tile(s)))]
        out_shape = [jax.ShapeDtypeStruct((ROWS, D_MODEL), F32),
                     jax.ShapeDtypeStruct((COND_ROWS, 1, N_MOD), F32)]
    return pl.pallas_call(
        functools.partial(_ffn_kernel, last=last),
        grid=(FF_STEPS,),
        in_specs=in_specs,
        out_specs=out_specs,
        out_shape=out_shape,
        scratch_shapes=[pltpu.VMEM((FF_TILES, D_MODEL, TN), BF16),
                        pltpu.VMEM((FF_TILES, D_MODEL, TN), BF16),
                        pltpu.VMEM((D_FF, D_MODEL), BF16),
                        pltpu.VMEM((FF_ROWS, D_MODEL), BF16),
                        pltpu.VMEM((FF_ROWS, D_FF), BF16)],
        compiler_params=_params(),
        name="ffn",
    )(*args)


def _rope_tables():
    rows = DEC_SEQ // GRID_W
    row = np.repeat(np.arange(rows, dtype=np.float32), GRID_W)
    col = np.tile(np.arange(GRID_W, dtype=np.float32), rows)
    inv = (1.0 / (np.float32(ROPE_THETA) ** (np.arange(N_FREQ, dtype=np.float32) / N_FREQ))).astype(np.float32)
    ang_r, ang_c = row[:, None] * inv, col[:, None] * inv
    cr, sr, cc, sc = np.cos(ang_r), np.sin(ang_r), np.cos(ang_c), np.sin(ang_c)
    cos = np.concatenate([cr, cr, cc, cc], axis=1)
    sin = np.concatenate([-sr, sr, -sc, sc], axis=1)
    reps = LANES // HEAD_DIM
    return (jnp.asarray(np.tile(cos, (1, reps)), F32), jnp.asarray(np.tile(sin, (1, reps)), F32))


def kernel(x_prompt, x_sample, cache_k_attn, cache_v_attn, cache_k_win, cache_v_win, c, c_ctx,
           w_mod, b_mod, g_mix, g_ffn, w_in, gq_attn, gk_attn, gq_win, gk_win, sink_win,
           conv_w, pool_w, pool_scale, w_branch, w_out, w_gate_up, w_down):
    assert x_prompt.shape == (BATCH, SEQ, D_MODEL) and x_sample.shape == (DEC_BATCH, DEC_SEQ, D_MODEL)
    assert cache_k_attn.shape == (DEC_BATCH, DEPTH, PAST_LEN, N_KV_HEADS, HEAD_DIM)
    assert w_in.shape == (DEPTH, D_MODEL, (COL_GATE + N_BRANCH * D_MODEL // TN) * TN)
    assert w_gate_up.shape == (DEPTH, D_MODEL, 2 * D_FF) and w_down.shape == (DEPTH, D_FF, D_MODEL)
    assert all(a.dtype == F32 for a in (x_prompt, x_sample, cache_k_attn, w_in, w_mod, w_gate_up, w_down))
    cond8 = jnp.zeros((COND_ROWS, D_MODEL), F32).at[0].set(c_ctx).at[1:1 + DEC_BATCH].set(c)
    modc = _modulation(cond8, w_mod, b_mod, 0)

    cos, sin = _rope_tables()
    gains = jnp.tile(jnp.stack([gq_attn, gq_win, gk_attn, gk_win], axis=1), (1, 1, TN // HEAD_DIM))
    gains = gains.reshape(DEPTH, 4, 1, TN)
    caches = [a.transpose(0, 1, 3, 4, 2).reshape(DEC_BATCH, DEPTH, KV_W, PAST_LEN)
              for a in (cache_k_attn, cache_v_attn, cache_k_win, cache_v_win)]

    xs = (x_prompt.reshape(P_ROWS, D_MODEL), x_sample.reshape(S_ROWS, D_MODEL))
    new_kv = []
    for l in range(DEPTH):
        q, kv, conv, pool, new, *stacked = _proj(xs, g_mix, modc, w_in, gains, cos, sin,
                                                 conv_w, pool_w, pool_scale, l)
        x = stacked[0] if stacked else xs[0]
        new_kv.append(new)
        att = _attention(sink_win, q, kv, *caches, l)
        x = _mixer_out(x, g_mix, modc, att, conv, pool, w_in, w_branch, w_out, l)
        if l + 1 < DEPTH:
            x, modc_next = _ffn(x, g_ffn, modc, w_gate_up, w_down, l, cond8, w_mod, b_mod)
            modc = modc_next
            xs = (x,)
        else:
            y_prompt, y_sample, *by_kind = _ffn(x, g_ffn, modc, w_gate_up, w_down, l, new_kv=new_kv)

    outs = [a.reshape(BATCH, DEPTH, N_KV_HEADS, HEAD_DIM, SEQ).transpose(0, 1, 4, 2, 3) for a in by_kind]
    return (y_prompt.reshape(BATCH, SEQ, D_MODEL), y_sample.reshape(DEC_BATCH, DEC_SEQ, D_MODEL), *outs)
```
